```python
import math
import jax, jax.numpy as jnp
from jax import lax
import numpy as np

D_MODEL = 1024
BATCH = 8
SEQ = 2048
DEPTH = 1
DEC_BATCH = 128
DEC_SEQ = 1
PAST_LEN = 16384
PAGE_SIZE = 128

W_LRU = D_MODEL
LRU_BLOCKS = 8
LRU_BLK = W_LRU // LRU_BLOCKS
LRU_C = 8.0
CONV_W = 4
M_HEADS = 4
D_M = D_MODEL
M_HD = D_M // M_HEADS
CHUNK = 64
D_FF = ((8 * D_MODEL + 3 * 256 - 1) // (3 * 256)) * 256
P_DIM = 256
EPS = 1e-6
M_INIT = -1e30
SPLITS = (W_LRU, W_LRU + D_M, W_LRU + 2 * D_M, W_LRU + 2 * D_M + M_HEADS,
          W_LRU + 2 * D_M + 2 * M_HEADS, W_LRU + 2 * D_M + 2 * M_HEADS + D_MODEL)
N_IN = W_LRU + 2 * D_M + 2 * M_HEADS + 2 * D_MODEL

kernel_name = "hawk_mlstm_parallel_gated_decode_step"


def _rmsnorm(x, g):
    xf = x.astype(jnp.float32)
    y = xf * lax.rsqrt(jnp.mean(xf * xf, axis=-1, keepdims=True) + EPS)
    return (y * g.astype(jnp.float32)).astype(x.dtype)


def _causal_conv(x, buf, w, b):
    T = x.shape[1]
    xp = jnp.concatenate([buf.astype(x.dtype), x], axis=1)
    out = b + sum(xp[:, j:j + T] * w[j] for j in range(CONV_W))
    return out, xp[:, T:]


def _rg_lru(x, h0, w_a, b_a, w_x, b_x, lam):
    B, T, W = x.shape
    f32 = jnp.float32
    xb = x.reshape(B, T, LRU_BLOCKS, LRU_BLK)
    r = jax.nn.sigmoid((jnp.einsum('btnc,ncd->btnd', xb, w_a).reshape(B, T, W) + b_a).astype(f32))
    ig = jax.nn.sigmoid((jnp.einsum('btnc,ncd->btnd', xb, w_x).reshape(B, T, W) + b_x).astype(f32))
    log_a = -LRU_C * r * jax.nn.softplus(-lam.astype(f32))
    a = jnp.exp(log_a)
    u = jnp.sqrt(-jnp.expm1(2.0 * log_a)) * ig * x.astype(f32)
    u = u.at[:, 0].add(a[:, 0] * h0.astype(f32))

    def comb(e1, e2):
        a1, b1 = e1
        a2, b2 = e2
        return a1 * a2, a2 * b1 + b2

    _, h = lax.associative_scan(comb, (a, u), axis=1)
    return h.astype(x.dtype), h[:, -1]


def _mlstm_chunk(carry, xs):
    C, n, m = carry
    qc, kc, vc, ic, fc = xs
    L = qc.shape[2]
    b = jnp.cumsum(fc, axis=-1)
    causal = jnp.tril(jnp.ones((L, L), dtype=bool))
    dlog = jnp.where(causal, b[..., :, None] - b[..., None, :] + ic[..., None, :], -jnp.inf)
    m_inter = b + m[..., None]
    m_t = jnp.maximum(m_inter, jnp.max(dlog, axis=-1))
    s = jnp.einsum('bhtd,bhsd->bhts', qc, kc) * jnp.exp(dlog - m_t[..., None])
    sc = jnp.exp(m_inter - m_t)
    num = jnp.einsum('bhts,bhse->bhte', s, vc) + sc[..., None] * jnp.einsum('bhtd,bhde->bhte', qc, C)
    den = jnp.sum(s, axis=-1) + sc * jnp.einsum('bhtd,bhd->bht', qc, n)
    h = num / jnp.maximum(jnp.abs(den), jnp.exp(-m_t))[..., None]
    m_last = m_t[..., -1]
    wk = jnp.exp(b[..., -1:] - b + ic - m_last[..., None])
    dec = jnp.exp(b[..., -1] + m - m_last)
    C_new = dec[..., None, None] * C + jnp.einsum('bhs,bhsd,bhse->bhde', wk, kc, vc)
    n_new = dec[..., None] * n + jnp.einsum('bhs,bhsd->bhd', wk, kc)
    return (C_new, n_new, m_last), h


def _mlstm(xm, o_pre, i_pre, f_pre, conv_buf, C0, n0, m0, cw, cb, w_q, w_k, w_v, g):
    B, T, _ = xm.shape
    f32 = jnp.float32
    xc, new_buf = _causal_conv(xm, conv_buf, cw, cb)
    xc = jax.nn.silu(xc).reshape(B, T, M_HEADS, M_HD)
    xh = xm.reshape(B, T, M_HEADS, M_HD)
    q = jnp.einsum('bthd,hde->bhte', xc, w_q).astype(f32) * (M_HD ** -0.5)
    k = jnp.einsum('bthd,hde->bhte', xc, w_k).astype(f32)
    v = jnp.einsum('bthd,hde->bhte', xh, w_v).astype(f32)
    ig = jnp.swapaxes(i_pre.astype(f32), 1, 2)
    lf = jax.nn.log_sigmoid(jnp.swapaxes(f_pre.astype(f32), 1, 2))
    L = CHUNK if T % CHUNK == 0 else T
    nc = T // L

    def chunks(a):
        return jnp.moveaxis(a.reshape(a.shape[:2] + (nc, L) + a.shape[3:]), 2, 0)

    carry0 = (C0.astype(f32), n0.astype(f32), m0.astype(f32))
    (C, n, m), h = lax.scan(_mlstm_chunk, carry0,
                            (chunks(q), chunks(k), chunks(v), chunks(ig), chunks(lf)))
    h = jnp.moveaxis(h, 0, 2).reshape(B, M_HEADS, T, M_HD)
    h = h * lax.rsqrt(jnp.mean(h * h, axis=-1, keepdims=True) + EPS) * g[:, None, :].astype(f32)
    h = jnp.swapaxes(h, 1, 2).reshape(B, T, D_M)
    out = (jax.nn.sigmoid(o_pre.astype(f32)) * h).astype(xm.dtype)
    return out, new_buf, C.astype(xm.dtype), n.astype(xm.dtype), m.astype(xm.dtype)


def _layer(x, p, st, lw):
    conv_l, h_l, conv_m, C0, n0, m0 = st
    (g_mix, w_in, b_gates, lru_cw, lru_cb, lru_wa, lru_ba, lru_wx, lru_bx, lru_lam,
     m_cw, m_cb, w_q, w_k, w_v, m_g, w_out, g_ffn, w_gate, w_up, w_down,
     g_ple, w_ple_gate, w_ple) = lw
    xn = _rmsnorm(x, g_mix)
    z = xn @ w_in
    x_l, x_m, o_m, i_m, f_m, g_l, g_m = jnp.split(z, SPLITS, axis=-1)
    i_m = i_m + b_gates[:M_HEADS]
    f_m = f_m + b_gates[M_HEADS:]
    xl_c, new_lconv = _causal_conv(x_l, conv_l, lru_cw, lru_cb)
    y_l, new_h = _rg_lru(xl_c, h_l, lru_wa, lru_ba, lru_wx, lru_bx, lru_lam)
    y_m, new_mconv, C1, n1, m1 = _mlstm(x_m, o_m, i_m, f_m, conv_m, C0, n0, m0,
                                        m_cw, m_cb, w_q, w_k, w_v, m_g)
    merged = jax.nn.sigmoid(g_l) * y_l + jax.nn.sigmoid(g_m) * y_m
    x = x + merged @ w_out
    xn = _rmsnorm(x, g_ffn)
    x = x + (jax.nn.silu(xn @ w_gate) * (xn @ w_up)) @ w_down
    gate = jax.nn.sigmoid(_rmsnorm(x, g_ple) @ w_ple_gate)
    x = x + gate * (p @ w_ple)
    return x, (new_lconv, new_h.astype(x.dtype), new_mconv, C1, n1, m1)


def _trunk(x, p, states, weights, final_g):
    new = [[] for _ in range(len(states))]
    for l in range(DEPTH):
        st = tuple(s[l] for s in states)
        lw = tuple(w[l] for w in weights)
        x, ns = _layer(x, p[l], st, lw)
        for lst, a in zip(new, ns):
            lst.append(a)
    return _rmsnorm(x, final_g), tuple(jnp.stack(a) for a in new)


def setup_inputs(seed: int = 0) -> dict:
    key = jax.random.key(seed)
    ks = iter(jax.random.split(key, 48))
    nrm = lambda shape, s: jax.random.normal(next(ks), shape, jnp.float32) * s
    gain = lambda shape: 1.0 + nrm(shape, 0.02)
    u = jax.random.uniform(next(ks), (DEPTH, W_LRU), jnp.float32, 0.9, 0.999)
    s = u ** (1.0 / LRU_C)
    lam = jnp.log(s) - jnp.log1p(-s)
    f_bias = jnp.linspace(3.0, 6.0, M_HEADS, dtype=jnp.float32)[None, :] + nrm((DEPTH, M_HEADS), 0.1)
    b_gates = jnp.concatenate([nrm((DEPTH, M_HEADS), 0.1), f_bias], axis=-1)
    return {
        "x_prompt": nrm((BATCH, SEQ, D_MODEL), 1.0),
        "x_sample": nrm((DEC_BATCH, DEC_SEQ, D_MODEL), 1.0),
        "state_lru_conv": nrm((DEPTH, DEC_BATCH, CONV_W - 1, W_LRU), 1.0),
        "state_lru_h": nrm((DEPTH, DEC_BATCH, W_LRU), 0.5),
        "state_mlstm_conv": nrm((DEPTH, DEC_BATCH, CONV_W - 1, D_M), 1.0),
        "state_mlstm_C": nrm((DEPTH, DEC_BATCH, M_HEADS, M_HD, M_HD), 0.1),
        "state_mlstm_n": nrm((DEPTH, DEC_BATCH, M_HEADS, M_HD), 0.3),
        "state_mlstm_m": jax.random.uniform(next(ks), (DEPTH, DEC_BATCH, M_HEADS), jnp.float32, -1.0, 3.0),
        "p_prompt": nrm((DEPTH, BATCH, SEQ, P_DIM), 1.0),
        "p_sample": nrm((DEPTH, DEC_BATCH, DEC_SEQ, P_DIM), 1.0),
        "norm_mix_g": gain((DEPTH, D_MODEL)),
        "w_in": nrm((DEPTH, D_MODEL, N_IN), D_MODEL ** -0.5),
        "b_gates": b_gates,
        "lru_conv_w": nrm((DEPTH, CONV_W, W_LRU), CONV_W ** -0.5),
        "lru_conv_b": nrm((DEPTH, W_LRU), 0.02),
        "lru_w_a": nrm((DEPTH, LRU_BLOCKS, LRU_BLK, LRU_BLK), LRU_BLK ** -0.5),
        "lru_b_a": nrm((DEPTH, W_LRU), 0.02),
        "lru_w_x": nrm((DEPTH, LRU_BLOCKS, LRU_BLK, LRU_BLK), LRU_BLK ** -0.5),
        "lru_b_x": nrm((DEPTH, W_LRU), 0.02),
        "lru_lambda": lam,
        "mlstm_conv_w": nrm((DEPTH, CONV_W, D_M), CONV_W ** -0.5),
        "mlstm_conv_b": nrm((DEPTH, D_M), 0.02),
        "w_q": nrm((DEPTH, M_HEADS, M_HD, M_HD), M_HD ** -0.5),
        "w_k": nrm((DEPTH, M_HEADS, M_HD, M_HD), M_HD ** -0.5),
        "w_v": nrm((DEPTH, M_HEADS, M_HD, M_HD), M_HD ** -0.5),
        "mlstm_norm_g": gain((DEPTH, M_HEADS, M_HD)),
        "w_out": nrm((DEPTH, D_MODEL, D_MODEL), D_MODEL ** -0.5),
        "norm_ffn_g": gain((DEPTH, D_MODEL)),
        "w_ffn_gate": nrm((DEPTH, D_MODEL, D_FF), D_MODEL ** -0.5),
        "w_ffn_up": nrm((DEPTH, D_MODEL, D_FF), D_MODEL ** -0.5),
        "w_ffn_down": nrm((DEPTH, D_FF, D_MODEL), D_FF ** -0.5),
        "norm_ple_g": gain((DEPTH, D_MODEL)),
        "w_ple_gate": nrm((DEPTH, D_MODEL, D_MODEL), D_MODEL ** -0.5),
        "w_ple": nrm((DEPTH, P_DIM, D_MODEL), P_DIM ** -0.5),
        "final_norm_g": gain((D_MODEL,)),
    }


def reference(x_prompt, x_sample, state_lru_conv, state_lru_h, state_mlstm_conv,
              state_mlstm_C, state_mlstm_n, state_mlstm_m, p_prompt, p_sample,
              norm_mix_g, w_in, b_gates, lru_conv_w, lru_conv_b, lru_w_a, lru_b_a,
              lru_w_x, lru_b_x, lru_lambda, mlstm_conv_w, mlstm_conv_b, w_q, w_k, w_v,
              mlstm_norm_g, w_out, norm_ffn_g, w_ffn_gate, w_ffn_up, w_ffn_down,
              norm_ple_g, w_ple_gate, w_ple, final_norm_g):
    weights = (norm_mix_g, w_in, b_gates, lru_conv_w, lru_conv_b, lru_w_a, lru_b_a,
               lru_w_x, lru_b_x, lru_lambda, mlstm_conv_w, mlstm_conv_b, w_q, w_k, w_v,
               mlstm_norm_g, w_out, norm_ffn_g, w_ffn_gate, w_ffn_up, w_ffn_down,
               norm_ple_g, w_ple_gate, w_ple)
    B = x_prompt.shape[0]
    dt = x_prompt.dtype
    prompt_states = (
        jnp.zeros((DEPTH, B, CONV_W - 1, W_LRU), dt),
        jnp.zeros((DEPTH, B, W_LRU), dt),
        jnp.zeros((DEPTH, B, CONV_W - 1, D_M), dt),
        jnp.zeros((DEPTH, B, M_HEADS, M_HD, M_HD), jnp.float32),
        jnp.zeros((DEPTH, B, M_HEADS, M_HD), jnp.float32),
        jnp.full((DEPTH, B, M_HEADS), M_INIT, jnp.float32),
    )
    sample_states = (state_lru_conv, state_lru_h, state_mlstm_conv,
                     state_mlstm_C, state_mlstm_n, state_mlstm_m)
    y_prompt, ps = _trunk(x_prompt, p_prompt, prompt_states, weights, final_norm_g)
    y_sample, ss = _trunk(x_sample, p_sample, sample_states, weights, final_norm_g)
    p_lru_conv, p_lru_h, p_m_conv, p_C, p_n, p_m = ps
    s_lru_conv, s_lru_h, s_m_conv, s_C, s_n, s_m = ss
    return (y_prompt, y_sample, p_lru_conv, p_lru_h, p_m_conv, p_C, p_n, p_m,
            s_lru_conv, s_lru_h, s_m_conv, s_C, s_n, s_m)
```

```python
import functools

import jax
import jax.numpy as jnp
from jax import lax
from jax.experimental import pallas as pl
from jax.experimental.pallas import tpu as pltpu

D_MODEL = 1024
M_HEADS = 4
M_HD = D_MODEL // M_HEADS
LRU_BLOCKS = 8
LRU_BLK = D_MODEL // LRU_BLOCKS
LRU_C = 8.0
CONV_W = 4
D_FF = 2816
P_DIM = 256
EPS = 1e-6
M_INIT = -1e30
MASKED = -1e30
N_GATES = 2 * M_HEADS
GATE_PAD = 128
SEQ_TILE = 256
FFN_TILE = 512
FF_CHUNK = 256
CARRY_ROWS = 8
VMEM_LIMIT = 56 * 1024 * 1024

BF = jnp.bfloat16
F32 = jnp.float32


def _dot(a, b):
    return jnp.dot(a, b, preferred_element_type=F32)


def _dot_nt(a, b):
    return lax.dot_general(a, b, (((1,), (1,)), ((), ())), preferred_element_type=F32)


def _dot_tn(a, b):
    return lax.dot_general(a, b, (((0,), (0,)), ((), ())), preferred_element_type=F32)


def _sigmoid(x):
    return 1.0 / (1.0 + jnp.exp(-x))


def _softplus(x):
    return jnp.maximum(x, 0.0) + jnp.log1p(jnp.exp(-jnp.abs(x)))


def _rms(x, g):
    return x * lax.rsqrt(jnp.mean(x * x, axis=-1, keepdims=True) + EPS) * g


def _lru_coeffs(xc, ga, gx, lam):
    r = _sigmoid(ga)
    ig = _sigmoid(gx)
    log_a = -LRU_C * r * _softplus(-lam)
    a = jnp.exp(log_a)
    u = jnp.sqrt(1.0 - a * a) * ig * xc
    return a, u


def _lru_gates(xcb, wax_ref, ba, bx):
    ga, gx = [], []
    for n in range(LRU_BLOCKS):
        g = _dot(xcb[:, n * LRU_BLK:(n + 1) * LRU_BLK], wax_ref[n])
        ga.append(g[:, :LRU_BLK])
        gx.append(g[:, LRU_BLK:])
    return jnp.concatenate(ga, axis=1) + ba, jnp.concatenate(gx, axis=1) + bx


def _scan_linear(a, u):
    rows = a.shape[0]
    row = lax.broadcasted_iota(jnp.int32, a.shape, 0)
    s = 1
    while s < rows:
        keep = row >= s
        u_sh = jnp.where(keep, pltpu.roll(u, s, 0), 0.0)
        u = a * u_sh + u
        if 2 * s < rows:
            a = a * jnp.where(keep, pltpu.roll(a, s, 0), 1.0)
        s *= 2
    return u


def _cumsum_rows(x):
    rows = x.shape[0]
    row = lax.broadcasted_iota(jnp.int32, x.shape, 0)
    s = 1
    while s < rows:
        x = x + jnp.where(row >= s, pltpu.roll(x, s, 0), 0.0)
        s *= 2
    return x


def _conv_tile(ext_ref, x_new, w_ref, b_ref, rows):
    ext_ref[CARRY_ROWS:CARRY_ROWS + rows, :] = x_new
    first = CARRY_ROWS - (CONV_W - 1)
    out = b_ref[...]
    for j in range(CONV_W):
        out = out + ext_ref[first + j:first + j + rows, :] * w_ref[j:j + 1, :]
    return out


def _mixer_kernel(x_ref, gmix_ref, wmain_ref, wif_ref, bif_ref, lcw_ref, lcb_ref, wax_ref,
                  ba_ref, bx_ref, lam_ref, mcw_ref, mcb_ref, wq_ref, wk_ref, wv_ref, mg_ref,
                  wout_ref,
                  x1_ref, lconv_ref, h_ref, mconv_ref, c_ref, n_ref, m_ref,
                  xl_ext, xm_ext, mrg_ref):
    tt = SEQ_TILE
    t = pl.program_id(1)

    @pl.when(t == 0)
    def _():
        xl_ext[0:CARRY_ROWS, :] = jnp.zeros((CARRY_ROWS, D_MODEL), F32)
        xm_ext[0:CARRY_ROWS, :] = jnp.zeros((CARRY_ROWS, D_MODEL), F32)
        h_ref[...] = jnp.zeros(h_ref.shape, F32)
        c_ref[...] = jnp.zeros(c_ref.shape, F32)
        n_ref[...] = jnp.zeros(n_ref.shape, F32)
        m_ref[...] = jnp.full(m_ref.shape, M_INIT, F32)

    x = x_ref[0]
    xnb = _rms(x, gmix_ref[...]).astype(BF)

    x_l = _dot(xnb, wmain_ref[:, 0:D_MODEL])
    xl_c = _conv_tile(xl_ext, x_l, lcw_ref, lcb_ref, tt)
    lconv_ref[0] = xl_ext[CARRY_ROWS + tt - (CONV_W - 1):CARRY_ROWS + tt, :]
    xl_ext[0:CARRY_ROWS, :] = xl_ext[tt:tt + CARRY_ROWS, :]
    ga, gx = _lru_gates(xl_c.astype(BF), wax_ref, ba_ref[...], bx_ref[...])
    a, u = _lru_coeffs(xl_c, ga, gx, lam_ref[...])
    row0 = lax.broadcasted_iota(jnp.int32, (tt, D_MODEL), 0) == 0
    u = u + jnp.where(row0, a * h_ref[0], 0.0)
    y_l = _scan_linear(a, u)
    h_ref[0] = y_l[tt - 1:tt, :]

    x_m = _dot(xnb, wmain_ref[:, D_MODEL:2 * D_MODEL])
    xm_c = _conv_tile(xm_ext, x_m, mcw_ref, mcb_ref, tt)
    mconv_ref[0] = xm_ext[CARRY_ROWS + tt - (CONV_W - 1):CARRY_ROWS + tt, :]
    xm_ext[0:CARRY_ROWS, :] = xm_ext[tt:tt + CARRY_ROWS, :]
    xcb = (xm_c * _sigmoid(xm_c)).astype(BF)
    xmb = x_m.astype(BF)

    pre = _dot(xnb, wif_ref[...]) + bif_ref[...]
    lane = lax.broadcasted_iota(jnp.int32, (tt, GATE_PAD), 1)
    gcol = jnp.where(lane < M_HEADS, pre, _cumsum_rows(-_softplus(-pre)))
    grow = gcol.T
    tri = (lax.broadcasted_iota(jnp.int32, (tt, tt), 0)
           >= lax.broadcasted_iota(jnp.int32, (tt, tt), 1))

    for h in range(M_HEADS):
        hs = slice(h * M_HD, (h + 1) * M_HD)
        q = _dot(xcb[:, hs], wq_ref[h]) * (M_HD ** -0.5)
        k = _dot(xcb[:, hs], wk_ref[h])
        v = _dot(xmb[:, hs], wv_ref[h])
        qb, kb, vb = q.astype(BF), k.astype(BF), v.astype(BF)
        ig_col, b_col = gcol[:, h:h + 1], gcol[:, M_HEADS + h:M_HEADS + h + 1]
        ig_row, b_row = grow[h:h + 1, :], grow[M_HEADS + h:M_HEADS + h + 1, :]
        m_prev = m_ref[0, :, h:h + 1]
        c_prev = c_ref[0, h]
        n_prev = n_ref[0, h:h + 1, :]

        dlog = jnp.where(tri, b_col - b_row + ig_row, MASKED)
        m_inter = b_col + m_prev
        m_t = jnp.maximum(m_inter, jnp.max(dlog, axis=1, keepdims=True))
        s = _dot_nt(qb, kb) * jnp.exp(dlog - m_t)
        sc = jnp.exp(m_inter - m_t)
        num = _dot(s.astype(BF), vb) + sc * _dot(qb, c_prev.astype(BF))
        den = jnp.sum(s, axis=1, keepdims=True) + sc * jnp.sum(q * n_prev, axis=1, keepdims=True)
        hh = num / jnp.maximum(jnp.abs(den), jnp.exp(-m_t))

        m_last = m_t[tt - 1:tt, :]
        b_last = b_col[tt - 1:tt, :]
        kw = k * jnp.exp(b_last - b_col + ig_col - m_last)
        dec = jnp.exp(b_last + m_prev - m_last)
        c_ref[0, h] = dec * c_prev + _dot_tn(kw.astype(BF), vb)
        n_ref[0, h:h + 1, :] = dec * n_prev + jnp.sum(kw, axis=0, keepdims=True)
        m_ref[0, :, h:h + 1] = m_last

        hn = hh * lax.rsqrt(jnp.mean(hh * hh, axis=-1, keepdims=True) + EPS) * mg_ref[:, hs]
        o_m = _dot(xnb, wmain_ref[:, 2 * D_MODEL + h * M_HD:2 * D_MODEL + (h + 1) * M_HD])
        g_l = _dot(xnb, wmain_ref[:, 3 * D_MODEL + h * M_HD:3 * D_MODEL + (h + 1) * M_HD])
        g_m = _dot(xnb, wmain_ref[:, 4 * D_MODEL + h * M_HD:4 * D_MODEL + (h + 1) * M_HD])
        merged = _sigmoid(g_l) * y_l[:, hs] + _sigmoid(g_m) * (_sigmoid(o_m) * hn)
        mrg_ref[:, hs] = merged.astype(BF)

    x1_ref[0] = x + _dot(mrg_ref[...], wout_ref[...])


def _ffn_tail(x1, p, gffn_ref, wg_ref, wu_ref, wd_ref, gple_ref, wpg_ref, wple_ref, gfin_ref):
    xnb = _rms(x1, gffn_ref[...]).astype(BF)
    x2 = x1
    for c in range(D_FF // FF_CHUNK):
        cs = slice(c * FF_CHUNK, (c + 1) * FF_CHUNK)
        hg = _dot(xnb, wg_ref[:, cs])
        hu = _dot(xnb, wu_ref[:, cs])
        act = (hg * _sigmoid(hg) * hu).astype(BF)
        x2 = x2 + _dot(act, wd_ref[cs, :])
    gate = _sigmoid(_dot(_rms(x2, gple_ref[...]).astype(BF), wpg_ref[...]))
    x3 = x2 + gate * _dot(p.astype(BF), wple_ref[...])
    return _rms(x3, gfin_ref[...])


def _ffn_kernel(x1_ref, p_ref, gffn_ref, wg_ref, wu_ref, wd_ref, gple_ref, wpg_ref, wple_ref,
                gfin_ref, y_ref):
    y_ref[...] = _ffn_tail(x1_ref[...], p_ref[...], gffn_ref, wg_ref, wu_ref, wd_ref, gple_ref,
                           wpg_ref, wple_ref, gfin_ref)


def _front_kernel(x_ref, lbuf_ref, h0_ref, mbuf_ref, gmix_ref, wmain_ref, wif_ref, bif_ref,
                  lcw_ref, lcb_ref, wax_ref, ba_ref, bx_ref, lam_ref, mcw_ref, mcb_ref,
                  wq_ref, wk_ref, wv_ref,
                  q_ref, k_ref, v_ref, g_ref, yl_ref, gm_ref, lbuf_out, h_out, mbuf_out):
    xnb = _rms(x_ref[...], gmix_ref[...]).astype(BF)

    def conv_step(buf_ref, buf_out, x_new, w_ref, b_ref):
        out = b_ref[...] + x_new * w_ref[CONV_W - 1:CONV_W, :]
        for j in range(CONV_W - 1):
            out = out + buf_ref[j] * w_ref[j:j + 1, :]
        for j in range(CONV_W - 2):
            buf_out[j] = buf_ref[j + 1]
        buf_out[CONV_W - 2] = x_new
        return out

    x_l = _dot(xnb, wmain_ref[:, 0:D_MODEL])
    xl_c = conv_step(lbuf_ref, lbuf_out, x_l, lcw_ref, lcb_ref)
    ga, gx = _lru_gates(xl_c.astype(BF), wax_ref, ba_ref[...], bx_ref[...])
    a, u = _lru_coeffs(xl_c, ga, gx, lam_ref[...])
    y_l = a * h0_ref[...] + u
    h_out[...] = y_l
    g_l = _dot(xnb, wmain_ref[:, 3 * D_MODEL:4 * D_MODEL])
    yl_ref[...] = _sigmoid(g_l) * y_l

    x_m = _dot(xnb, wmain_ref[:, D_MODEL:2 * D_MODEL])
    xm_c = conv_step(mbuf_ref, mbuf_out, x_m, mcw_ref, mcb_ref)
    xcb = (xm_c * _sigmoid(xm_c)).astype(BF)
    xmb = x_m.astype(BF)
    for h in range(M_HEADS):
        hs = slice(h * M_HD, (h + 1) * M_HD)
        q_ref[:, hs] = _dot(xcb[:, hs], wq_ref[h]) * (M_HD ** -0.5)
        k_ref[:, hs] = _dot(xcb[:, hs], wk_ref[h])
        v_ref[:, hs] = _dot(xmb[:, hs], wv_ref[h])
    o_m = _dot(xnb, wmain_ref[:, 2 * D_MODEL:3 * D_MODEL])
    g_m = _dot(xnb, wmain_ref[:, 4 * D_MODEL:5 * D_MODEL])
    gm_ref[...] = _sigmoid(g_m) * _sigmoid(o_m)

    pre = _dot(xnb, wif_ref[...]) + bif_ref[...]
    lane = lax.broadcasted_iota(jnp.int32, pre.shape, 1)
    g_ref[...] = jnp.where(lane < M_HEADS, pre, -_softplus(-pre))


def _state_kernel(q_ref, k_ref, v_ref, g_ref, m_ref, n_ref, c_ref,
                  h_out, n_out, m_out, c_out):
    b = pl.program_id(0)
    q, k, v = q_ref[b], k_ref[b], v_ref[b]
    g = g_ref[b]
    m_all = m_ref[b]
    n_all = n_ref[b]
    qk_t = jnp.concatenate([q, k], axis=0).T
    for h in range(M_HEADS):
        ig, lf = g[:, h:h + 1], g[:, M_HEADS + h:M_HEADS + h + 1]
        m_prev = m_all[:, h:h + 1]
        q_row, k_row, v_row, n_row = q[h:h + 1], k[h:h + 1], v[h:h + 1], n_all[h:h + 1]
        q_col, k_col = qk_t[:, h:h + 1], qk_t[:, M_HEADS + h:M_HEADS + h + 1]
        c_prev = c_ref[0, h]

        m_inter = lf + m_prev
        m_t = jnp.maximum(m_inter, ig)
        wk = jnp.exp(ig - m_t)
        s = jnp.sum(q_row * k_row, axis=1, keepdims=True) * wk
        sc = jnp.exp(m_inter - m_t)
        num = s * v_row + sc * jnp.sum(q_col * c_prev, axis=0, keepdims=True)
        den = s + sc * jnp.sum(q_row * n_row, axis=1, keepdims=True)
        h_out[b, h:h + 1] = num / jnp.maximum(jnp.abs(den), jnp.exp(-m_t))
        c_out[0, h] = sc * c_prev + (wk * k_col) * v_row
        n_out[b, h:h + 1] = sc * n_row + wk * k_row
        m_out[b, :, h:h + 1] = m_t


def _back_kernel(x_ref, hpre_ref, yl_ref, gm_ref, p_ref, mg_ref, wout_ref, gffn_ref, wg_ref,
                 wu_ref, wd_ref, gple_ref, wpg_ref, wple_ref, gfin_ref, y_ref, mrg_ref):
    for h in range(M_HEADS):
        hs = slice(h * M_HD, (h + 1) * M_HD)
        hh = hpre_ref[:, hs]
        hn = hh * lax.rsqrt(jnp.mean(hh * hh, axis=-1, keepdims=True) + EPS) * mg_ref[:, hs]
        mrg_ref[:, hs] = (yl_ref[:, hs] + gm_ref[:, hs] * hn).astype(BF)
    x1 = x_ref[...] + _dot(mrg_ref[...], wout_ref[...])
    y_ref[...] = _ffn_tail(x1, p_ref[...], gffn_ref, wg_ref, wu_ref, wd_ref, gple_ref, wpg_ref,
                           wple_ref, gfin_ref)


def _resident(shape):
    nd = len(shape)
    return pl.BlockSpec(shape, lambda *_: (0,) * nd, pipeline_mode=pl.Buffered(1))


def _params(n_axes):
    return pltpu.CompilerParams(dimension_semantics=("arbitrary",) * n_axes,
                                vmem_limit_bytes=VMEM_LIMIT)


def kernel(x_prompt, x_sample, state_lru_conv, state_lru_h, state_mlstm_conv, state_mlstm_C, state_mlstm_n, state_mlstm_m, p_prompt, p_sample, norm_mix_g, w_in, b_gates, lru_conv_w, lru_conv_b, lru_w_a, lru_b_a, lru_w_x, lru_b_x, lru_lambda, mlstm_conv_w, mlstm_conv_b, w_q, w_k, w_v, mlstm_norm_g, w_out, norm_ffn_g, w_ffn_gate, w_ffn_up, w_ffn_down, norm_ple_g, w_ple_gate, w_ple, final_norm_g):
    assert w_in.shape[0] == 1, "single-layer trunk"
    B, T, _ = x_prompt.shape
    S = x_sample.shape[0]
    assert T % SEQ_TILE == 0 and (B * T) % FFN_TILE == 0 and x_sample.shape[1] == 1

    g0 = 2 * D_MODEL + D_MODEL
    w0 = w_in[0]
    wmain = jnp.concatenate([w0[:, :g0], w0[:, g0 + N_GATES:]], axis=1).astype(BF)
    wif = jnp.pad(w0[:, g0:g0 + N_GATES], ((0, 0), (0, GATE_PAD - N_GATES))).astype(BF)
    bif = jnp.pad(b_gates[0], (0, GATE_PAD - N_GATES)).reshape(1, GATE_PAD)
    wax = jnp.concatenate([lru_w_a[0], lru_w_x[0]], axis=2).astype(BF)
    row = lambda a: a.reshape(1, -1)
    gmix, ba, bx, lam = row(norm_mix_g[0]), row(lru_b_a[0]), row(lru_b_x[0]), row(lru_lambda[0])
    lcw, lcb = lru_conv_w[0], row(lru_conv_b[0])
    mcw, mcb = mlstm_conv_w[0], row(mlstm_conv_b[0])
    wq, wk, wv = w_q[0].astype(BF), w_k[0].astype(BF), w_v[0].astype(BF)
    mg = row(mlstm_norm_g[0])
    wout = w_out[0].astype(BF)
    gffn, gple, gfin = row(norm_ffn_g[0]), row(norm_ple_g[0]), row(final_norm_g)
    wg, wu, wd = w_ffn_gate[0].astype(BF), w_ffn_up[0].astype(BF), w_ffn_down[0].astype(BF)
    wpg, wple = w_ple_gate[0].astype(BF), w_ple[0].astype(BF)

    mixer_w = (gmix, wmain, wif, bif, lcw, lcb, wax, ba, bx, lam, mcw, mcb, wq, wk, wv, mg, wout)
    ffn_w = (gffn, wg, wu, wd, gple, wpg, wple, gfin)

    nt = T // SEQ_TILE
    sds = jax.ShapeDtypeStruct
    x1, p_lconv, p_h, p_mconv, p_c, p_n, p_m = pl.pallas_call(
        _mixer_kernel,
        grid=(B, nt),
        in_specs=[pl.BlockSpec((1, SEQ_TILE, D_MODEL), lambda b, t: (b, t, 0))]
                 + [_resident(w.shape) for w in mixer_w],
        out_specs=[
            pl.BlockSpec((1, SEQ_TILE, D_MODEL), lambda b, t: (b, t, 0)),
            pl.BlockSpec((1, CONV_W - 1, D_MODEL), lambda b, t: (b, 0, 0)),
            pl.BlockSpec((1, 1, D_MODEL), lambda b, t: (b, 0, 0)),
            pl.BlockSpec((1, CONV_W - 1, D_MODEL), lambda b, t: (b, 0, 0)),
            pl.BlockSpec((1, M_HEADS, M_HD, M_HD), lambda b, t: (b, 0, 0, 0)),
            pl.BlockSpec((1, M_HEADS, M_HD), lambda b, t: (b, 0, 0)),
            pl.BlockSpec((1, 1, M_HEADS), lambda b, t: (b, 0, 0)),
        ],
        out_shape=[
            sds((B, T, D_MODEL), F32),
            sds((B, CONV_W - 1, D_MODEL), F32),
            sds((B, 1, D_MODEL), F32),
            sds((B, CONV_W - 1, D_MODEL), F32),
            sds((B, M_HEADS, M_HD, M_HD), F32),
            sds((B, M_HEADS, M_HD), F32),
            sds((B, 1, M_HEADS), F32),
        ],
        scratch_shapes=[
            pltpu.VMEM((CARRY_ROWS + SEQ_TILE, D_MODEL), F32),
            pltpu.VMEM((CARRY_ROWS + SEQ_TILE, D_MODEL), F32),
            pltpu.VMEM((SEQ_TILE, D_MODEL), BF),
        ],
        compiler_params=_params(2),
        name="prompt_mixer",
    )(x_prompt, *mixer_w)

    n_tok = B * T
    y_prompt = pl.pallas_call(
        _ffn_kernel,
        grid=(n_tok // FFN_TILE,),
        in_specs=[pl.BlockSpec((FFN_TILE, D_MODEL), lambda i: (i, 0)),
                  pl.BlockSpec((FFN_TILE, P_DIM), lambda i: (i, 0))]
                 + [_resident(w.shape) for w in ffn_w],
        out_specs=pl.BlockSpec((FFN_TILE, D_MODEL), lambda i: (i, 0)),
        out_shape=sds((n_tok, D_MODEL), F32),
        compiler_params=_params(1),
        name="prompt_ffn",
    )(x1.reshape(n_tok, D_MODEL), p_prompt[0].reshape(n_tok, P_DIM), *ffn_w)

    xs = x_sample.reshape(S, D_MODEL)
    lbuf = jnp.swapaxes(state_lru_conv[0], 0, 1)
    mbuf = jnp.swapaxes(state_mlstm_conv[0], 0, 1)
    front_w = (gmix, wmain, wif, bif, lcw, lcb, wax, ba, bx, lam, mcw, mcb, wq, wk, wv)
    tok = sds((S, D_MODEL), F32)
    buf = sds((CONV_W - 1, S, D_MODEL), F32)
    q, k, v, g, yl, gm, s_lbuf, s_h, s_mbuf = pl.pallas_call(
        _front_kernel,
        out_shape=[tok, tok, tok, sds((S, GATE_PAD), F32), tok, tok, buf, tok, buf],
        compiler_params=pltpu.CompilerParams(vmem_limit_bytes=VMEM_LIMIT),
        name="sample_front",
    )(xs, lbuf, state_lru_h[0], mbuf, *front_w)

    heads = lambda a: a.reshape(S, M_HEADS, M_HD)
    whole = lambda shape: pl.BlockSpec(shape, lambda b: (0,) * len(shape))
    c_spec = pl.BlockSpec((1, M_HEADS, M_HD, M_HD), lambda b: (b, 0, 0, 0))
    hv, gv, mv = (S, M_HEADS, M_HD), (S, 1, GATE_PAD), (S, 1, M_HEADS)
    hpre, s_n, s_m, s_c = pl.pallas_call(
        _state_kernel,
        grid=(S,),
        in_specs=[whole(hv), whole(hv), whole(hv), whole(gv), whole(mv), whole(hv), c_spec],
        out_specs=[whole(hv), whole(hv), whole(mv), c_spec],
        out_shape=[sds(hv, F32), sds(hv, F32), sds(mv, F32),
                   sds((S, M_HEADS, M_HD, M_HD), F32)],
        compiler_params=_params(1),
        name="sample_state",
    )(heads(q), heads(k), heads(v), g.reshape(gv), state_mlstm_m[0].reshape(mv),
      state_mlstm_n[0], state_mlstm_C[0])

    y_sample = pl.pallas_call(
        _back_kernel,
        out_shape=tok,
        scratch_shapes=[pltpu.VMEM((S, D_MODEL), BF)],
        compiler_params=pltpu.CompilerParams(vmem_limit_bytes=VMEM_LIMIT),
        name="sample_back",
    )(xs, hpre.reshape(S, D_MODEL), yl, gm, p_sample[0].reshape(S, P_DIM), mg, wout, *ffn_w)

    lead = lambda a: a[None]
    return (y_prompt.reshape(B, T, D_MODEL), y_sample.reshape(S, 1, D_MODEL),
            lead(p_lconv), p_h.reshape(1, B, D_MODEL), lead(p_mconv), lead(p_c), lead(p_n),
            p_m.reshape(1, B, M_HEADS),
            lead(jnp.swapaxes(s_lbuf, 0, 1)), lead(s_h), lead(jnp.swapaxes(s_mbuf, 0, 1)),
            lead(s_c), lead(s_n), s_m.reshape(1, S, M_HEADS))
```

```python
import numpy as np

import jax
import jax.numpy as jnp
from jax import lax
from jax.experimental import pallas as pl
from jax.experimental.pallas import tpu as pltpu

D_MODEL = 1024
M_HEADS = 4
M_HD = D_MODEL // M_HEADS
LRU_BLOCKS = 8
LRU_BLK = D_MODEL // LRU_BLOCKS
LRU_C = 8.0
CONV_W = 4
D_FF = 2816
P_DIM = 256
EPS = 1e-6
M_INIT = -1e30
MASKED = -1e30
N_GATES = 2 * M_HEADS
GATE_PAD = 128
SUBLANES = 8
SEQ_TILE = 256
SEG = SEQ_TILE // SUBLANES
HEAD_ROWS = (CONV_W - 1) * SUBLANES
FFN_TILE = 512
FF_CHUNK = 256
STATE_BATCH = 4
VMEM_LIMIT = 56 * 1024 * 1024

BF = jnp.bfloat16
F32 = jnp.float32


def _dot(a, b):
    return jnp.dot(a, b, preferred_element_type=F32)


def _dot_nt(a, b):
    return lax.dot_general(a, b, (((1,), (1,)), ((), ())), preferred_element_type=F32)


def _dot_tn(a, b):
    return lax.dot_general(a, b, (((0,), (0,)), ((), ())), preferred_element_type=F32)


def _sigmoid(x):
    return 1.0 / (1.0 + jnp.exp(-x))


def _softplus(x):
    return jnp.maximum(x, 0.0) + jnp.log1p(jnp.exp(-jnp.abs(x)))


def _rms(x, g):
    return x * lax.rsqrt(jnp.mean(x * x, axis=-1, keepdims=True) + EPS) * g


def _group(x, i):
    return x[i * SUBLANES:(i + 1) * SUBLANES, :]


def _lru_coeffs(xc, ga, gx, lam):
    r = _sigmoid(ga)
    ig = _sigmoid(gx)
    log_a = -LRU_C * r * _softplus(-lam)
    a = jnp.exp(log_a)
    om = 1.0 - a * a
    root = jnp.where(om > 0.0, om * lax.rsqrt(om), 0.0)
    return a, root * ig * xc


def _lru_gates(xcb, wax_ref, ba, bx):
    ga, gx = [], []
    for n in range(LRU_BLOCKS):
        g = _dot(xcb[:, n * LRU_BLK:(n + 1) * LRU_BLK], wax_ref[n])
        ga.append(g[:, :LRU_BLK])
        gx.append(g[:, LRU_BLK:])
    return jnp.concatenate(ga, axis=1) + ba, jnp.concatenate(gx, axis=1) + bx


def _scan_interleaved(a, u, h0, y_ref):
    prods, sums = [], []
    p = s = None
    for i in range(SEG):
        ai, ui = _group(a, i), _group(u, i)
        p, s = (ai, ui) if i == 0 else (ai * p, ai * s + ui)
        prods.append(p)
        sums.append(s)
    c = h0
    starts = [c]
    for j in range(SUBLANES - 1):
        c = p[j:j + 1, :] * c + s[j:j + 1, :]
        starts.append(c)
    start = jnp.concatenate(starts, axis=0)
    for i in range(SEG):
        y_ref[i * SUBLANES:(i + 1) * SUBLANES, :] = sums[i] + prods[i] * start


def _cumsum_interleaved(x):
    acc = []
    run = None
    for i in range(SEG):
        run = _group(x, i) if i == 0 else run + _group(x, i)
        acc.append(run)
    sub = lax.broadcasted_iota(jnp.int32, run.shape, 0)
    inc = run
    s = 1
    while s < SUBLANES:
        inc = inc + jnp.where(sub >= s, pltpu.roll(inc, s, 0), 0.0)
        s *= 2
    before = inc - run
    return jnp.concatenate([r + before for r in acc], axis=0)


def _conv_interleaved(ext_ref, carry_ref, x_new, w_ref, b_ref):
    sub = lax.broadcasted_iota(jnp.int32, (SUBLANES, D_MODEL), 0)
    for r in range(CONV_W - 1):
        cur = _group(x_new, SEG - (CONV_W - 1) + r)
        prev = carry_ref[0, r:r + 1, :]
        ext_ref[r * SUBLANES:(r + 1) * SUBLANES, :] = jnp.where(sub == 0, prev,
                                                                pltpu.roll(cur, 1, 0))
        carry_ref[0, r:r + 1, :] = cur[SUBLANES - 1:SUBLANES, :]
    ext_ref[HEAD_ROWS:HEAD_ROWS + SEQ_TILE, :] = x_new
    out = b_ref[...]
    for j in range(CONV_W):
        out = out + ext_ref[j * SUBLANES:j * SUBLANES + SEQ_TILE, :] * w_ref[j:j + 1, :]
    return out


def _time_of_row(r):
    return (r & (SUBLANES - 1)) * SEG + (r >> (SUBLANES.bit_length() - 1))


def _mixer_kernel(x_ref, perm_ref, permt_ref, gmix_ref, wmain_ref, wif_ref, bif_ref, lcw_ref,
                  lcb_ref, wax_ref, ba_ref, bx_ref, lam_ref, mcw_ref, mcb_ref, wq_ref, wk_ref,
                  wv_ref, mg_ref, wout_ref,
                  x1_ref, lconv_ref, h_ref, mconv_ref, c_ref, n_ref, m_ref,
                  xl_ext, xm_ext, yl_ref, mrg_ref):
    tt = SEQ_TILE
    last = tt - 1

    @pl.when(pl.program_id(1) == 0)
    def _():
        lconv_ref[...] = jnp.zeros(lconv_ref.shape, F32)
        mconv_ref[...] = jnp.zeros(mconv_ref.shape, F32)
        h_ref[...] = jnp.zeros(h_ref.shape, F32)
        c_ref[...] = jnp.zeros(c_ref.shape, F32)
        n_ref[...] = jnp.zeros(n_ref.shape, F32)
        m_ref[...] = jnp.full(m_ref.shape, M_INIT, F32)

    x = x_ref[0]
    xn_t = _rms(x, gmix_ref[...]).astype(BF)
    xnb = _dot(perm_ref[...], xn_t).astype(BF)

    x_l = _dot(xnb, wmain_ref[:, 0:D_MODEL])
    xl_c = _conv_interleaved(xl_ext, lconv_ref, x_l, lcw_ref, lcb_ref)
    ga, gx = _lru_gates(xl_c.astype(BF), wax_ref, ba_ref[...], bx_ref[...])
    a, u = _lru_coeffs(xl_c, ga, gx, lam_ref[...])
    _scan_interleaved(a, u, h_ref[0], yl_ref)
    h_ref[0] = yl_ref[last:last + 1, :]

    x_m = _dot(xnb, wmain_ref[:, D_MODEL:2 * D_MODEL])
    xm_c = _conv_interleaved(xm_ext, mconv_ref, x_m, mcw_ref, mcb_ref)
    xcb = (xm_c * _sigmoid(xm_c)).astype(BF)
    xmb = x_m.astype(BF)

    pre = _dot(xnb, wif_ref[...]) + bif_ref[...]
    lane = lax.broadcasted_iota(jnp.int32, (tt, GATE_PAD), 1)
    gcol = jnp.where(lane < M_HEADS, pre, _cumsum_interleaved(-_softplus(-pre)))
    grow = gcol.T
    tri = (_time_of_row(lax.broadcasted_iota(jnp.int32, (tt, 1), 0))
           >= _time_of_row(lax.broadcasted_iota(jnp.int32, (1, tt), 1)))

    for h in range(M_HEADS):
        hs = slice(h * M_HD, (h + 1) * M_HD)
        q = _dot(xcb[:, hs], wq_ref[h]) * (M_HD ** -0.5)
        k = _dot(xcb[:, hs], wk_ref[h])
        v = _dot(xmb[:, hs], wv_ref[h])
        qb, kb, vb = q.astype(BF), k.astype(BF), v.astype(BF)
        ig_col, b_col = gcol[:, h:h + 1], gcol[:, M_HEADS + h:M_HEADS + h + 1]
        ig_row, b_row = grow[h:h + 1, :], grow[M_HEADS + h:M_HEADS + h + 1, :]
        m_prev = m_ref[0, :, h:h + 1]
        c_prev = c_ref[0, h]
        n_prev = n_ref[0, h:h + 1, :]

        dlog = jnp.where(tri, b_col - b_row + ig_row, MASKED)
        m_inter = b_col + m_prev
        m_t = jnp.maximum(m_inter, jnp.max(dlog, axis=1, keepdims=True))
        s = _dot_nt(qb, kb) * jnp.exp(dlog - m_t)
        sc = jnp.exp(m_inter - m_t)
        num = _dot(s.astype(BF), vb) + sc * _dot(qb, c_prev.astype(BF))
        den = jnp.sum(s, axis=1, keepdims=True) + sc * jnp.sum(q * n_prev, axis=1, keepdims=True)
        hh = num / jnp.maximum(jnp.abs(den), jnp.exp(-m_t))

        m_last = m_t[last:last + 1, :]
        b_last = b_col[last:last + 1, :]
        kw = k * jnp.exp(b_last - b_col + ig_col - m_last)
        dec = jnp.exp(b_last + m_prev - m_last)
        c_ref[0, h] = dec * c_prev + _dot_tn(kw.astype(BF), vb)
        n_ref[0, h:h + 1, :] = dec * n_prev + jnp.sum(kw, axis=0, keepdims=True)
        m_ref[0, :, h:h + 1] = m_last

        hn = hh * lax.rsqrt(jnp.mean(hh * hh, axis=-1, keepdims=True) + EPS) * mg_ref[:, hs]
        o_m = _dot(xnb, wmain_ref[:, 2 * D_MODEL + h * M_HD:2 * D_MODEL + (h + 1) * M_HD])
        g_l = _dot(xnb, wmain_ref[:, 3 * D_MODEL + h * M_HD:3 * D_MODEL + (h + 1) * M_HD])
        g_m = _dot(xnb, wmain_ref[:, 4 * D_MODEL + h * M_HD:4 * D_MODEL + (h + 1) * M_HD])
        merged = _sigmoid(g_l) * yl_ref[:, hs] + _sigmoid(g_m) * (_sigmoid(o_m) * hn)
        mrg_ref[:, hs] = merged.astype(BF)

    merged_t = _dot(permt_ref[...], mrg_ref[...]).astype(BF)
    x1_ref[0] = x + _dot(merged_t, wout_ref[...])


def _ffn_tail(x1, p, gffn_ref, wg_ref, wu_ref, wd_ref, gple_ref, wpg_ref, wple_ref, gfin_ref):
    xnb = _rms(x1, gffn_ref[...]).astype(BF)
    x2 = x1
    for c in range(D_FF // FF_CHUNK):
        cs = slice(c * FF_CHUNK, (c + 1) * FF_CHUNK)
        hg = _dot(xnb, wg_ref[:, cs])
        hu = _dot(xnb, wu_ref[:, cs])
        act = (hg * _sigmoid(hg) * hu).astype(BF)
        x2 = x2 + _dot(act, wd_ref[cs, :])
    gate = _sigmoid(_dot(_rms(x2, gple_ref[...]).astype(BF), wpg_ref[...]))
    x3 = x2 + gate * _dot(p.astype(BF), wple_ref[...])
    return _rms(x3, gfin_ref[...])


def _ffn_kernel(x1_ref, p_ref, gffn_ref, wg_ref, wu_ref, wd_ref, gple_ref, wpg_ref, wple_ref,
                gfin_ref, y_ref):
    y_ref[...] = _ffn_tail(x1_ref[...], p_ref[...], gffn_ref, wg_ref, wu_ref, wd_ref, gple_ref,
                           wpg_ref, wple_ref, gfin_ref)


def _front_kernel(x_ref, lbuf_ref, h0_ref, mbuf_ref, gmix_ref, wmain_ref, wif_ref, bif_ref,
                  lcw_ref, lcb_ref, wax_ref, ba_ref, bx_ref, lam_ref, mcw_ref, mcb_ref,
                  wq_ref, wk_ref, wv_ref,
                  q_ref, k_ref, v_ref, g_ref, yl_ref, gm_ref, lbuf_out, h_out, mbuf_out):
    xnb = _rms(x_ref[...], gmix_ref[...]).astype(BF)

    def conv_step(buf_ref, buf_out, x_new, w_ref, b_ref):
        out = b_ref[...] + x_new * w_ref[CONV_W - 1:CONV_W, :]
        for j in range(CONV_W - 1):
            out = out + buf_ref[j] * w_ref[j:j + 1, :]
        for j in range(CONV_W - 2):
            buf_out[j] = buf_ref[j + 1]
        buf_out[CONV_W - 2] = x_new
        return out

    x_l = _dot(xnb, wmain_ref[:, 0:D_MODEL])
    xl_c = conv_step(lbuf_ref, lbuf_out, x_l, lcw_ref, lcb_ref)
    ga, gx = _lru_gates(xl_c.astype(BF), wax_ref, ba_ref[...], bx_ref[...])
    a, u = _lru_coeffs(xl_c, ga, gx, lam_ref[...])
    y_l = a * h0_ref[...] + u
    h_out[...] = y_l
    g_l = _dot(xnb, wmain_ref[:, 3 * D_MODEL:4 * D_MODEL])
    yl_ref[...] = _sigmoid(g_l) * y_l

    x_m = _dot(xnb, wmain_ref[:, D_MODEL:2 * D_MODEL])
    xm_c = conv_step(mbuf_ref, mbuf_out, x_m, mcw_ref, mcb_ref)
    xcb = (xm_c * _sigmoid(xm_c)).astype(BF)
    xmb = x_m.astype(BF)
    for h in range(M_HEADS):
        hs = slice(h * M_HD, (h + 1) * M_HD)
        q_ref[:, hs] = _dot(xcb[:, hs], wq_ref[h]) * (M_HD ** -0.5)
        k_ref[:, hs] = _dot(xcb[:, hs], wk_ref[h])
        v_ref[:, hs] = _dot(xmb[:, hs], wv_ref[h])
    o_m = _dot(xnb, wmain_ref[:, 2 * D_MODEL:3 * D_MODEL])
    g_m = _dot(xnb, wmain_ref[:, 4 * D_MODEL:5 * D_MODEL])
    gm_ref[...] = _sigmoid(g_m) * _sigmoid(o_m)

    pre = _dot(xnb, wif_ref[...]) + bif_ref[...]
    lane = lax.broadcasted_iota(jnp.int32, pre.shape, 1)
    g_ref[...] = jnp.where(lane < M_HEADS, pre, -_softplus(-pre))


def _state_kernel(q_ref, k_ref, v_ref, g_ref, m_ref, n_ref, c_ref,
                  h_out, n_out, m_out, c_out):
    for bb in range(STATE_BATCH):
        b = pl.program_id(0) * STATE_BATCH + bb
        q, k, v = q_ref[b], k_ref[b], v_ref[b]
        g = g_ref[b]
        m_all = m_ref[b]
        n_all = n_ref[b]
        qk_t = jnp.concatenate([q, k], axis=0).T
        for h in range(M_HEADS):
            ig, lf = g[:, h:h + 1], g[:, M_HEADS + h:M_HEADS + h + 1]
            m_prev = m_all[:, h:h + 1]
            q_row, k_row, v_row, n_row = q[h:h + 1], k[h:h + 1], v[h:h + 1], n_all[h:h + 1]
            q_col, k_col = qk_t[:, h:h + 1], qk_t[:, M_HEADS + h:M_HEADS + h + 1]
            c_prev = c_ref[bb, h]

            m_inter = lf + m_prev
            m_t = jnp.maximum(m_inter, ig)
            wk = jnp.exp(ig - m_t)
            s = jnp.sum(q_row * k_row, axis=1, keepdims=True) * wk
            sc = jnp.exp(m_inter - m_t)
            num = s * v_row + sc * jnp.sum(q_col * c_prev, axis=0, keepdims=True)
            den = s + sc * jnp.sum(q_row * n_row, axis=1, keepdims=True)
            h_out[b, h:h + 1] = num / jnp.maximum(jnp.abs(den), jnp.exp(-m_t))
            c_out[bb, h] = sc * c_prev + (wk * k_col) * v_row
            n_out[b, h:h + 1] = sc * n_row + wk * k_row
            m_out[b, :, h:h + 1] = m_t


def _back_kernel(x_ref, hpre_ref, yl_ref, gm_ref, p_ref, mg_ref, wout_ref, gffn_ref, wg_ref,
                 wu_ref, wd_ref, gple_ref, wpg_ref, wple_ref, gfin_ref, y_ref, mrg_ref):
    for h in range(M_HEADS):
        hs = slice(h * M_HD, (h + 1) * M_HD)
        hh = hpre_ref[:, hs]
        hn = hh * lax.rsqrt(jnp.mean(hh * hh, axis=-1, keepdims=True) + EPS) * mg_ref[:, hs]
        mrg_ref[:, hs] = (yl_ref[:, hs] + gm_ref[:, hs] * hn).astype(BF)
    x1 = x_ref[...] + _dot(mrg_ref[...], wout_ref[...])
    y_ref[...] = _ffn_tail(x1, p_ref[...], gffn_ref, wg_ref, wu_ref, wd_ref, gple_ref, wpg_ref,
                           wple_ref, gfin_ref)


def _resident(shape):
    nd = len(shape)
    return pl.BlockSpec(shape, lambda *_: (0,) * nd, pipeline_mode=pl.Buffered(1))


def _params(n_axes):
    return pltpu.CompilerParams(dimension_semantics=("arbitrary",) * n_axes,
                                vmem_limit_bytes=VMEM_LIMIT)


def _interleave_matrix():
    r = np.arange(SEQ_TILE)
    perm = np.zeros((SEQ_TILE, SEQ_TILE), np.float32)
    perm[r, (r % SUBLANES) * SEG + r // SUBLANES] = 1.0
    return perm


def kernel(x_prompt, x_sample, state_lru_conv, state_lru_h, state_mlstm_conv, state_mlstm_C, state_mlstm_n, state_mlstm_m, p_prompt, p_sample, norm_mix_g, w_in, b_gates, lru_conv_w, lru_conv_b, lru_w_a, lru_b_a, lru_w_x, lru_b_x, lru_lambda, mlstm_conv_w, mlstm_conv_b, w_q, w_k, w_v, mlstm_norm_g, w_out, norm_ffn_g, w_ffn_gate, w_ffn_up, w_ffn_down, norm_ple_g, w_ple_gate, w_ple, final_norm_g):
    assert w_in.shape[0] == 1, "single-layer trunk"
    B, T, _ = x_prompt.shape
    S = x_sample.shape[0]
    assert T % SEQ_TILE == 0 and (B * T) % FFN_TILE == 0 and x_sample.shape[1] == 1
    assert S % STATE_BATCH == 0

    g0 = 2 * D_MODEL + D_MODEL
    w0 = w_in[0]
    wmain = jnp.concatenate([w0[:, :g0], w0[:, g0 + N_GATES:]], axis=1).astype(BF)
    wif = jnp.pad(w0[:, g0:g0 + N_GATES], ((0, 0), (0, GATE_PAD - N_GATES))).astype(BF)
    bif = jnp.pad(b_gates[0], (0, GATE_PAD - N_GATES)).reshape(1, GATE_PAD)
    wax = jnp.concatenate([lru_w_a[0], lru_w_x[0]], axis=2).astype(BF)
    row = lambda a: a.reshape(1, -1)
    gmix, ba, bx, lam = row(norm_mix_g[0]), row(lru_b_a[0]), row(lru_b_x[0]), row(lru_lambda[0])
    lcw, lcb = lru_conv_w[0], row(lru_conv_b[0])
    mcw, mcb = mlstm_conv_w[0], row(mlstm_conv_b[0])
    wq, wk, wv = w_q[0].astype(BF), w_k[0].astype(BF), w_v[0].astype(BF)
    mg = row(mlstm_norm_g[0])
    wout = w_out[0].astype(BF)
    gffn, gple, gfin = row(norm_ffn_g[0]), row(norm_ple_g[0]), row(final_norm_g)
    wg, wu, wd = w_ffn_gate[0].astype(BF), w_ffn_up[0].astype(BF), w_ffn_down[0].astype(BF)
    wpg, wple = w_ple_gate[0].astype(BF), w_ple[0].astype(BF)
    perm_np = _interleave_matrix()
    perm, perm_t = jnp.asarray(perm_np, BF), jnp.asarray(perm_np.T, BF)

    front_w = (gmix, wmain, wif, bif, lcw, lcb, wax, ba, bx, lam, mcw, mcb, wq, wk, wv)
    mixer_w = (perm, perm_t) + front_w + (mg, wout)
    ffn_w = (gffn, wg, wu, wd, gple, wpg, wple, gfin)

    nt = T // SEQ_TILE
    sds = jax.ShapeDtypeStruct
    x1, p_lconv, p_h, p_mconv, p_c, p_n, p_m = pl.pallas_call(
        _mixer_kernel,
        grid=(B, nt),
        in_specs=[pl.BlockSpec((1, SEQ_TILE, D_MODEL), lambda b, t: (b, t, 0))]
                 + [_resident(w.shape) for w in mixer_w],
        out_specs=[
            pl.BlockSpec((1, SEQ_TILE, D_MODEL), lambda b, t: (b, t, 0)),
            pl.BlockSpec((1, CONV_W - 1, D_MODEL), lambda b, t: (b, 0, 0)),
            pl.BlockSpec((1, 1, D_MODEL), lambda b, t: (b, 0, 0)),
            pl.BlockSpec((1, CONV_W - 1, D_MODEL), lambda b, t: (b, 0, 0)),
            pl.BlockSpec((1, M_HEADS, M_HD, M_HD), lambda b, t: (b, 0, 0, 0)),
            pl.BlockSpec((1, M_HEADS, M_HD), lambda b, t: (b, 0, 0)),
            pl.BlockSpec((1, 1, M_HEADS), lambda b, t: (b, 0, 0)),
        ],
        out_shape=[
            sds((B, T, D_MODEL), F32),
            sds((B, CONV_W - 1, D_MODEL), F32),
            sds((B, 1, D_MODEL), F32),
            sds((B, CONV_W - 1, D_MODEL), F32),
            sds((B, M_HEADS, M_HD, M_HD), F32),
            sds((B, M_HEADS, M_HD), F32),
            sds((B, 1, M_HEADS), F32),
        ],
        scratch_shapes=[
            pltpu.VMEM((HEAD_ROWS + SEQ_TILE, D_MODEL), F32),
            pltpu.VMEM((HEAD_ROWS + SEQ_TILE, D_MODEL), F32),
            pltpu.VMEM((SEQ_TILE, D_MODEL), F32),
            pltpu.VMEM((SEQ_TILE, D_MODEL), BF),
        ],
        compiler_params=_params(2),
        name="prompt_mixer",
    )(x_prompt, *mixer_w)

    n_tok = B * T
    y_prompt = pl.pallas_call(
        _ffn_kernel,
        grid=(n_tok // FFN_TILE,),
        in_specs=[pl.BlockSpec((FFN_TILE, D_MODEL), lambda i: (i, 0)),
                  pl.BlockSpec((FFN_TILE, P_DIM), lambda i: (i, 0))]
                 + [_resident(w.shape) for w in ffn_w],
        out_specs=pl.BlockSpec((FFN_TILE, D_MODEL), lambda i: (i, 0)),
        out_shape=sds((n_tok, D_MODEL), F32),
        compiler_params=_params(1),
        name="prompt_ffn",
    )(x1.reshape(n_tok, D_MODEL), p_prompt[0].reshape(n_tok, P_DIM), *ffn_w)

    xs = x_sample.reshape(S, D_MODEL)
    lbuf = jnp.swapaxes(state_lru_conv[0], 0, 1)
    mbuf = jnp.swapaxes(state_mlstm_conv[0], 0, 1)
    tok = sds((S, D_MODEL), F32)
    buf = sds((CONV_W - 1, S, D_MODEL), F32)
    q, k, v, g, yl, gm, s_lbuf, s_h, s_mbuf = pl.pallas_call(
        _front_kernel,
        out_shape=[tok, tok, tok, sds((S, GATE_PAD), F32), tok, tok, buf, tok, buf],
        compiler_params=pltpu.CompilerParams(vmem_limit_bytes=VMEM_LIMIT),
        name="sample_front",
    )(xs, lbuf, state_lru_h[0], mbuf, *front_w)

    heads = lambda a: a.reshape(S, M_HEADS, M_HD)
    whole = lambda shape: pl.BlockSpec(shape, lambda b: (0,) * len(shape))
    c_spec = pl.BlockSpec((STATE_BATCH, M_HEADS, M_HD, M_HD), lambda b: (b, 0, 0, 0))
    hv, gv, mv = (S, M_HEADS, M_HD), (S, 1, GATE_PAD), (S, 1, M_HEADS)
    hpre, s_n, s_m, s_c = pl.pallas_call(
        _state_kernel,
        grid=(S // STATE_BATCH,),
        in_specs=[whole(hv), whole(hv), whole(hv), whole(gv), whole(mv), whole(hv), c_spec],
        out_specs=[whole(hv), whole(hv), whole(mv), c_spec],
        out_shape=[sds(hv, F32), sds(hv, F32), sds(mv, F32),
                   sds((S, M_HEADS, M_HD, M_HD), F32)],
        compiler_params=_params(1),
        name="sample_state",
    )(heads(q), heads(k), heads(v), g.reshape(gv), state_mlstm_m[0].reshape(mv),
      state_mlstm_n[0], state_mlstm_C[0])

    y_sample = pl.pallas_call(
        _back_kernel,
        out_shape=tok,
        scratch_shapes=[pltpu.VMEM((S, D_MODEL), BF)],
        compiler_params=pltpu.CompilerParams(vmem_limit_bytes=VMEM_LIMIT),
        name="sample_back",
    )(xs, hpre.reshape(S, D_MODEL), yl, gm, p_sample[0].reshape(S, P_DIM), mg, wout, *ffn_w)

    lead = lambda a: a[None]
    return (y_prompt.reshape(B, T, D_MODEL), y_sample.reshape(S, 1, D_MODEL),
            lead(p_lconv), p_h.reshape(1, B, D_MODEL), lead(p_mconv), lead(p_c), lead(p_n),
            p_m.reshape(1, B, M_HEADS),
            lead(jnp.swapaxes(s_lbuf, 0, 1)), lead(s_h), lead(jnp.swapaxes(s_mbuf, 0, 1)),
            lead(s_c), lead(s_n), s_m.reshape(1, S, M_HEADS))
```

```python
import numpy as np

import jax
import jax.numpy as jnp
from jax import lax
from jax.experimental import pallas as pl
from jax.experimental.pallas import tpu as pltpu

D_MODEL = 1024
M_HEADS = 4
M_HD = D_MODEL // M_HEADS
LRU_BLOCKS = 8
LRU_BLK = D_MODEL // LRU_BLOCKS
LRU_C = 8.0
CONV_W = 4
D_FF = 2816
P_DIM = 256
EPS = 1e-6
M_INIT = -1e30
MASKED = -1e30
N_GATES = 2 * M_HEADS
GATE_PAD = 128
SUBLANES = 8
SEQ_TILE = 256
SEG = SEQ_TILE // SUBLANES
FFN_TILE = 512
FF_CHUNK = 256
STATE_BATCH = 4
VMEM_LIMIT = 56 * 1024 * 1024

BF = jnp.bfloat16
F32 = jnp.float32


def _dot(a, b):
    return jnp.dot(a, b, preferred_element_type=F32)


def _dot_nt(a, b):
    return lax.dot_general(a, b, (((1,), (1,)), ((), ())), preferred_element_type=F32)


def _dot_tn(a, b):
    return lax.dot_general(a, b, (((0,), (0,)), ((), ())), preferred_element_type=F32)


def _sigmoid(x):
    return 1.0 / (1.0 + jnp.exp(-x))


def _softplus(x):
    return jnp.maximum(x, 0.0) + jnp.log1p(jnp.exp(-jnp.abs(x)))


def _rms(x, g):
    return x * lax.rsqrt(jnp.mean(x * x, axis=-1, keepdims=True) + EPS) * g


def _group(x, i):
    return x[i * SUBLANES:(i + 1) * SUBLANES, :]


def _lru_coeffs(xc, ga, gx, lam):
    r = _sigmoid(ga)
    ig = _sigmoid(gx)
    log_a = -LRU_C * r * _softplus(-lam)
    a = jnp.exp(log_a)
    om = 1.0 - a * a
    root = jnp.where(om > 0.0, om * lax.rsqrt(om), 0.0)
    return a, root * ig * xc


def _lru_gates(xcb, wax_ref, ba, bx):
    ga, gx = [], []
    for n in range(LRU_BLOCKS):
        g = _dot(xcb[:, n * LRU_BLK:(n + 1) * LRU_BLK], wax_ref[n])
        ga.append(g[:, :LRU_BLK])
        gx.append(g[:, LRU_BLK:])
    return jnp.concatenate(ga, axis=1) + ba, jnp.concatenate(gx, axis=1) + bx


def _scan_interleaved(a, u, h0):
    prods, sums = [], []
    p = s = None
    for i in range(SEG):
        ai, ui = _group(a, i), _group(u, i)
        p, s = (ai, ui) if i == 0 else (ai * p, ai * s + ui)
        prods.append(p)
        sums.append(s)
    c = h0
    starts = [c]
    for j in range(SUBLANES - 1):
        c = p[j:j + 1, :] * c + s[j:j + 1, :]
        starts.append(c)
    start = jnp.concatenate(starts, axis=0)
    return jnp.concatenate([sums[i] + prods[i] * start for i in range(SEG)], axis=0)


def _cumsum_interleaved(x):
    acc = []
    run = None
    for i in range(SEG):
        run = _group(x, i) if i == 0 else run + _group(x, i)
        acc.append(run)
    sub = lax.broadcasted_iota(jnp.int32, run.shape, 0)
    inc = run
    s = 1
    while s < SUBLANES:
        inc = inc + jnp.where(sub >= s, pltpu.roll(inc, s, 0), 0.0)
        s *= 2
    before = inc - run
    return jnp.concatenate([r + before for r in acc], axis=0)


def _conv_interleaved(carry, x_new, w, b):
    sub = lax.broadcasted_iota(jnp.int32, (SUBLANES, x_new.shape[1]), 0)
    head, new = [], []
    for r in range(CONV_W - 1):
        cur = _group(x_new, SEG - (CONV_W - 1) + r)
        head.append(jnp.where(sub == 0, carry[r:r + 1, :], pltpu.roll(cur, 1, 0)))
        new.append(cur[SUBLANES - 1:SUBLANES, :])
    ext = jnp.concatenate(head + [x_new], axis=0)
    out = b
    for j in range(CONV_W):
        out = out + ext[j * SUBLANES:j * SUBLANES + SEQ_TILE, :] * w[j:j + 1, :]
    return out, jnp.concatenate(new, axis=0)


def _time_of_row(r):
    return (r & (SUBLANES - 1)) * SEG + (r >> (SUBLANES.bit_length() - 1))


def _mixer_kernel(x_ref, perm_ref, permt_ref, gmix_ref, wmain_ref, wif_ref, bif_ref, lcw_ref,
                  lcb_ref, wax_ref, ba_ref, bx_ref, lam_ref, mcw_ref, mcb_ref, wq_ref, wk_ref,
                  wv_ref, mg_ref, wout_ref,
                  x1_ref, lconv_ref, h_ref, mconv_ref, c_ref, n_ref, m_ref):
    tt = SEQ_TILE
    last = tt - 1

    @pl.when(pl.program_id(1) == 0)
    def _():
        lconv_ref[...] = jnp.zeros(lconv_ref.shape, F32)
        mconv_ref[...] = jnp.zeros(mconv_ref.shape, F32)
        h_ref[...] = jnp.zeros(h_ref.shape, F32)
        c_ref[...] = jnp.zeros(c_ref.shape, F32)
        n_ref[...] = jnp.zeros(n_ref.shape, F32)
        m_ref[...] = jnp.full(m_ref.shape, M_INIT, F32)

    x = x_ref[0]
    lconv, mconv, h0 = lconv_ref[0], mconv_ref[0], h_ref[0]
    c_all = [c_ref[0, h] for h in range(M_HEADS)]
    n_all, m_all = n_ref[0], m_ref[0]
    lcw, lcb, mcw, mcb = lcw_ref[...], lcb_ref[...], mcw_ref[...], mcb_ref[...]
    ba, bx, lam, mg = ba_ref[...], bx_ref[...], lam_ref[...], mg_ref[...]

    xn_t = _rms(x, gmix_ref[...]).astype(BF)
    xnb = _dot(perm_ref[...], xn_t).astype(BF)

    def proj(col0, g):
        return _dot(xnb, wmain_ref[:, col0 + g * M_HD:col0 + (g + 1) * M_HD])

    x_l = {0: proj(0, 0)}
    x_m, o_m, g_l, g_m = {}, {}, {}, {}

    y_l, lconv_new, h_new = [], [], []
    for g in range(M_HEADS):
        gs = slice(g * M_HD, (g + 1) * M_HD)
        if g + 1 < M_HEADS:
            x_l[g + 1] = proj(0, g + 1)
        x_m[g] = proj(D_MODEL, g)
        o_m[g] = proj(2 * D_MODEL, g)
        xl_c, cnew = _conv_interleaved(lconv[:, gs], x_l[g], lcw[:, gs], lcb[:, gs])
        lconv_new.append(cnew)
        xlb = xl_c.astype(BF)
        ga, gx = [], []
        for n in range(2):
            gg = _dot(xlb[:, n * LRU_BLK:(n + 1) * LRU_BLK], wax_ref[2 * g + n])
            ga.append(gg[:, :LRU_BLK])
            gx.append(gg[:, LRU_BLK:])
        a, u = _lru_coeffs(xl_c, jnp.concatenate(ga, axis=1) + ba[:, gs],
                           jnp.concatenate(gx, axis=1) + bx[:, gs], lam[:, gs])
        y = _scan_interleaved(a, u, h0[:, gs])
        y_l.append(y)
        h_new.append(y[last:last + 1, :])

    pre = _dot(xnb, wif_ref[...]) + bif_ref[...]
    lane = lax.broadcasted_iota(jnp.int32, (tt, GATE_PAD), 1)
    gcol = jnp.where(lane < M_HEADS, pre, _cumsum_interleaved(-_softplus(-pre)))
    grow = gcol.T
    tri = (_time_of_row(lax.broadcasted_iota(jnp.int32, (tt, 1), 0))
           >= _time_of_row(lax.broadcasted_iota(jnp.int32, (1, tt), 1)))

    merged, mconv_new, c_new, n_new, m_new = [], [], [], [], []
    for h in range(M_HEADS):
        hs = slice(h * M_HD, (h + 1) * M_HD)
        g_l[h] = proj(3 * D_MODEL, h)
        g_m[h] = proj(4 * D_MODEL, h)
        xm_c, cnew = _conv_interleaved(mconv[:, hs], x_m[h], mcw[:, hs], mcb[:, hs])
        mconv_new.append(cnew)
        xcb = (xm_c * _sigmoid(xm_c)).astype(BF)
        q = _dot(xcb, wq_ref[h]) * (M_HD ** -0.5)
        k = _dot(xcb, wk_ref[h])
        v = _dot(x_m[h].astype(BF), wv_ref[h])
        qb, kb, vb = q.astype(BF), k.astype(BF), v.astype(BF)
        ig_col, b_col = gcol[:, h:h + 1], gcol[:, M_HEADS + h:M_HEADS + h + 1]
        ig_row, b_row = grow[h:h + 1, :], grow[M_HEADS + h:M_HEADS + h + 1, :]
        m_prev = m_all[:, h:h + 1]
        c_prev = c_all[h]
        n_prev = n_all[h:h + 1, :]
        gate_l = _sigmoid(g_l[h]) * y_l[h]
        gate_m = _sigmoid(g_m[h]) * _sigmoid(o_m[h])

        dlog = jnp.where(tri, b_col - b_row + ig_row, MASKED)
        m_inter = b_col + m_prev
        m_t = jnp.maximum(m_inter, jnp.max(dlog, axis=1, keepdims=True))
        s = _dot_nt(qb, kb) * jnp.exp(dlog - m_t)
        sc = jnp.exp(m_inter - m_t)
        num = _dot(s.astype(BF), vb) + sc * _dot(qb, c_prev.astype(BF))
        den = jnp.sum(s, axis=1, keepdims=True) + sc * jnp.sum(q * n_prev, axis=1, keepdims=True)
        hh = num / jnp.maximum(jnp.abs(den), jnp.exp(-m_t))

        m_last = m_t[last:last + 1, :]
        b_last = b_col[last:last + 1, :]
        kw = k * jnp.exp(b_last - b_col + ig_col - m_last)
        dec = jnp.exp(b_last + m_prev - m_last)
        c_new.append(dec * c_prev + _dot_tn(kw.astype(BF), vb))
        n_new.append(dec * n_prev + jnp.sum(kw, axis=0, keepdims=True))
        m_new.append(m_last)

        hn = hh * lax.rsqrt(jnp.mean(hh * hh, axis=-1, keepdims=True) + EPS) * mg[:, hs]
        merged.append((gate_l + gate_m * hn).astype(BF))

    merged_t = _dot(permt_ref[...], jnp.concatenate(merged, axis=1)).astype(BF)
    x1_ref[0] = x + _dot(merged_t, wout_ref[...])
    lconv_ref[0] = jnp.concatenate(lconv_new, axis=1)
    mconv_ref[0] = jnp.concatenate(mconv_new, axis=1)
    h_ref[0] = jnp.concatenate(h_new, axis=1)
    for h in range(M_HEADS):
        c_ref[0, h] = c_new[h]
    n_ref[0] = jnp.concatenate(n_new, axis=0)
    m_ref[0] = jnp.concatenate(m_new, axis=1)


def _ffn_tail(x1, p, gffn_ref, wg_ref, wu_ref, wd_ref, gple_ref, wpg_ref, wple_ref, gfin_ref):
    xnb = _rms(x1, gffn_ref[...]).astype(BF)
    x2 = x1
    for c in range(D_FF // FF_CHUNK):
        cs = slice(c * FF_CHUNK, (c + 1) * FF_CHUNK)
        hg = _dot(xnb, wg_ref[:, cs])
        hu = _dot(xnb, wu_ref[:, cs])
        act = (hg * _sigmoid(hg) * hu).astype(BF)
        x2 = x2 + _dot(act, wd_ref[cs, :])
    gate = _sigmoid(_dot(_rms(x2, gple_ref[...]).astype(BF), wpg_ref[...]))
    x3 = x2 + gate * _dot(p.astype(BF), wple_ref[...])
    return _rms(x3, gfin_ref[...])


def _ffn_kernel(x1_ref, p_ref, gffn_ref, wg_ref, wu_ref, wd_ref, gple_ref, wpg_ref, wple_ref,
                gfin_ref, y_ref):
    y_ref[...] = _ffn_tail(x1_ref[...], p_ref[...], gffn_ref, wg_ref, wu_ref, wd_ref, gple_ref,
                           wpg_ref, wple_ref, gfin_ref)


def _front_kernel(x_ref, lbuf_ref, h0_ref, mbuf_ref, gmix_ref, wmain_ref, wif_ref, bif_ref,
                  lcw_ref, lcb_ref, wax_ref, ba_ref, bx_ref, lam_ref, mcw_ref, mcb_ref,
                  wq_ref, wk_ref, wv_ref,
                  q_ref, k_ref, v_ref, g_ref, yl_ref, gm_ref, lbuf_out, h_out, mbuf_out):
    xnb = _rms(x_ref[...], gmix_ref[...]).astype(BF)

    def conv_step(buf_ref, buf_out, x_new, w_ref, b_ref):
        out = b_ref[...] + x_new * w_ref[CONV_W - 1:CONV_W, :]
        for j in range(CONV_W - 1):
            out = out + buf_ref[j] * w_ref[j:j + 1, :]
        for j in range(CONV_W - 2):
            buf_out[j] = buf_ref[j + 1]
        buf_out[CONV_W - 2] = x_new
        return out

    x_l = _dot(xnb, wmain_ref[:, 0:D_MODEL])
    xl_c = conv_step(lbuf_ref, lbuf_out, x_l, lcw_ref, lcb_ref)
    ga, gx = _lru_gates(xl_c.astype(BF), wax_ref, ba_ref[...], bx_ref[...])
    a, u = _lru_coeffs(xl_c, ga, gx, lam_ref[...])
    y_l = a * h0_ref[...] + u
    h_out[...] = y_l
    g_l = _dot(xnb, wmain_ref[:, 3 * D_MODEL:4 * D_MODEL])
    yl_ref[...] = _sigmoid(g_l) * y_l

    x_m = _dot(xnb, wmain_ref[:, D_MODEL:2 * D_MODEL])
    xm_c = conv_step(mbuf_ref, mbuf_out, x_m, mcw_ref, mcb_ref)
    xcb = (xm_c * _sigmoid(xm_c)).astype(BF)
    xmb = x_m.astype(BF)
    for h in range(M_HEADS):
        hs = slice(h * M_HD, (h + 1) * M_HD)
        q_ref[:, hs] = _dot(xcb[:, hs], wq_ref[h]) * (M_HD ** -0.5)
        k_ref[:, hs] = _dot(xcb[:, hs], wk_ref[h])
        v_ref[:, hs] = _dot(xmb[:, hs], wv_ref[h])
    o_m = _dot(xnb, wmain_ref[:, 2 * D_MODEL:3 * D_MODEL])
    g_m = _dot(xnb, wmain_ref[:, 4 * D_MODEL:5 * D_MODEL])
    gm_ref[...] = _sigmoid(g_m) * _sigmoid(o_m)

    pre = _dot(xnb, wif_ref[...]) + bif_ref[...]
    lane = lax.broadcasted_iota(jnp.int32, pre.shape, 1)
    g_ref[...] = jnp.where(lane < M_HEADS, pre, -_softplus(-pre))


def _state_kernel(q_ref, k_ref, v_ref, g_ref, m_ref, n_ref, c_ref,
                  h_out, n_out, m_out, c_out):
    for bb in range(STATE_BATCH):
        b = pl.program_id(0) * STATE_BATCH + bb
        q, k, v = q_ref[b], k_ref[b], v_ref[b]
        g = g_ref[b]
        m_all = m_ref[b]
        n_all = n_ref[b]
        qk_t = jnp.concatenate([q, k], axis=0).T
        for h in range(M_HEADS):
            ig, lf = g[:, h:h + 1], g[:, M_HEADS + h:M_HEADS + h + 1]
            m_prev = m_all[:, h:h + 1]
            q_row, k_row, v_row, n_row = q[h:h + 1], k[h:h + 1], v[h:h + 1], n_all[h:h + 1]
            q_col, k_col = qk_t[:, h:h + 1], qk_t[:, M_HEADS + h:M_HEADS + h + 1]
            c_prev = c_ref[bb, h]

            m_inter = lf + m_prev
            m_t = jnp.maximum(m_inter, ig)
            wk = jnp.exp(ig - m_t)
            s = jnp.sum(q_row * k_row, axis=1, keepdims=True) * wk
            sc = jnp.exp(m_inter - m_t)
            num = s * v_row + sc * jnp.sum(q_col * c_prev, axis=0, keepdims=True)
            den = s + sc * jnp.sum(q_row * n_row, axis=1, keepdims=True)
            h_out[b, h:h + 1] = num / jnp.maximum(jnp.abs(den), jnp.exp(-m_t))
            c_out[bb, h] = sc * c_prev + (wk * k_col) * v_row
            n_out[b, h:h + 1] = sc * n_row + wk * k_row
            m_out[b, :, h:h + 1] = m_t


def _back_kernel(x_ref, hpre_ref, yl_ref, gm_ref, p_ref, mg_ref, wout_ref, gffn_ref, wg_ref,
                 wu_ref, wd_ref, gple_ref, wpg_ref, wple_ref, gfin_ref, y_ref, mrg_ref):
    for h in range(M_HEADS):
        hs = slice(h * M_HD, (h + 1) * M_HD)
        hh = hpre_ref[:, hs]
        hn = hh * lax.rsqrt(jnp.mean(hh * hh, axis=-1, keepdims=True) + EPS) * mg_ref[:, hs]
        mrg_ref[:, hs] = (yl_ref[:, hs] + gm_ref[:, hs] * hn).astype(BF)
    x1 = x_ref[...] + _dot(mrg_ref[...], wout_ref[...])
    y_ref[...] = _ffn_tail(x1, p_ref[...], gffn_ref, wg_ref, wu_ref, wd_ref, gple_ref, wpg_ref,
                           wple_ref, gfin_ref)


def _resident(shape):
    nd = len(shape)
    return pl.BlockSpec(shape, lambda *_: (0,) * nd, pipeline_mode=pl.Buffered(1))


def _params(n_axes):
    return pltpu.CompilerParams(dimension_semantics=("arbitrary",) * n_axes,
                                vmem_limit_bytes=VMEM_LIMIT)


def _interleave_matrix():
    r = np.arange(SEQ_TILE)
    perm = np.zeros((SEQ_TILE, SEQ_TILE), np.float32)
    perm[r, (r % SUBLANES) * SEG + r // SUBLANES] = 1.0
    return perm


def kernel(x_prompt, x_sample, state_lru_conv, state_lru_h, state_mlstm_conv, state_mlstm_C, state_mlstm_n, state_mlstm_m, p_prompt, p_sample, norm_mix_g, w_in, b_gates, lru_conv_w, lru_conv_b, lru_w_a, lru_b_a, lru_w_x, lru_b_x, lru_lambda, mlstm_conv_w, mlstm_conv_b, w_q, w_k, w_v, mlstm_norm_g, w_out, norm_ffn_g, w_ffn_gate, w_ffn_up, w_ffn_down, norm_ple_g, w_ple_gate, w_ple, final_norm_g):
    assert w_in.shape[0] == 1, "single-layer trunk"
    B, T, _ = x_prompt.shape
    S = x_sample.shape[0]
    assert T % SEQ_TILE == 0 and (B * T) % FFN_TILE == 0 and x_sample.shape[1] == 1
    assert S % STATE_BATCH == 0

    g0 = 2 * D_MODEL + D_MODEL
    w0 = w_in[0]
    wmain = jnp.concatenate([w0[:, :g0], w0[:, g0 + N_GATES:]], axis=1).astype(BF)
    wif = jnp.pad(w0[:, g0:g0 + N_GATES], ((0, 0), (0, GATE_PAD - N_GATES))).astype(BF)
    bif = jnp.pad(b_gates[0], (0, GATE_PAD - N_GATES)).reshape(1, GATE_PAD)
    wax = jnp.concatenate([lru_w_a[0], lru_w_x[0]], axis=2).astype(BF)
    row = lambda a: a.reshape(1, -1)
    gmix, ba, bx, lam = row(norm_mix_g[0]), row(lru_b_a[0]), row(lru_b_x[0]), row(lru_lambda[0])
    lcw, lcb = lru_conv_w[0], row(lru_conv_b[0])
    mcw, mcb = mlstm_conv_w[0], row(mlstm_conv_b[0])
    wq, wk, wv = w_q[0].astype(BF), w_k[0].astype(BF), w_v[0].astype(BF)
    mg = row(mlstm_norm_g[0])
    wout = w_out[0].astype(BF)
    gffn, gple, gfin = row(norm_ffn_g[0]), row(norm_ple_g[0]), row(final_norm_g)
    wg, wu, wd = w_ffn_gate[0].astype(BF), w_ffn_up[0].astype(BF), w_ffn_down[0].astype(BF)
    wpg, wple = w_ple_gate[0].astype(BF), w_ple[0].astype(BF)
    perm_np = _interleave_matrix()
    perm, perm_t = jnp.asarray(perm_np, BF), jnp.asarray(perm_np.T, BF)

    front_w = (gmix, wmain, wif, bif, lcw, lcb, wax, ba, bx, lam, mcw, mcb, wq, wk, wv)
    mixer_w = (perm, perm_t) + front_w + (mg, wout)
    ffn_w = (gffn, wg, wu, wd, gple, wpg, wple, gfin)

    nt = T // SEQ_TILE
    sds = jax.ShapeDtypeStruct
    x1, p_lconv, p_h, p_mconv, p_c, p_n, p_m = pl.pallas_call(
        _mixer_kernel,
        grid=(B, nt),
        in_specs=[pl.BlockSpec((1, SEQ_TILE, D_MODEL), lambda b, t: (b, t, 0))]
                 + [_resident(w.shape) for w in mixer_w],
        out_specs=[
            pl.BlockSpec((1, SEQ_TILE, D_MODEL), lambda b, t: (b, t, 0)),
            pl.BlockSpec((1, CONV_W - 1, D_MODEL), lambda b, t: (b, 0, 0)),
            pl.BlockSpec((1, 1, D_MODEL), lambda b, t: (b, 0, 0)),
            pl.BlockSpec((1, CONV_W - 1, D_MODEL), lambda b, t: (b, 0, 0)),
            pl.BlockSpec((1, M_HEADS, M_HD, M_HD), lambda b, t: (b, 0, 0, 0)),
            pl.BlockSpec((1, M_HEADS, M_HD), lambda b, t: (b, 0, 0)),
            pl.BlockSpec((1, 1, M_HEADS), lambda b, t: (b, 0, 0)),
        ],
        out_shape=[
            sds((B, T, D_MODEL), F32),
            sds((B, CONV_W - 1, D_MODEL), F32),
            sds((B, 1, D_MODEL), F32),
            sds((B, CONV_W - 1, D_MODEL), F32),
            sds((B, M_HEADS, M_HD, M_HD), F32),
            sds((B, M_HEADS, M_HD), F32),
            sds((B, 1, M_HEADS), F32),
        ],
        compiler_params=_params(2),
        name="prompt_mixer",
    )(x_prompt, *mixer_w)

    n_tok = B * T
    y_prompt = pl.pallas_call(
        _ffn_kernel,
        grid=(n_tok // FFN_TILE,),
        in_specs=[pl.BlockSpec((FFN_TILE, D_MODEL), lambda i: (i, 0)),
                  pl.BlockSpec((FFN_TILE, P_DIM), lambda i: (i, 0))]
                 + [_resident(w.shape) for w in ffn_w],
        out_specs=pl.BlockSpec((FFN_TILE, D_MODEL), lambda i: (i, 0)),
        out_shape=sds((n_tok, D_MODEL), F32),
        compiler_params=_params(1),
        name="prompt_ffn",
    )(x1.reshape(n_tok, D_MODEL), p_prompt[0].reshape(n_tok, P_DIM), *ffn_w)

    xs = x_sample.reshape(S, D_MODEL)
    lbuf = jnp.swapaxes(state_lru_conv[0], 0, 1)
    mbuf = jnp.swapaxes(state_mlstm_conv[0], 0, 1)
    tok = sds((S, D_MODEL), F32)
    buf = sds((CONV_W - 1, S, D_MODEL), F32)
    q, k, v, g, yl, gm, s_lbuf, s_h, s_mbuf = pl.pallas_call(
        _front_kernel,
        out_shape=[tok, tok, tok, sds((S, GATE_PAD), F32), tok, tok, buf, tok, buf],
        compiler_params=pltpu.CompilerParams(vmem_limit_bytes=VMEM_LIMIT),
        name="sample_front",
    )(xs, lbuf, state_lru_h[0], mbuf, *front_w)

    heads = lambda a: a.reshape(S, M_HEADS, M_HD)
    whole = lambda shape: pl.BlockSpec(shape, lambda b: (0,) * len(shape))
    c_spec = pl.BlockSpec((STATE_BATCH, M_HEADS, M_HD, M_HD), lambda b: (b, 0, 0, 0))
    hv, gv, mv = (S, M_HEADS, M_HD), (S, 1, GATE_PAD), (S, 1, M_HEADS)
    hpre, s_n, s_m, s_c = pl.pallas_call(
        _state_kernel,
        grid=(S // STATE_BATCH,),
        in_specs=[whole(hv), whole(hv), whole(hv), whole(gv), whole(mv), whole(hv), c_spec],
        out_specs=[whole(hv), whole(hv), whole(mv), c_spec],
        out_shape=[sds(hv, F32), sds(hv, F32), sds(mv, F32),
                   sds((S, M_HEADS, M_HD, M_HD), F32)],
        compiler_params=_params(1),
        name="sample_state",
    )(heads(q), heads(k), heads(v), g.reshape(gv), state_mlstm_m[0].reshape(mv),
      state_mlstm_n[0], state_mlstm_C[0])

    y_sample = pl.pallas_call(
        _back_kernel,
        out_shape=tok,
        scratch_shapes=[pltpu.VMEM((S, D_MODEL), BF)],
        compiler_params=pltpu.CompilerParams(vmem_limit_bytes=VMEM_LIMIT),
        name="sample_back",
    )(xs, hpre.reshape(S, D_MODEL), yl, gm, p_sample[0].reshape(S, P_DIM), mg, wout, *ffn_w)

    lead = lambda a: a[None]
    return (y_prompt.reshape(B, T, D_MODEL), y_sample.reshape(S, 1, D_MODEL),
            lead(p_lconv), p_h.reshape(1, B, D_MODEL), lead(p_mconv), lead(p_c), lead(p_n),
            p_m.reshape(1, B, M_HEADS),
            lead(jnp.swapaxes(s_lbuf, 0, 1)), lead(s_h), lead(jnp.swapaxes(s_mbuf, 0, 1)),
            lead(s_c), lead(s_n), s_m.reshape(1, S, M_HEADS))
```

```python
import numpy as np

import jax
import jax.numpy as jnp
from jax import lax
from jax.experimental import pallas as pl
from jax.experimental.pallas import tpu as pltpu

D_MODEL = 1024
M_HEADS = 4
M_HD = D_MODEL // M_HEADS
LRU_BLOCKS = 8
LRU_BLK = D_MODEL // LRU_BLOCKS
LRU_C = 8.0
CONV_W = 4
D_FF = 2816
P_DIM = 256
EPS = 1e-6
M_INIT = -1e30
MASKED = -1e30
N_GATES = 2 * M_HEADS
GATE_PAD = 128
SUBLANES = 8
SEQ_TILE = 256
SEG = SEQ_TILE // SUBLANES
FFN_TILE = 512
FF_CHUNK = 256
STATE_BATCH = 4
VMEM_LIMIT = 56 * 1024 * 1024

BF = jnp.bfloat16
F32 = jnp.float32


def _dot(a, b):
    return jnp.dot(a, b, preferred_element_type=F32)


def _dot_nt(a, b):
    return lax.dot_general(a, b, (((1,), (1,)), ((), ())), preferred_element_type=F32)


def _dot_tn(a, b):
    return lax.dot_general(a, b, (((0,), (0,)), ((), ())), preferred_element_type=F32)


def _sigmoid(x):
    return 1.0 / (1.0 + jnp.exp(-x))


def _softplus(x):
    return jnp.maximum(x, 0.0) + jnp.log1p(jnp.exp(-jnp.abs(x)))


def _rms(x, g):
    return x * lax.rsqrt(jnp.mean(x * x, axis=-1, keepdims=True) + EPS) * g


def _group(x, i):
    return x[i * SUBLANES:(i + 1) * SUBLANES, :]


def _lru_coeffs(xc, ga, gx, lam):
    r = _sigmoid(ga)
    ig = _sigmoid(gx)
    log_a = -LRU_C * r * _softplus(-lam)
    a = jnp.exp(log_a)
    om = 1.0 - a * a
    root = jnp.where(om > 0.0, om * lax.rsqrt(om), 0.0)
    return a, root * ig * xc


def _lru_gates(xcb, wax_ref, ba, bx):
    ga, gx = [], []
    for n in range(LRU_BLOCKS):
        g = _dot(xcb[:, n * LRU_BLK:(n + 1) * LRU_BLK], wax_ref[n])
        ga.append(g[:, :LRU_BLK])
        gx.append(g[:, LRU_BLK:])
    return jnp.concatenate(ga, axis=1) + ba, jnp.concatenate(gx, axis=1) + bx


def _scan_interleaved(a, u, h0):
    prods, sums = [], []
    p = s = None
    for i in range(SEG):
        ai, ui = _group(a, i), _group(u, i)
        p, s = (ai, ui) if i == 0 else (ai * p, ai * s + ui)
        prods.append(p)
        sums.append(s)
    c = h0
    starts = [c]
    for j in range(SUBLANES - 1):
        c = p[j:j + 1, :] * c + s[j:j + 1, :]
        starts.append(c)
    start = jnp.concatenate(starts, axis=0)
    return jnp.concatenate([sums[i] + prods[i] * start for i in range(SEG)], axis=0)


def _cumsum_interleaved(x):
    acc = []
    run = None
    for i in range(SEG):
        run = _group(x, i) if i == 0 else run + _group(x, i)
        acc.append(run)
    sub = lax.broadcasted_iota(jnp.int32, run.shape, 0)
    inc = run
    s = 1
    while s < SUBLANES:
        inc = inc + jnp.where(sub >= s, pltpu.roll(inc, s, 0), 0.0)
        s *= 2
    before = inc - run
    return jnp.concatenate([r + before for r in acc], axis=0)


def _conv_interleaved(carry, x_new, w, b):
    sub = lax.broadcasted_iota(jnp.int32, (SUBLANES, x_new.shape[1]), 0)
    head, new = [], []
    for r in range(CONV_W - 1):
        cur = _group(x_new, SEG - (CONV_W - 1) + r)
        head.append(jnp.where(sub == 0, carry[r:r + 1, :], pltpu.roll(cur, 1, 0)))
        new.append(cur[SUBLANES - 1:SUBLANES, :])
    ext = jnp.concatenate(head + [x_new], axis=0)
    out = b
    for j in range(CONV_W):
        out = out + ext[j * SUBLANES:j * SUBLANES + SEQ_TILE, :] * w[j:j + 1, :]
    return out, jnp.concatenate(new, axis=0)


def _time_of_row(r):
    return (r & (SUBLANES - 1)) * SEG + (r >> (SUBLANES.bit_length() - 1))


def _mixer_kernel(x_ref, perm_ref, permt_ref, gmix_ref, wmain_ref, wif_ref, bif_ref, lcw_ref,
                  lcb_ref, wax_ref, ba_ref, bx_ref, lam_ref, mcw_ref, mcb_ref, wq_ref, wk_ref,
                  wv_ref, mg_ref, wout_ref,
                  x1_ref, lconv_ref, h_ref, mconv_ref, c_ref, n_ref, m_ref):
    tt = SEQ_TILE
    last = tt - 1

    @pl.when(pl.program_id(1) == 0)
    def _():
        lconv_ref[...] = jnp.zeros(lconv_ref.shape, F32)
        mconv_ref[...] = jnp.zeros(mconv_ref.shape, F32)
        h_ref[...] = jnp.zeros(h_ref.shape, F32)
        c_ref[...] = jnp.zeros(c_ref.shape, F32)
        n_ref[...] = jnp.zeros(n_ref.shape, F32)
        m_ref[...] = jnp.full(m_ref.shape, M_INIT, F32)

    x = x_ref[0]
    lconv, mconv, h0 = lconv_ref[0], mconv_ref[0], h_ref[0]
    c_all = [c_ref[0, h] for h in range(M_HEADS)]
    n_all, m_all = n_ref[0], m_ref[0]
    lcw, lcb, mcw, mcb = lcw_ref[...], lcb_ref[...], mcw_ref[...], mcb_ref[...]
    ba, bx, lam, mg = ba_ref[...], bx_ref[...], lam_ref[...], mg_ref[...]

    xn_t = _rms(x, gmix_ref[...]).astype(BF)
    xnb = _dot(perm_ref[...], xn_t).astype(BF)

    def proj(col0, g):
        return _dot(xnb, wmain_ref[:, col0 + g * M_HD:col0 + (g + 1) * M_HD])

    x_l = {0: proj(0, 0)}
    x_m, o_m, g_l, g_m = {}, {}, {}, {}

    y_l, lconv_new, h_new = [], [], []
    for g in range(M_HEADS):
        gs = slice(g * M_HD, (g + 1) * M_HD)
        if g + 1 < M_HEADS:
            x_l[g + 1] = proj(0, g + 1)
        x_m[g] = proj(D_MODEL, g)
        o_m[g] = proj(2 * D_MODEL, g)
        xl_c, cnew = _conv_interleaved(lconv[:, gs], x_l[g], lcw[:, gs], lcb[:, gs])
        lconv_new.append(cnew)
        xlb = xl_c.astype(BF)
        ga, gx = [], []
        for n in range(2):
            gg = _dot(xlb[:, n * LRU_BLK:(n + 1) * LRU_BLK], wax_ref[2 * g + n])
            ga.append(gg[:, :LRU_BLK])
            gx.append(gg[:, LRU_BLK:])
        a, u = _lru_coeffs(xl_c, jnp.concatenate(ga, axis=1) + ba[:, gs],
                           jnp.concatenate(gx, axis=1) + bx[:, gs], lam[:, gs])
        y = _scan_interleaved(a, u, h0[:, gs])
        y_l.append(y)
        h_new.append(y[last:last + 1, :])

    pre = _dot(xnb, wif_ref[...]) + bif_ref[...]
    lane = lax.broadcasted_iota(jnp.int32, (tt, GATE_PAD), 1)
    gcol = jnp.where(lane < M_HEADS, pre, _cumsum_interleaved(-_softplus(-pre)))
    grow = gcol.T
    tri = (_time_of_row(lax.broadcasted_iota(jnp.int32, (tt, 1), 0))
           >= _time_of_row(lax.broadcasted_iota(jnp.int32, (1, tt), 1)))

    merged, mconv_new, c_new, n_new, m_new = [], [], [], [], []
    for h in range(M_HEADS):
        hs = slice(h * M_HD, (h + 1) * M_HD)
        g_l[h] = proj(3 * D_MODEL, h)
        g_m[h] = proj(4 * D_MODEL, h)
        xm_c, cnew = _conv_interleaved(mconv[:, hs], x_m[h], mcw[:, hs], mcb[:, hs])
        mconv_new.append(cnew)
        xcb = (xm_c * _sigmoid(xm_c)).astype(BF)
        q = _dot(xcb, wq_ref[h]) * (M_HD ** -0.5)
        k = _dot(xcb, wk_ref[h])
        v = _dot(x_m[h].astype(BF), wv_ref[h])
        qb, kb, vb = q.astype(BF), k.astype(BF), v.astype(BF)
        ig_col, b_col = gcol[:, h:h + 1], gcol[:, M_HEADS + h:M_HEADS + h + 1]
        ig_row, b_row = grow[h:h + 1, :], grow[M_HEADS + h:M_HEADS + h + 1, :]
        m_prev = m_all[:, h:h + 1]
        c_prev = c_all[h]
        n_prev = n_all[h:h + 1, :]
        gate_l = _sigmoid(g_l[h]) * y_l[h]
        gate_m = _sigmoid(g_m[h]) * _sigmoid(o_m[h])

        dlog = jnp.where(tri, b_col - b_row + ig_row, MASKED)
        m_inter = b_col + m_prev
        m_t = jnp.maximum(m_inter, jnp.max(dlog, axis=1, keepdims=True))
        s = _dot_nt(qb, kb) * jnp.exp(dlog - m_t)
        sc = jnp.exp(m_inter - m_t)
        num = _dot(s.astype(BF), vb) + sc * _dot(qb, c_prev.astype(BF))
        den = jnp.sum(s, axis=1, keepdims=True) + sc * jnp.sum(q * n_prev, axis=1, keepdims=True)
        hh = num / jnp.maximum(jnp.abs(den), jnp.exp(-m_t))

        m_last = m_t[last:last + 1, :]
        b_last = b_col[last:last + 1, :]
        kw = k * jnp.exp(b_last - b_col + ig_col - m_last)
        dec = jnp.exp(b_last + m_prev - m_last)
        c_new.append(dec * c_prev + _dot_tn(kw.astype(BF), vb))
        n_new.append(dec * n_prev + jnp.sum(kw, axis=0, keepdims=True))
        m_new.append(m_last)

        hn = hh * lax.rsqrt(jnp.mean(hh * hh, axis=-1, keepdims=True) + EPS) * mg[:, hs]
        merged.append((gate_l + gate_m * hn).astype(BF))

    merged_t = _dot(permt_ref[...], jnp.concatenate(merged, axis=1)).astype(BF)
    x1_ref[0] = x + _dot(merged_t, wout_ref[...])
    lconv_ref[0] = jnp.concatenate(lconv_new, axis=1)
    mconv_ref[0] = jnp.concatenate(mconv_new, axis=1)
    h_ref[0] = jnp.concatenate(h_new, axis=1)
    for h in range(M_HEADS):
        c_ref[0, h] = c_new[h]
    n_ref[0] = jnp.concatenate(n_new, axis=0)
    m_ref[0] = jnp.concatenate(m_new, axis=1)


def _ffn_tail(x1, p, gffn_ref, wg_ref, wu_ref, wd_ref, gple_ref, wpg_ref, wple_ref, gfin_ref):
    xnb = _rms(x1, gffn_ref[...]).astype(BF)
    x2 = x1
    for c in range(D_FF // FF_CHUNK):
        cs = slice(c * FF_CHUNK, (c + 1) * FF_CHUNK)
        hg = _dot(xnb, wg_ref[:, cs])
        hu = _dot(xnb, wu_ref[:, cs])
        act = (hg * _sigmoid(hg) * hu).astype(BF)
        x2 = x2 + _dot(act, wd_ref[cs, :])
    gate = _sigmoid(_dot(_rms(x2, gple_ref[...]).astype(BF), wpg_ref[...]))
    x3 = x2 + gate * _dot(p.astype(BF), wple_ref[...])
    return _rms(x3, gfin_ref[...])


def _ffn_kernel(x1_ref, p_ref, gffn_ref, wg_ref, wu_ref, wd_ref, gple_ref, wpg_ref, wple_ref,
                gfin_ref, y_ref):
    y_ref[...] = _ffn_tail(x1_ref[...], p_ref[...], gffn_ref, wg_ref, wu_ref, wd_ref, gple_ref,
                           wpg_ref, wple_ref, gfin_ref)


def _front_kernel(x_ref, lbuf_ref, h0_ref, mbuf_ref, gmix_ref, wmain_ref, wif_ref, bif_ref,
                  lcw_ref, lcb_ref, wax_ref, ba_ref, bx_ref, lam_ref, mcw_ref, mcb_ref,
                  wq_ref, wk_ref, wv_ref, m0_ref, n0_ref,
                  q_ref, kw_ref, v_ref, sc_ref, hv_ref, hc_ref, n_out, m_out,
                  yl_ref, gm_ref, lbuf_out, h_out, mbuf_out):
    xnb = _rms(x_ref[...], gmix_ref[...]).astype(BF)

    def per_head(cols):
        return jnp.concatenate([jnp.broadcast_to(cols[:, h:h + 1], (cols.shape[0], M_HD))
                                for h in range(M_HEADS)], axis=1)

    def head_sums(a):
        return jnp.concatenate([jnp.sum(a[:, h * M_HD:(h + 1) * M_HD], axis=1, keepdims=True)
                                for h in range(M_HEADS)], axis=1)

    def conv_step(buf_ref, buf_out, x_new, w_ref, b_ref):
        out = b_ref[...] + x_new * w_ref[CONV_W - 1:CONV_W, :]
        for j in range(CONV_W - 1):
            out = out + buf_ref[j] * w_ref[j:j + 1, :]
        for j in range(CONV_W - 2):
            buf_out[j] = buf_ref[j + 1]
        buf_out[CONV_W - 2] = x_new
        return out

    x_l = _dot(xnb, wmain_ref[:, 0:D_MODEL])
    xl_c = conv_step(lbuf_ref, lbuf_out, x_l, lcw_ref, lcb_ref)
    ga, gx = _lru_gates(xl_c.astype(BF), wax_ref, ba_ref[...], bx_ref[...])
    a, u = _lru_coeffs(xl_c, ga, gx, lam_ref[...])
    y_l = a * h0_ref[...] + u
    h_out[...] = y_l
    g_l = _dot(xnb, wmain_ref[:, 3 * D_MODEL:4 * D_MODEL])
    yl_ref[...] = _sigmoid(g_l) * y_l

    x_m = _dot(xnb, wmain_ref[:, D_MODEL:2 * D_MODEL])
    xm_c = conv_step(mbuf_ref, mbuf_out, x_m, mcw_ref, mcb_ref)
    xcb = (xm_c * _sigmoid(xm_c)).astype(BF)
    xmb = x_m.astype(BF)
    q, k, v = [], [], []
    for h in range(M_HEADS):
        hs = slice(h * M_HD, (h + 1) * M_HD)
        q.append(_dot(xcb[:, hs], wq_ref[h]) * (M_HD ** -0.5))
        k.append(_dot(xcb[:, hs], wk_ref[h]))
        v.append(_dot(xmb[:, hs], wv_ref[h]))
    q, k, v = (jnp.concatenate(a, axis=1) for a in (q, k, v))
    o_m = _dot(xnb, wmain_ref[:, 2 * D_MODEL:3 * D_MODEL])
    g_m = _dot(xnb, wmain_ref[:, 4 * D_MODEL:5 * D_MODEL])
    gm_ref[...] = _sigmoid(g_m) * _sigmoid(o_m)

    pre = _dot(xnb, wif_ref[...]) + bif_ref[...]
    ig = pre[:, 0:M_HEADS]
    lf = -_softplus(-pre[:, M_HEADS:N_GATES])
    n_prev = n0_ref[...]
    m_inter = lf + m0_ref[...]
    m_t = jnp.maximum(m_inter, ig)
    wk = jnp.exp(ig - m_t)
    sc = jnp.exp(m_inter - m_t)
    s = head_sums(q * k) * wk
    den = s + sc * head_sums(q * n_prev)
    rden = 1.0 / jnp.maximum(jnp.abs(den), jnp.exp(-m_t))
    wk_d, sc_d = per_head(wk), per_head(sc)
    hv_ref[...] = per_head(s * rden) * v
    hc_ref[...] = per_head(sc * rden)
    q_ref[...] = q
    kw_ref[...] = wk_d * k
    v_ref[...] = v
    sc_ref[...] = sc
    n_out[...] = sc_d * n_prev + wk_d * k
    m_out[...] = m_t


def _state_kernel(sc_ref, q_ref, kw_ref, v_ref, c_ref, qc_out, c_out):
    base = pl.program_id(0) * STATE_BATCH
    cols = []
    for bb in range(STATE_BATCH):
        b = base + bb
        cols.append(jnp.concatenate([q_ref[b], kw_ref[b]], axis=0).T)
    for bb in range(STATE_BATCH):
        b = base + bb
        v = v_ref[b]
        for h in range(M_HEADS):
            q_col, kw_col = cols[bb][:, h:h + 1], cols[bb][:, M_HEADS + h:M_HEADS + h + 1]
            c_prev = c_ref[bb, h]
            qc_out[b, h:h + 1] = jnp.sum(q_col * c_prev, axis=0, keepdims=True)
            c_out[bb, h] = sc_ref[b, h] * c_prev + kw_col * v[h:h + 1]


def _back_kernel(x_ref, qc_ref, hv_ref, hc_ref, yl_ref, gm_ref, p_ref, mg_ref, wout_ref, gffn_ref,
                 wg_ref, wu_ref, wd_ref, gple_ref, wpg_ref, wple_ref, gfin_ref, y_ref, mrg_ref):
    for h in range(M_HEADS):
        hs = slice(h * M_HD, (h + 1) * M_HD)
        hh = hv_ref[:, hs] + hc_ref[:, hs] * qc_ref[:, hs]
        hn = hh * lax.rsqrt(jnp.mean(hh * hh, axis=-1, keepdims=True) + EPS) * mg_ref[:, hs]
        mrg_ref[:, hs] = (yl_ref[:, hs] + gm_ref[:, hs] * hn).astype(BF)
    x1 = x_ref[...] + _dot(mrg_ref[...], wout_ref[...])
    y_ref[...] = _ffn_tail(x1, p_ref[...], gffn_ref, wg_ref, wu_ref, wd_ref, gple_ref, wpg_ref,
                           wple_ref, gfin_ref)


def _resident(shape):
    nd = len(shape)
    return pl.BlockSpec(shape, lambda *_: (0,) * nd, pipeline_mode=pl.Buffered(1))


def _params(n_axes):
    return pltpu.CompilerParams(dimension_semantics=("arbitrary",) * n_axes,
                                vmem_limit_bytes=VMEM_LIMIT)


def _interleave_matrix():
    r = np.arange(SEQ_TILE)
    perm = np.zeros((SEQ_TILE, SEQ_TILE), np.float32)
    perm[r, (r % SUBLANES) * SEG + r // SUBLANES] = 1.0
    return perm


def kernel(x_prompt, x_sample, state_lru_conv, state_lru_h, state_mlstm_conv, state_mlstm_C, state_mlstm_n, state_mlstm_m, p_prompt, p_sample, norm_mix_g, w_in, b_gates, lru_conv_w, lru_conv_b, lru_w_a, lru_b_a, lru_w_x, lru_b_x, lru_lambda, mlstm_conv_w, mlstm_conv_b, w_q, w_k, w_v, mlstm_norm_g, w_out, norm_ffn_g, w_ffn_gate, w_ffn_up, w_ffn_down, norm_ple_g, w_ple_gate, w_ple, final_norm_g):
    assert w_in.shape[0] == 1, "single-layer trunk"
    B, T, _ = x_prompt.shape
    S = x_sample.shape[0]
    assert T % SEQ_TILE == 0 and (B * T) % FFN_TILE == 0 and x_sample.shape[1] == 1
    assert S % STATE_BATCH == 0

    g0 = 2 * D_MODEL + D_MODEL
    w0 = w_in[0]
    wmain = jnp.concatenate([w0[:, :g0], w0[:, g0 + N_GATES:]], axis=1).astype(BF)
    wif = jnp.pad(w0[:, g0:g0 + N_GATES], ((0, 0), (0, GATE_PAD - N_GATES))).astype(BF)
    bif = jnp.pad(b_gates[0], (0, GATE_PAD - N_GATES)).reshape(1, GATE_PAD)
    wax = jnp.concatenate([lru_w_a[0], lru_w_x[0]], axis=2).astype(BF)
    row = lambda a: a.reshape(1, -1)
    gmix, ba, bx, lam = row(norm_mix_g[0]), row(lru_b_a[0]), row(lru_b_x[0]), row(lru_lambda[0])
    lcw, lcb = lru_conv_w[0], row(lru_conv_b[0])
    mcw, mcb = mlstm_conv_w[0], row(mlstm_conv_b[0])
    wq, wk, wv = w_q[0].astype(BF), w_k[0].astype(BF), w_v[0].astype(BF)
    mg = row(mlstm_norm_g[0])
    wout = w_out[0].astype(BF)
    gffn, gple, gfin = row(norm_ffn_g[0]), row(norm_ple_g[0]), row(final_norm_g)
    wg, wu, wd = w_ffn_gate[0].astype(BF), w_ffn_up[0].astype(BF), w_ffn_down[0].astype(BF)
    wpg, wple = w_ple_gate[0].astype(BF), w_ple[0].astype(BF)
    perm_np = _interleave_matrix()
    perm, perm_t = jnp.asarray(perm_np, BF), jnp.asarray(perm_np.T, BF)

    front_w = (gmix, wmain, wif, bif, lcw, lcb, wax, ba, bx, lam, mcw, mcb, wq, wk, wv)
    mixer_w = (perm, perm_t) + front_w + (mg, wout)
    ffn_w = (gffn, wg, wu, wd, gple, wpg, wple, gfin)

    nt = T // SEQ_TILE
    sds = jax.ShapeDtypeStruct
    x1, p_lconv, p_h, p_mconv, p_c, p_n, p_m = pl.pallas_call(
        _mixer_kernel,
        grid=(B, nt),
        in_specs=[pl.BlockSpec((1, SEQ_TILE, D_MODEL), lambda b, t: (b, t, 0))]
                 + [_resident(w.shape) for w in mixer_w],
        out_specs=[
            pl.BlockSpec((1, SEQ_TILE, D_MODEL), lambda b, t: (b, t, 0)),
            pl.BlockSpec((1, CONV_W - 1, D_MODEL), lambda b, t: (b, 0, 0)),
            pl.BlockSpec((1, 1, D_MODEL), lambda b, t: (b, 0, 0)),
            pl.BlockSpec((1, CONV_W - 1, D_MODEL), lambda b, t: (b, 0, 0)),
            pl.BlockSpec((1, M_HEADS, M_HD, M_HD), lambda b, t: (b, 0, 0, 0)),
            pl.BlockSpec((1, M_HEADS, M_HD), lambda b, t: (b, 0, 0)),
            pl.BlockSpec((1, 1, M_HEADS), lambda b, t: (b, 0, 0)),
        ],
        out_shape=[
            sds((B, T, D_MODEL), F32),
            sds((B, CONV_W - 1, D_MODEL), F32),
            sds((B, 1, D_MODEL), F32),
            sds((B, CONV_W - 1, D_MODEL), F32),
            sds((B, M_HEADS, M_HD, M_HD), F32),
            sds((B, M_HEADS, M_HD), F32),
            sds((B, 1, M_HEADS), F32),
        ],
        compiler_params=_params(2),
        name="prompt_mixer",
    )(x_prompt, *mixer_w)

    n_tok = B * T
    y_prompt = pl.pallas_call(
        _ffn_kernel,
        grid=(n_tok // FFN_TILE,),
        in_specs=[pl.BlockSpec((FFN_TILE, D_MODEL), lambda i: (i, 0)),
                  pl.BlockSpec((FFN_TILE, P_DIM), lambda i: (i, 0))]
                 + [_resident(w.shape) for w in ffn_w],
        out_specs=pl.BlockSpec((FFN_TILE, D_MODEL), lambda i: (i, 0)),
        out_shape=sds((n_tok, D_MODEL), F32),
        compiler_params=_params(1),
        name="prompt_ffn",
    )(x1.reshape(n_tok, D_MODEL), p_prompt[0].reshape(n_tok, P_DIM), *ffn_w)

    xs = x_sample.reshape(S, D_MODEL)
    lbuf = jnp.swapaxes(state_lru_conv[0], 0, 1)
    mbuf = jnp.swapaxes(state_mlstm_conv[0], 0, 1)
    tok = sds((S, D_MODEL), F32)
    buf = sds((CONV_W - 1, S, D_MODEL), F32)
    per_head = sds((S, M_HEADS), F32)
    q, kw, v, sc, hv, hc, s_n, s_m, yl, gm, s_lbuf, s_h, s_mbuf = pl.pallas_call(
        _front_kernel,
        out_shape=[tok, tok, tok, per_head, tok, tok, tok, per_head, tok, tok, buf, tok, buf],
        compiler_params=pltpu.CompilerParams(vmem_limit_bytes=VMEM_LIMIT),
        name="sample_front",
    )(xs, lbuf, state_lru_h[0], mbuf, *front_w, state_mlstm_m[0],
      state_mlstm_n[0].reshape(S, D_MODEL))

    heads = lambda a: a.reshape(S, M_HEADS, M_HD)
    rows = (S, M_HEADS, M_HD)
    whole = pl.BlockSpec(rows, lambda b: (0, 0, 0))
    c_spec = pl.BlockSpec((STATE_BATCH, M_HEADS, M_HD, M_HD), lambda b: (b, 0, 0, 0))
    qc, s_c = pl.pallas_call(
        _state_kernel,
        grid=(S // STATE_BATCH,),
        in_specs=[pl.BlockSpec(memory_space=pltpu.SMEM), whole, whole, whole, c_spec],
        out_specs=[whole, c_spec],
        out_shape=[sds(rows, F32), sds((S, M_HEADS, M_HD, M_HD), F32)],
        compiler_params=_params(1),
        name="sample_state",
    )(sc, heads(q), heads(kw), heads(v), state_mlstm_C[0])

    y_sample = pl.pallas_call(
        _back_kernel,
        out_shape=tok,
        scratch_shapes=[pltpu.VMEM((S, D_MODEL), BF)],
        compiler_params=pltpu.CompilerParams(vmem_limit_bytes=VMEM_LIMIT),
        name="sample_back",
    )(xs, qc.reshape(S, D_MODEL), hv, hc, yl, gm, p_sample[0].reshape(S, P_DIM), mg, wout, *ffn_w)

    lead = lambda a: a[None]
    return (y_prompt.reshape(B, T, D_MODEL), y_sample.reshape(S, 1, D_MODEL),
            lead(p_lconv), p_h.reshape(1, B, D_MODEL), lead(p_mconv), lead(p_c), lead(p_n),
            p_m.reshape(1, B, M_HEADS),
            lead(jnp.swapaxes(s_lbuf, 0, 1)), lead(s_h), lead(jnp.swapaxes(s_mbuf, 0, 1)),
            lead(s_c), lead(heads(s_n)), lead(s_m))
```

```python
import numpy as np

import jax
import jax.numpy as jnp
from jax import lax
from jax.experimental import pallas as pl
from jax.experimental.pallas import tpu as pltpu

D_MODEL = 1024
M_HEADS = 4
M_HD = D_MODEL // M_HEADS
LRU_BLOCKS = 8
LRU_BLK = D_MODEL // LRU_BLOCKS
LRU_C = 8.0
CONV_W = 4
D_FF = 2816
P_DIM = 256
EPS = 1e-6
M_INIT = -1e30
MASKED = -1e30
N_GATES = 2 * M_HEADS
GATE_PAD = 128
SUBLANES = 8
SEQ_TILE = 256
SEG = SEQ_TILE // SUBLANES
FFN_TILE = 512
FF_CHUNK = 256
STATE_BATCH = 4
VMEM_LIMIT = 56 * 1024 * 1024

BF = jnp.bfloat16
F32 = jnp.float32


def _dot(a, b):
    return jnp.dot(a, b, preferred_element_type=F32)


def _dot_nt(a, b):
    return lax.dot_general(a, b, (((1,), (1,)), ((), ())), preferred_element_type=F32)


def _dot_tn(a, b):
    return lax.dot_general(a, b, (((0,), (0,)), ((), ())), preferred_element_type=F32)


def _sigmoid(x):
    return 1.0 / (1.0 + jnp.exp(-x))


def _softplus(x):
    return jnp.maximum(x, 0.0) + jnp.log1p(jnp.exp(-jnp.abs(x)))


def _rms(x, g):
    return x * lax.rsqrt(jnp.mean(x * x, axis=-1, keepdims=True) + EPS) * g


def _group(x, i):
    return x[i * SUBLANES:(i + 1) * SUBLANES, :]


def _lru_coeffs(xc, ga, gx, lam):
    r = _sigmoid(ga)
    ig = _sigmoid(gx)
    log_a = -LRU_C * r * _softplus(-lam)
    a = jnp.exp(log_a)
    om = 1.0 - a * a
    root = jnp.where(om > 0.0, om * lax.rsqrt(om), 0.0)
    return a, root * ig * xc


def _lru_gates(xcb, wax_ref, ba, bx):
    ga, gx = [], []
    for n in range(LRU_BLOCKS):
        g = _dot(xcb[:, n * LRU_BLK:(n + 1) * LRU_BLK], wax_ref[n])
        ga.append(g[:, :LRU_BLK])
        gx.append(g[:, LRU_BLK:])
    return jnp.concatenate(ga, axis=1) + ba, jnp.concatenate(gx, axis=1) + bx


def _scan_interleaved(a, u, h0):
    prods, sums = [], []
    p = s = None
    for i in range(SEG):
        ai, ui = _group(a, i), _group(u, i)
        p, s = (ai, ui) if i == 0 else (ai * p, ai * s + ui)
        prods.append(p)
        sums.append(s)
    c = h0
    starts = [c]
    for j in range(SUBLANES - 1):
        c = p[j:j + 1, :] * c + s[j:j + 1, :]
        starts.append(c)
    start = jnp.concatenate(starts, axis=0)
    return jnp.concatenate([sums[i] + prods[i] * start for i in range(SEG)], axis=0)


def _cumsum_interleaved(x):
    acc = []
    run = None
    for i in range(SEG):
        run = _group(x, i) if i == 0 else run + _group(x, i)
        acc.append(run)
    sub = lax.broadcasted_iota(jnp.int32, run.shape, 0)
    inc = run
    s = 1
    while s < SUBLANES:
        inc = inc + jnp.where(sub >= s, pltpu.roll(inc, s, 0), 0.0)
        s *= 2
    before = inc - run
    return jnp.concatenate([r + before for r in acc], axis=0)


def _conv_interleaved(carry, x_new, w, b):
    sub = lax.broadcasted_iota(jnp.int32, (SUBLANES, x_new.shape[1]), 0)
    head, new = [], []
    for r in range(CONV_W - 1):
        cur = _group(x_new, SEG - (CONV_W - 1) + r)
        head.append(jnp.where(sub == 0, carry[r:r + 1, :], pltpu.roll(cur, 1, 0)))
        new.append(cur[SUBLANES - 1:SUBLANES, :])
    ext = jnp.concatenate(head + [x_new], axis=0)
    out = b
    for j in range(CONV_W):
        out = out + ext[j * SUBLANES:j * SUBLANES + SEQ_TILE, :] * w[j:j + 1, :]
    return out, jnp.concatenate(new, axis=0)


def _time_of_row(r):
    return (r & (SUBLANES - 1)) * SEG + (r >> (SUBLANES.bit_length() - 1))


class _Plan:
    def __init__(self):
        self.tasks = {}

    def add(self, name, unit, cost, deps, fn):
        self.tasks[name] = (unit, cost, tuple(deps), fn)

    def order(self):
        succ = {n: [] for n in self.tasks}
        for n, (_, _, deps, _) in self.tasks.items():
            for p in deps:
                succ[p].append(n)
        tail = {}

        def path(n):
            if n not in tail:
                tail[n] = self.tasks[n][1] + max([path(s) for s in succ[n]], default=0)
            return tail[n]

        free = {"M": 0, "V": 0}
        done, order, left = {}, [], list(self.tasks)
        while left:
            ready = [n for n in left if all(p in done for p in self.tasks[n][2])]

            def start(n):
                unit, _, deps, _ = self.tasks[n]
                return max([free[unit]] + [done[p] for p in deps])

            n = min(ready, key=lambda n: (start(n), -path(n)))
            unit, cost, _, _ = self.tasks[n]
            st = start(n)
            done[n] = free[unit] = st + cost
            order.append((st, len(order), n))
            left.remove(n)
        return [n for _, _, n in sorted(order)]

    def run(self):
        for n in self.order():
            self.tasks[n][3]()


def _mixer_kernel(x_ref, perm_ref, permt_ref, gmix_ref, wmain_ref, wif_ref, bif_ref, lcw_ref,
                  lcb_ref, wax_ref, ba_ref, bx_ref, lam_ref, mcw_ref, mcb_ref, wq_ref, wk_ref,
                  wv_ref, mg_ref, wout_ref,
                  x1_ref, lconv_ref, h_ref, mconv_ref, c_ref, n_ref, m_ref):
    tt = SEQ_TILE
    last = tt - 1

    @pl.when(pl.program_id(1) == 0)
    def _():
        lconv_ref[...] = jnp.zeros(lconv_ref.shape, F32)
        mconv_ref[...] = jnp.zeros(mconv_ref.shape, F32)
        h_ref[...] = jnp.zeros(h_ref.shape, F32)
        c_ref[...] = jnp.zeros(c_ref.shape, F32)
        n_ref[...] = jnp.zeros(n_ref.shape, F32)
        m_ref[...] = jnp.full(m_ref.shape, M_INIT, F32)

    x = x_ref[0]
    lconv, mconv, h0 = lconv_ref[0], mconv_ref[0], h_ref[0]
    c_all = [c_ref[0, h] for h in range(M_HEADS)]
    n_all, m_all = n_ref[0], m_ref[0]
    lcw, lcb, mcw, mcb = lcw_ref[...], lcb_ref[...], mcw_ref[...], mcb_ref[...]
    ba, bx, lam, mg = ba_ref[...], bx_ref[...], lam_ref[...], mg_ref[...]

    v = {}
    plan = _Plan()

    def t_norm():
        v["xn_t"] = _rms(x, gmix_ref[...]).astype(BF)
    plan.add("norm", "V", 650, (), t_norm)

    def t_perm():
        v["xnb"] = _dot(perm_ref[...], v["xn_t"]).astype(BF)
    plan.add("perm", "M", 260, ("norm",), t_perm)

    def proj(name, col0, g):
        def run():
            v[name, g] = _dot(v["xnb"], wmain_ref[:, col0 + g * M_HD:col0 + (g + 1) * M_HD])
        plan.add((name, g), "M", 260, ("perm",), run)

    def t_wif():
        v["pre"] = _dot(v["xnb"], wif_ref[...]) + bif_ref[...]
    plan.add("wif", "M", 260, ("perm",), t_wif)

    def t_gcum():
        pre = v["pre"]
        lane = lax.broadcasted_iota(jnp.int32, (tt, GATE_PAD), 1)
        v["gcol"] = jnp.where(lane < M_HEADS, pre, _cumsum_interleaved(-_softplus(-pre)))
        v["grow"] = v["gcol"].T
        v["tri"] = (_time_of_row(lax.broadcasted_iota(jnp.int32, (tt, 1), 0))
                    >= _time_of_row(lax.broadcasted_iota(jnp.int32, (1, tt), 1)))
    plan.add("gcum", "V", 150, ("wif",), t_gcum)

    for g in range(M_HEADS):
        gs = slice(g * M_HD, (g + 1) * M_HD)
        for name, col0 in (("xl", 0), ("xm", D_MODEL), ("om", 2 * D_MODEL), ("gl", 3 * D_MODEL),
                           ("gm", 4 * D_MODEL)):
            proj(name, col0, g)

        def t_convl(g=g, gs=gs):
            v["xlc", g], v["lconv", g] = _conv_interleaved(lconv[:, gs], v["xl", g], lcw[:, gs],
                                                           lcb[:, gs])
        plan.add(("convl", g), "V", 135, (("xl", g),), t_convl)

        def t_gates(g=g, gs=gs):
            xlb = v["xlc", g].astype(BF)
            ga, gx = [], []
            for n in range(2):
                gg = _dot(xlb[:, n * LRU_BLK:(n + 1) * LRU_BLK], wax_ref[2 * g + n])
                ga.append(gg[:, :LRU_BLK])
                gx.append(gg[:, LRU_BLK:])
            v["ga", g] = jnp.concatenate(ga, axis=1) + ba[:, gs]
            v["gx", g] = jnp.concatenate(gx, axis=1) + bx[:, gs]
        plan.add(("gates", g), "M", 130, (("convl", g),), t_gates)

        def t_coef(g=g, gs=gs):
            v["a", g], v["u", g] = _lru_coeffs(v["xlc", g], v["ga", g], v["gx", g], lam[:, gs])
        plan.add(("coef", g), "V", 280, (("gates", g),), t_coef)

        def t_scan(g=g, gs=gs):
            v["yl", g] = _scan_interleaved(v["a", g], v["u", g], h0[:, gs])
        plan.add(("scan", g), "V", 120, (("coef", g),), t_scan)

        h = g

        def t_convm(h=h, gs=gs):
            xm_c, v["mconv", h] = _conv_interleaved(mconv[:, gs], v["xm", h], mcw[:, gs], mcb[:, gs])
            v["xcb", h] = (xm_c * _sigmoid(xm_c)).astype(BF)
        plan.add(("convm", h), "V", 210, (("xm", h),), t_convm)

        def t_qkv(h=h):
            v["q", h] = _dot(v["xcb", h], wq_ref[h]) * (M_HD ** -0.5)
            v["k", h] = _dot(v["xcb", h], wk_ref[h])
            vv = _dot(v["xm", h].astype(BF), wv_ref[h])
            v["qb", h], v["kb", h], v["vb", h] = (v["q", h].astype(BF), v["k", h].astype(BF),
                                                  vv.astype(BF))
        plan.add(("qkv", h), "M", 200, (("convm", h),), t_qkv)

        def t_qk(h=h):
            v["qk", h] = _dot_nt(v["qb", h], v["kb", h])
        plan.add(("qk", h), "M", 64, (("qkv", h),), t_qk)

        def t_sp(h=h):
            gcol, grow = v["gcol"], v["grow"]
            b_col = gcol[:, M_HEADS + h:M_HEADS + h + 1]
            ig_row, b_row = grow[h:h + 1, :], grow[M_HEADS + h:M_HEADS + h + 1, :]
            dlog = jnp.where(v["tri"], b_col - b_row + ig_row, MASKED)
            m_inter = b_col + m_all[:, h:h + 1]
            m_t = jnp.maximum(m_inter, jnp.max(dlog, axis=1, keepdims=True))
            s = v["qk", h] * jnp.exp(dlog - m_t)
            v["m_t", h], v["sc", h] = m_t, jnp.exp(m_inter - m_t)
            v["ssum", h] = jnp.sum(s, axis=1, keepdims=True)
            v["sb", h] = s.astype(BF)
        plan.add(("sp", h), "V", 200, (("qk", h), "gcum"), t_sp)

        def t_sv(h=h):
            v["sv", h] = _dot(v["sb", h], v["vb", h])
            v["qc", h] = _dot(v["qb", h], c_all[h].astype(BF))
        plan.add(("sv", h), "M", 130, (("sp", h),), t_sv)

        def t_hn(h=h, gs=gs):
            sc, m_t = v["sc", h], v["m_t", h]
            num = v["sv", h] + sc * v["qc", h]
            den = v["ssum", h] + sc * jnp.sum(v["q", h] * n_all[h:h + 1, :], axis=1, keepdims=True)
            hh = num / jnp.maximum(jnp.abs(den), jnp.exp(-m_t))
            v["hn", h] = hh * lax.rsqrt(jnp.mean(hh * hh, axis=-1, keepdims=True) + EPS) * mg[:, gs]
        plan.add(("hn", h), "V", 170, (("sv", h),), t_hn)

        def t_kw(h=h):
            gcol = v["gcol"]
            ig_col, b_col = gcol[:, h:h + 1], gcol[:, M_HEADS + h:M_HEADS + h + 1]
            m_last = v["m_t", h][last:last + 1, :]
            b_last = b_col[last:last + 1, :]
            kw = v["k", h] * jnp.exp(b_last - b_col + ig_col - m_last)
            v["dec", h] = jnp.exp(b_last + m_all[:, h:h + 1] - m_last)
            v["m_new", h] = m_last
            v["n_new", h] = v["dec", h] * n_all[h:h + 1, :] + jnp.sum(kw, axis=0, keepdims=True)
            v["kwb", h] = kw.astype(BF)
        plan.add(("kw", h), "V", 80, (("sp", h),), t_kw)

        def t_ckv(h=h):
            v["ckv", h] = _dot_tn(v["kwb", h], v["vb", h])
        plan.add(("ckv", h), "M", 64, (("kw", h),), t_ckv)

        def t_cnew(h=h):
            v["c_new", h] = v["dec", h] * c_all[h] + v["ckv", h]
        plan.add(("cnew", h), "V", 40, (("ckv", h),), t_cnew)

        def t_sig(h=h):
            v["gate_l", h] = _sigmoid(v["gl", h]) * v["yl", h]
            v["gate_m", h] = _sigmoid(v["gm", h]) * _sigmoid(v["om", h])
        plan.add(("sig", h), "V", 180, (("gl", h), ("gm", h), ("om", h), ("scan", h)), t_sig)

        def t_mrg(h=h):
            v["mrg", h] = (v["gate_l", h] + v["gate_m", h] * v["hn", h]).astype(BF)
        plan.add(("mrg", h), "V", 40, (("sig", h), ("hn", h)), t_mrg)

        def t_pt(h=h):
            v["mrg_t", h] = _dot(permt_ref[...], v["mrg", h]).astype(BF)
        plan.add(("pt", h), "M", 64, (("mrg", h),), t_pt)

        def t_wo(h=h, gs=gs):
            part = _dot(v["mrg_t", h], wout_ref[gs, :])
            v["out"] = part if "out" not in v else v["out"] + part
        plan.add(("wo", h), "M", 260, (("pt", h),) + ((("wo", h - 1),) if h else ()), t_wo)

    def t_fin():
        x1_ref[0] = x + v["out"]
        lconv_ref[0] = jnp.concatenate([v["lconv", g] for g in range(M_HEADS)], axis=1)
        mconv_ref[0] = jnp.concatenate([v["mconv", g] for g in range(M_HEADS)], axis=1)
        h_ref[0] = jnp.concatenate([v["yl", g][last:last + 1, :] for g in range(M_HEADS)], axis=1)
        for h in range(M_HEADS):
            c_ref[0, h] = v["c_new", h]
        n_ref[0] = jnp.concatenate([v["n_new", h] for h in range(M_HEADS)], axis=0)
        m_ref[0] = jnp.concatenate([v["m_new", h] for h in range(M_HEADS)], axis=1)
    plan.add("fin", "V", 100, tuple(("wo", h) for h in range(M_HEADS))
             + tuple(("cnew", h) for h in range(M_HEADS)), t_fin)

    plan.run()


def _ffn_tail(x1, p, gffn_ref, wg_ref, wu_ref, wd_ref, gple_ref, wpg_ref, wple_ref, gfin_ref):
    xnb = _rms(x1, gffn_ref[...]).astype(BF)
    x2 = x1
    for c in range(D_FF // FF_CHUNK):
        cs = slice(c * FF_CHUNK, (c + 1) * FF_CHUNK)
        hg = _dot(xnb, wg_ref[:, cs])
        hu = _dot(xnb, wu_ref[:, cs])
        act = (hg * _sigmoid(hg) * hu).astype(BF)
        x2 = x2 + _dot(act, wd_ref[cs, :])
    gate = _sigmoid(_dot(_rms(x2, gple_ref[...]).astype(BF), wpg_ref[...]))
    x3 = x2 + gate * _dot(p.astype(BF), wple_ref[...])
    return _rms(x3, gfin_ref[...])


def _ffn_kernel(x1_ref, p_ref, gffn_ref, wg_ref, wu_ref, wd_ref, gple_ref, wpg_ref, wple_ref,
                gfin_ref, y_ref):
    y_ref[...] = _ffn_tail(x1_ref[...], p_ref[...], gffn_ref, wg_ref, wu_ref, wd_ref, gple_ref,
                           wpg_ref, wple_ref, gfin_ref)


def _front_kernel(x_ref, lbuf_ref, h0_ref, mbuf_ref, gmix_ref, wmain_ref, wif_ref, bif_ref,
                  lcw_ref, lcb_ref, wax_ref, ba_ref, bx_ref, lam_ref, mcw_ref, mcb_ref,
                  wq_ref, wk_ref, wv_ref, m0_ref, n0_ref,
                  q_ref, kw_ref, v_ref, sc_ref, hv_ref, hc_ref, n_out, m_out,
                  yl_ref, gm_ref, lbuf_out, h_out, mbuf_out):
    xnb = _rms(x_ref[...], gmix_ref[...]).astype(BF)

    def per_head(cols):
        return jnp.concatenate([jnp.broadcast_to(cols[:, h:h + 1], (cols.shape[0], M_HD))
                                for h in range(M_HEADS)], axis=1)

    def head_sums(a):
        return jnp.concatenate([jnp.sum(a[:, h * M_HD:(h + 1) * M_HD], axis=1, keepdims=True)
                                for h in range(M_HEADS)], axis=1)

    def conv_step(buf_ref, buf_out, x_new, w_ref, b_ref):
        out = b_ref[...] + x_new * w_ref[CONV_W - 1:CONV_W, :]
        for j in range(CONV_W - 1):
            out = out + buf_ref[j] * w_ref[j:j + 1, :]
        for j in range(CONV_W - 2):
            buf_out[j] = buf_ref[j + 1]
        buf_out[CONV_W - 2] = x_new
        return out

    x_l = _dot(xnb, wmain_ref[:, 0:D_MODEL])
    xl_c = conv_step(lbuf_ref, lbuf_out, x_l, lcw_ref, lcb_ref)
    ga, gx = _lru_gates(xl_c.astype(BF), wax_ref, ba_ref[...], bx_ref[...])
    a, u = _lru_coeffs(xl_c, ga, gx, lam_ref[...])
    y_l = a * h0_ref[...] + u
    h_out[...] = y_l
    g_l = _dot(xnb, wmain_ref[:, 3 * D_MODEL:4 * D_MODEL])
    yl_ref[...] = _sigmoid(g_l) * y_l

    x_m = _dot(xnb, wmain_ref[:, D_MODEL:2 * D_MODEL])
    xm_c = conv_step(mbuf_ref, mbuf_out, x_m, mcw_ref, mcb_ref)
    xcb = (xm_c * _sigmoid(xm_c)).astype(BF)
    xmb = x_m.astype(BF)
    q, k, v = [], [], []
    for h in range(M_HEADS):
        hs = slice(h * M_HD, (h + 1) * M_HD)
        q.append(_dot(xcb[:, hs], wq_ref[h]) * (M_HD ** -0.5))
        k.append(_dot(xcb[:, hs], wk_ref[h]))
        v.append(_dot(xmb[:, hs], wv_ref[h]))
    q, k, v = (jnp.concatenate(a, axis=1) for a in (q, k, v))
    o_m = _dot(xnb, wmain_ref[:, 2 * D_MODEL:3 * D_MODEL])
    g_m = _dot(xnb, wmain_ref[:, 4 * D_MODEL:5 * D_MODEL])
    gm_ref[...] = _sigmoid(g_m) * _sigmoid(o_m)

    pre = _dot(xnb, wif_ref[...]) + bif_ref[...]
    ig = pre[:, 0:M_HEADS]
    lf = -_softplus(-pre[:, M_HEADS:N_GATES])
    n_prev = n0_ref[...]
    m_inter = lf + m0_ref[...]
    m_t = jnp.maximum(m_inter, ig)
    wk = jnp.exp(ig - m_t)
    sc = jnp.exp(m_inter - m_t)
    s = head_sums(q * k) * wk
    den = s + sc * head_sums(q * n_prev)
    rden = 1.0 / jnp.maximum(jnp.abs(den), jnp.exp(-m_t))
    wk_d, sc_d = per_head(wk), per_head(sc)
    hv_ref[...] = per_head(s * rden) * v
    hc_ref[...] = per_head(sc * rden)
    q_ref[...] = q
    kw_ref[...] = wk_d * k
    v_ref[...] = v
    sc_ref[...] = sc
    n_out[...] = sc_d * n_prev + wk_d * k
    m_out[...] = m_t


def _state_kernel(sc_ref, q_ref, kw_ref, v_ref, c_ref, qc_out, c_out):
    base = pl.program_id(0) * STATE_BATCH
    cols = []
    for bb in range(STATE_BATCH):
        b = base + bb
        cols.append(jnp.concatenate([q_ref[b], kw_ref[b]], axis=0).T)
    for bb in range(STATE_BATCH):
        b = base + bb
        v = v_ref[b]
        for h in range(M_HEADS):
            q_col, kw_col = cols[bb][:, h:h + 1], cols[bb][:, M_HEADS + h:M_HEADS + h + 1]
            c_prev = c_ref[bb, h]
            qc_out[b, h:h + 1] = jnp.sum(q_col * c_prev, axis=0, keepdims=True)
            c_out[bb, h] = sc_ref[b, h] * c_prev + kw_col * v[h:h + 1]


def _back_kernel(x_ref, qc_ref, hv_ref, hc_ref, yl_ref, gm_ref, p_ref, mg_ref, wout_ref, gffn_ref,
                 wg_ref, wu_ref, wd_ref, gple_ref, wpg_ref, wple_ref, gfin_ref, y_ref, mrg_ref):
    for h in range(M_HEADS):
        hs = slice(h * M_HD, (h + 1) * M_HD)
        hh = hv_ref[:, hs] + hc_ref[:, hs] * qc_ref[:, hs]
        hn = hh * lax.rsqrt(jnp.mean(hh * hh, axis=-1, keepdims=True) + EPS) * mg_ref[:, hs]
        mrg_ref[:, hs] = (yl_ref[:, hs] + gm_ref[:, hs] * hn).astype(BF)
    x1 = x_ref[...] + _dot(mrg_ref[...], wout_ref[...])
    y_ref[...] = _ffn_tail(x1, p_ref[...], gffn_ref, wg_ref, wu_ref, wd_ref, gple_ref, wpg_ref,
                           wple_ref, gfin_ref)


def _resident(shape):
    nd = len(shape)
    return pl.BlockSpec(shape, lambda *_: (0,) * nd, pipeline_mode=pl.Buffered(1))


def _params(n_axes):
    return pltpu.CompilerParams(dimension_semantics=("arbitrary",) * n_axes,
                                vmem_limit_bytes=VMEM_LIMIT)


def _interleave_matrix():
    r = np.arange(SEQ_TILE)
    perm = np.zeros((SEQ_TILE, SEQ_TILE), np.float32)
    perm[r, (r % SUBLANES) * SEG + r // SUBLANES] = 1.0
    return perm


def kernel(x_prompt, x_sample, state_lru_conv, state_lru_h, state_mlstm_conv, state_mlstm_C, state_mlstm_n, state_mlstm_m, p_prompt, p_sample, norm_mix_g, w_in, b_gates, lru_conv_w, lru_conv_b, lru_w_a, lru_b_a, lru_w_x, lru_b_x, lru_lambda, mlstm_conv_w, mlstm_conv_b, w_q, w_k, w_v, mlstm_norm_g, w_out, norm_ffn_g, w_ffn_gate, w_ffn_up, w_ffn_down, norm_ple_g, w_ple_gate, w_ple, final_norm_g):
    assert w_in.shape[0] == 1, "single-layer trunk"
    B, T, _ = x_prompt.shape
    S = x_sample.shape[0]
    assert T % SEQ_TILE == 0 and (B * T) % FFN_TILE == 0 and x_sample.shape[1] == 1
    assert S % STATE_BATCH == 0

    g0 = 2 * D_MODEL + D_MODEL
    w0 = w_in[0]
    wmain = jnp.concatenate([w0[:, :g0], w0[:, g0 + N_GATES:]], axis=1).astype(BF)
    wif = jnp.pad(w0[:, g0:g0 + N_GATES], ((0, 0), (0, GATE_PAD - N_GATES))).astype(BF)
    bif = jnp.pad(b_gates[0], (0, GATE_PAD - N_GATES)).reshape(1, GATE_PAD)
    wax = jnp.concatenate([lru_w_a[0], lru_w_x[0]], axis=2).astype(BF)
    row = lambda a: a.reshape(1, -1)
    gmix, ba, bx, lam = row(norm_mix_g[0]), row(lru_b_a[0]), row(lru_b_x[0]), row(lru_lambda[0])
    lcw, lcb = lru_conv_w[0], row(lru_conv_b[0])
    mcw, mcb = mlstm_conv_w[0], row(mlstm_conv_b[0])
    wq, wk, wv = w_q[0].astype(BF), w_k[0].astype(BF), w_v[0].astype(BF)
    mg = row(mlstm_norm_g[0])
    wout = w_out[0].astype(BF)
    gffn, gple, gfin = row(norm_ffn_g[0]), row(norm_ple_g[0]), row(final_norm_g)
    wg, wu, wd = w_ffn_gate[0].astype(BF), w_ffn_up[0].astype(BF), w_ffn_down[0].astype(BF)
    wpg, wple = w_ple_gate[0].astype(BF), w_ple[0].astype(BF)
    perm_np = _interleave_matrix()
    perm, perm_t = jnp.asarray(perm_np, BF), jnp.asarray(perm_np.T, BF)

    front_w = (gmix, wmain, wif, bif, lcw, lcb, wax, ba, bx, lam, mcw, mcb, wq, wk, wv)
    mixer_w = (perm, perm_t) + front_w + (mg, wout)
    ffn_w = (gffn, wg, wu, wd, gple, wpg, wple, gfin)

    nt = T // SEQ_TILE
    sds = jax.ShapeDtypeStruct
    x1, p_lconv, p_h, p_mconv, p_c, p_n, p_m = pl.pallas_call(
        _mixer_kernel,
        grid=(B, nt),
        in_specs=[pl.BlockSpec((1, SEQ_TILE, D_MODEL), lambda b, t: (b, t, 0))]
                 + [_resident(w.shape) for w in mixer_w],
        out_specs=[
            pl.BlockSpec((1, SEQ_TILE, D_MODEL), lambda b, t: (b, t, 0)),
            pl.BlockSpec((1, CONV_W - 1, D_MODEL), lambda b, t: (b, 0, 0)),
            pl.BlockSpec((1, 1, D_MODEL), lambda b, t: (b, 0, 0)),
            pl.BlockSpec((1, CONV_W - 1, D_MODEL), lambda b, t: (b, 0, 0)),
            pl.BlockSpec((1, M_HEADS, M_HD, M_HD), lambda b, t: (b, 0, 0, 0)),
            pl.BlockSpec((1, M_HEADS, M_HD), lambda b, t: (b, 0, 0)),
            pl.BlockSpec((1, 1, M_HEADS), lambda b, t: (b, 0, 0)),
        ],
        out_shape=[
            sds((B, T, D_MODEL), F32),
            sds((B, CONV_W - 1, D_MODEL), F32),
            sds((B, 1, D_MODEL), F32),
            sds((B, CONV_W - 1, D_MODEL), F32),
            sds((B, M_HEADS, M_HD, M_HD), F32),
            sds((B, M_HEADS, M_HD), F32),
            sds((B, 1, M_HEADS), F32),
        ],
        compiler_params=_params(2),
        name="prompt_mixer",
    )(x_prompt, *mixer_w)

    n_tok = B * T
    y_prompt = pl.pallas_call(
        _ffn_kernel,
        grid=(n_tok // FFN_TILE,),
        in_specs=[pl.BlockSpec((FFN_TILE, D_MODEL), lambda i: (i, 0)),
                  pl.BlockSpec((FFN_TILE, P_DIM), lambda i: (i, 0))]
                 + [_resident(w.shape) for w in ffn_w],
        out_specs=pl.BlockSpec((FFN_TILE, D_MODEL), lambda i: (i, 0)),
        out_shape=sds((n_tok, D_MODEL), F32),
        compiler_params=_params(1),
        name="prompt_ffn",
    )(x1.reshape(n_tok, D_MODEL), p_prompt[0].reshape(n_tok, P_DIM), *ffn_w)

    xs = x_sample.reshape(S, D_MODEL)
    lbuf = jnp.swapaxes(state_lru_conv[0], 0, 1)
    mbuf = jnp.swapaxes(state_mlstm_conv[0], 0, 1)
    tok = sds((S, D_MODEL), F32)
    buf = sds((CONV_W - 1, S, D_MODEL), F32)
    per_head = sds((S, M_HEADS), F32)
    q, kw, v, sc, hv, hc, s_n, s_m, yl, gm, s_lbuf, s_h, s_mbuf = pl.pallas_call(
        _front_kernel,
        out_shape=[tok, tok, tok, per_head, tok, tok, tok, per_head, tok, tok, buf, tok, buf],
        compiler_params=pltpu.CompilerParams(vmem_limit_bytes=VMEM_LIMIT),
        name="sample_front",
    )(xs, lbuf, state_lru_h[0], mbuf, *front_w, state_mlstm_m[0],
      state_mlstm_n[0].reshape(S, D_MODEL))

    heads = lambda a: a.reshape(S, M_HEADS, M_HD)
    rows = (S, M_HEADS, M_HD)
    whole = pl.BlockSpec(rows, lambda b: (0, 0, 0))
    c_spec = pl.BlockSpec((STATE_BATCH, M_HEADS, M_HD, M_HD), lambda b: (b, 0, 0, 0))
    qc, s_c = pl.pallas_call(
        _state_kernel,
        grid=(S // STATE_BATCH,),
        in_specs=[pl.BlockSpec(memory_space=pltpu.SMEM), whole, whole, whole, c_spec],
        out_specs=[whole, c_spec],
        out_shape=[sds(rows, F32), sds((S, M_HEADS, M_HD, M_HD), F32)],
        compiler_params=_params(1),
        name="sample_state",
    )(sc, heads(q), heads(kw), heads(v), state_mlstm_C[0])

    y_sample = pl.pallas_call(
        _back_kernel,
        out_shape=tok,
        scratch_shapes=[pltpu.VMEM((S, D_MODEL), BF)],
        compiler_params=pltpu.CompilerParams(vmem_limit_bytes=VMEM_LIMIT),
        name="sample_back",
    )(xs, qc.reshape(S, D_MODEL), hv, hc, yl, gm, p_sample[0].reshape(S, P_DIM), mg, wout, *ffn_w)

    lead = lambda a: a[None]
    return (y_prompt.reshape(B, T, D_MODEL), y_sample.reshape(S, 1, D_MODEL),
            lead(p_lconv), p_h.reshape(1, B, D_MODEL), lead(p_mconv), lead(p_c), lead(p_n),
            p_m.reshape(1, B, M_HEADS),
            lead(jnp.swapaxes(s_lbuf, 0, 1)), lead(s_h), lead(jnp.swapaxes(s_mbuf, 0, 1)),
            lead(s_c), lead(heads(s_n)), lead(s_m))
```

```python
import numpy as np

import jax
import jax.numpy as jnp
from jax import lax
from jax.experimental import pallas as pl
from jax.experimental.pallas import tpu as pltpu

D_MODEL = 1024
M_HEADS = 4
M_HD = D_MODEL // M_HEADS
LRU_BLOCKS = 8
LRU_BLK = D_MODEL // LRU_BLOCKS
LRU_C = 8.0
CONV_W = 4
D_FF = 2816
P_DIM = 256
EPS = 1e-6
M_INIT = -1e30
MASKED = -1e30
N_GATES = 2 * M_HEADS
GATE_PAD = 128
SUBLANES = 8
SEQ_TILE = 256
SEG = SEQ_TILE // SUBLANES
FFN_TILE = 512
FF_CHUNK = 256
STATE_BATCH = 8
VMEM_LIMIT = 56 * 1024 * 1024

BF = jnp.bfloat16
F32 = jnp.float32


def _dot(a, b):
    return jnp.dot(a, b, preferred_element_type=F32)


def _dot_nt(a, b):
    return lax.dot_general(a, b, (((1,), (1,)), ((), ())), preferred_element_type=F32)


def _dot_tn(a, b):
    return lax.dot_general(a, b, (((0,), (0,)), ((), ())), preferred_element_type=F32)


def _sigmoid(x):
    return 1.0 / (1.0 + jnp.exp(-x))


def _softplus(x):
    return jnp.maximum(x, 0.0) + jnp.log1p(jnp.exp(-jnp.abs(x)))


def _rms(x, g):
    return x * lax.rsqrt(jnp.mean(x * x, axis=-1, keepdims=True) + EPS) * g


def _group(x, i):
    return x[i * SUBLANES:(i + 1) * SUBLANES, :]


def _lru_coeffs(xc, ga, gx, lam):
    r = _sigmoid(ga)
    ig = _sigmoid(gx)
    log_a = -LRU_C * r * _softplus(-lam)
    a = jnp.exp(log_a)
    om = 1.0 - a * a
    root = jnp.where(om > 0.0, om * lax.rsqrt(om), 0.0)
    return a, root * ig * xc


def _lru_gates(xcb, wax_ref, ba, bx):
    ga, gx = [], []
    for n in range(LRU_BLOCKS):
        g = _dot(xcb[:, n * LRU_BLK:(n + 1) * LRU_BLK], wax_ref[n])
        ga.append(g[:, :LRU_BLK])
        gx.append(g[:, LRU_BLK:])
    return jnp.concatenate(ga, axis=1) + ba, jnp.concatenate(gx, axis=1) + bx


def _scan_interleaved(a, u, h0):
    prods, sums = [], []
    p = s = None
    for i in range(SEG):
        ai, ui = _group(a, i), _group(u, i)
        p, s = (ai, ui) if i == 0 else (ai * p, ai * s + ui)
        prods.append(p)
        sums.append(s)
    c = h0
    starts = [c]
    for j in range(SUBLANES - 1):
        c = p[j:j + 1, :] * c + s[j:j + 1, :]
        starts.append(c)
    start = jnp.concatenate(starts, axis=0)
    return jnp.concatenate([sums[i] + prods[i] * start for i in range(SEG)], axis=0)


def _cumsum_interleaved(x):
    acc = []
    run = None
    for i in range(SEG):
        run = _group(x, i) if i == 0 else run + _group(x, i)
        acc.append(run)
    sub = lax.broadcasted_iota(jnp.int32, run.shape, 0)
    inc = run
    s = 1
    while s < SUBLANES:
        inc = inc + jnp.where(sub >= s, pltpu.roll(inc, s, 0), 0.0)
        s *= 2
    before = inc - run
    return jnp.concatenate([r + before for r in acc], axis=0)


def _conv_interleaved(carry, x_new, w, b):
    sub = lax.broadcasted_iota(jnp.int32, (SUBLANES, x_new.shape[1]), 0)
    head, new = [], []
    for r in range(CONV_W - 1):
        cur = _group(x_new, SEG - (CONV_W - 1) + r)
        head.append(jnp.where(sub == 0, carry[r:r + 1, :], pltpu.roll(cur, 1, 0)))
        new.append(cur[SUBLANES - 1:SUBLANES, :])
    ext = jnp.concatenate(head + [x_new], axis=0)
    out = b
    for j in range(CONV_W):
        out = out + ext[j * SUBLANES:j * SUBLANES + SEQ_TILE, :] * w[j:j + 1, :]
    return out, jnp.concatenate(new, axis=0)


def _time_of_row(r):
    return (r & (SUBLANES - 1)) * SEG + (r >> (SUBLANES.bit_length() - 1))


class _Plan:
    def __init__(self):
        self.tasks = {}

    def add(self, name, unit, cost, deps, fn):
        self.tasks[name] = (unit, cost, tuple(deps), fn)

    def order(self):
        succ = {n: [] for n in self.tasks}
        for n, (_, _, deps, _) in self.tasks.items():
            for p in deps:
                succ[p].append(n)
        tail = {}

        def path(n):
            if n not in tail:
                tail[n] = self.tasks[n][1] + max([path(s) for s in succ[n]], default=0)
            return tail[n]

        free = {"M": 0, "V": 0}
        done, order, left = {}, [], list(self.tasks)
        while left:
            ready = [n for n in left if all(p in done for p in self.tasks[n][2])]

            def start(n):
                unit, _, deps, _ = self.tasks[n]
                return max([free[unit]] + [done[p] for p in deps])

            n = min(ready, key=lambda n: (start(n), -path(n)))
            unit, cost, _, _ = self.tasks[n]
            st = start(n)
            done[n] = free[unit] = st + cost
            order.append((st, len(order), n))
            left.remove(n)
        return [n for _, _, n in sorted(order)]

    def run(self):
        for n in self.order():
            self.tasks[n][3]()


def _mixer_kernel(x_ref, perm_ref, permt_ref, gmix_ref, wbr_ref, wmg_ref, wif_ref, bif_ref, lcw_ref,
                  lcb_ref, wax_ref, ba_ref, bx_ref, lam_ref, mcw_ref, mcb_ref, wq_ref, wk_ref,
                  wv_ref, mg_ref, wout_ref,
                  x1_ref, lconv_ref, h_ref, mconv_ref, c_ref, n_ref, m_ref):
    tt = SEQ_TILE
    last = tt - 1

    @pl.when(pl.program_id(1) == 0)
    def _():
        lconv_ref[...] = jnp.zeros(lconv_ref.shape, F32)
        mconv_ref[...] = jnp.zeros(mconv_ref.shape, F32)
        h_ref[...] = jnp.zeros(h_ref.shape, F32)
        c_ref[...] = jnp.zeros(c_ref.shape, F32)
        n_ref[...] = jnp.zeros(n_ref.shape, F32)
        m_ref[...] = jnp.full(m_ref.shape, M_INIT, F32)

    x = x_ref[0]
    lconv, mconv, h0 = lconv_ref[0], mconv_ref[0], h_ref[0]
    c_all = [c_ref[0, h] for h in range(M_HEADS)]
    n_all, m_all = n_ref[0], m_ref[0]
    lcw, lcb, mcw, mcb = lcw_ref[...], lcb_ref[...], mcw_ref[...], mcb_ref[...]
    ba, bx, lam, mg = ba_ref[...], bx_ref[...], lam_ref[...], mg_ref[...]

    v = {}
    plan = _Plan()

    def t_norm():
        v["xn_t"] = _rms(x, gmix_ref[...]).astype(BF)
    plan.add("norm", "V", 650, (), t_norm)

    def t_perm():
        v["xnb"] = _dot(perm_ref[...], v["xn_t"]).astype(BF)
    plan.add("perm", "M", 260, ("norm",), t_perm)

    def proj(name, w_ref, col0, g):
        def run():
            v[name, g] = _dot(v["xnb"], w_ref[:, col0 + g * M_HD:col0 + (g + 1) * M_HD])
        plan.add((name, g), "M", 260, ("perm",), run)

    def t_wif():
        v["pre"] = _dot(v["xnb"], wif_ref[...]) + bif_ref[...]
    plan.add("wif", "M", 260, ("perm",), t_wif)

    def t_gcum():
        pre = v["pre"]
        lane = lax.broadcasted_iota(jnp.int32, (tt, GATE_PAD), 1)
        v["gcol"] = jnp.where(lane < M_HEADS, pre, _cumsum_interleaved(-_softplus(-pre)))
        v["grow"] = v["gcol"].T
        v["tri"] = (_time_of_row(lax.broadcasted_iota(jnp.int32, (tt, 1), 0))
                    >= _time_of_row(lax.broadcasted_iota(jnp.int32, (1, tt), 1)))
    plan.add("gcum", "V", 150, ("wif",), t_gcum)

    for g in range(M_HEADS):
        gs = slice(g * M_HD, (g + 1) * M_HD)
        for name, w_ref, col0 in (("xl", wbr_ref, 0), ("xm", wbr_ref, D_MODEL),
                                  ("om", wbr_ref, 2 * D_MODEL), ("gl", wmg_ref, 0),
                                  ("gm", wmg_ref, D_MODEL)):
            proj(name, w_ref, col0, g)

        def t_convl(g=g, gs=gs):
            v["xlc", g], v["lconv", g] = _conv_interleaved(lconv[:, gs], v["xl", g], lcw[:, gs],
                                                           lcb[:, gs])
        plan.add(("convl", g), "V", 135, (("xl", g),), t_convl)

        def t_gates(g=g, gs=gs):
            xlb = v["xlc", g].astype(BF)
            ga, gx = [], []
            for n in range(2):
                gg = _dot(xlb[:, n * LRU_BLK:(n + 1) * LRU_BLK], wax_ref[2 * g + n])
                ga.append(gg[:, :LRU_BLK])
                gx.append(gg[:, LRU_BLK:])
            v["ga", g] = jnp.concatenate(ga, axis=1) + ba[:, gs]
            v["gx", g] = jnp.concatenate(gx, axis=1) + bx[:, gs]
        plan.add(("gates", g), "M", 130, (("convl", g),), t_gates)

        def t_coef(g=g, gs=gs):
            v["a", g], v["u", g] = _lru_coeffs(v["xlc", g], v["ga", g], v["gx", g], lam[:, gs])
        plan.add(("coef", g), "V", 280, (("gates", g),), t_coef)

        def t_scan(g=g, gs=gs):
            v["yl", g] = _scan_interleaved(v["a", g], v["u", g], h0[:, gs])
        plan.add(("scan", g), "V", 120, (("coef", g),), t_scan)

        h = g

        def t_convm(h=h, gs=gs):
            xm_c, v["mconv", h] = _conv_interleaved(mconv[:, gs], v["xm", h], mcw[:, gs], mcb[:, gs])
            v["xcb", h] = (xm_c * _sigmoid(xm_c)).astype(BF)
        plan.add(("convm", h), "V", 210, (("xm", h),), t_convm)

        def t_qkv(h=h):
            v["q", h] = _dot(v["xcb", h], wq_ref[h]) * (M_HD ** -0.5)
            v["k", h] = _dot(v["xcb", h], wk_ref[h])
            vv = _dot(v["xm", h].astype(BF), wv_ref[h])
            v["qb", h], v["kb", h], v["vb", h] = (v["q", h].astype(BF), v["k", h].astype(BF),
                                                  vv.astype(BF))
        plan.add(("qkv", h), "M", 200, (("convm", h),), t_qkv)

        def t_qk(h=h):
            v["qk", h] = _dot_nt(v["qb", h], v["kb", h])
        plan.add(("qk", h), "M", 64, (("qkv", h),), t_qk)

        def t_sp(h=h):
            gcol, grow = v["gcol"], v["grow"]
            b_col = gcol[:, M_HEADS + h:M_HEADS + h + 1]
            ig_row, b_row = grow[h:h + 1, :], grow[M_HEADS + h:M_HEADS + h + 1, :]
            dlog = jnp.where(v["tri"], b_col - b_row + ig_row, MASKED)
            m_inter = b_col + m_all[:, h:h + 1]
            m_t = jnp.maximum(m_inter, jnp.max(dlog, axis=1, keepdims=True))
            s = v["qk", h] * jnp.exp(dlog - m_t)
            v["m_t", h], v["sc", h] = m_t, jnp.exp(m_inter - m_t)
            v["ssum", h] = jnp.sum(s, axis=1, keepdims=True)
            v["sb", h] = s.astype(BF)
        plan.add(("sp", h), "V", 200, (("qk", h), "gcum"), t_sp)

        def t_sv(h=h):
            v["sv", h] = _dot(v["sb", h], v["vb", h])
            v["qc", h] = _dot(v["qb", h], c_all[h].astype(BF))
        plan.add(("sv", h), "M", 130, (("sp", h),), t_sv)

        def t_hn(h=h, gs=gs):
            sc, m_t = v["sc", h], v["m_t", h]
            num = v["sv", h] + sc * v["qc", h]
            den = v["ssum", h] + sc * jnp.sum(v["q", h] * n_all[h:h + 1, :], axis=1, keepdims=True)
            hh = num / jnp.maximum(jnp.abs(den), jnp.exp(-m_t))
            v["hn", h] = hh * lax.rsqrt(jnp.mean(hh * hh, axis=-1, keepdims=True) + EPS) * mg[:, gs]
        plan.add(("hn", h), "V", 170, (("sv", h),), t_hn)

        def t_kw(h=h):
            gcol = v["gcol"]
            ig_col, b_col = gcol[:, h:h + 1], gcol[:, M_HEADS + h:M_HEADS + h + 1]
            m_last = v["m_t", h][last:last + 1, :]
            b_last = b_col[last:last + 1, :]
            kw = v["k", h] * jnp.exp(b_last - b_col + ig_col - m_last)
            v["dec", h] = jnp.exp(b_last + m_all[:, h:h + 1] - m_last)
            v["m_new", h] = m_last
            v["n_new", h] = v["dec", h] * n_all[h:h + 1, :] + jnp.sum(kw, axis=0, keepdims=True)
            v["kwb", h] = kw.astype(BF)
        plan.add(("kw", h), "V", 80, (("sp", h),), t_kw)

        def t_ckv(h=h):
            v["ckv", h] = _dot_tn(v["kwb", h], v["vb", h])
        plan.add(("ckv", h), "M", 64, (("kw", h),), t_ckv)

        def t_cnew(h=h):
            v["c_new", h] = v["dec", h] * c_all[h] + v["ckv", h]
        plan.add(("cnew", h), "V", 40, (("ckv", h),), t_cnew)

        def t_sig(h=h):
            v["gate_l", h] = _sigmoid(v["gl", h]) * v["yl", h]
            v["gate_m", h] = _sigmoid(v["gm", h]) * _sigmoid(v["om", h])
        plan.add(("sig", h), "V", 180, (("gl", h), ("gm", h), ("om", h), ("scan", h)), t_sig)

        def t_mrg(h=h):
            v["mrg", h] = (v["gate_l", h] + v["gate_m", h] * v["hn", h]).astype(BF)
        plan.add(("mrg", h), "V", 40, (("sig", h), ("hn", h)), t_mrg)

        def t_pt(h=h):
            v["mrg_t", h] = _dot(permt_ref[...], v["mrg", h]).astype(BF)
        plan.add(("pt", h), "M", 64, (("mrg", h),), t_pt)

        def t_wo(h=h, gs=gs):
            part = _dot(v["mrg_t", h], wout_ref[gs, :])
            v["out"] = part if "out" not in v else v["out"] + part
        plan.add(("wo", h), "M", 260, (("pt", h),) + ((("wo", h - 1),) if h else ()), t_wo)

    def t_fin():
        x1_ref[0] = x + v["out"]
        lconv_ref[0] = jnp.concatenate([v["lconv", g] for g in range(M_HEADS)], axis=1)
        mconv_ref[0] = jnp.concatenate([v["mconv", g] for g in range(M_HEADS)], axis=1)
        h_ref[0] = jnp.concatenate([v["yl", g][last:last + 1, :] for g in range(M_HEADS)], axis=1)
        for h in range(M_HEADS):
            c_ref[0, h] = v["c_new", h]
        n_ref[0] = jnp.concatenate([v["n_new", h] for h in range(M_HEADS)], axis=0)
        m_ref[0] = jnp.concatenate([v["m_new", h] for h in range(M_HEADS)], axis=1)
    plan.add("fin", "V", 100, tuple(("wo", h) for h in range(M_HEADS))
             + tuple(("cnew", h) for h in range(M_HEADS)), t_fin)

    plan.run()


def _ffn_tail(x1, p, gffn_ref, wg_ref, wu_ref, wd_ref, gple_ref, wpg_ref, wple_ref, gfin_ref):
    xnb = _rms(x1, gffn_ref[...]).astype(BF)
    x2 = x1
    for c in range(D_FF // FF_CHUNK):
        cs = slice(c * FF_CHUNK, (c + 1) * FF_CHUNK)
        hg = _dot(xnb, wg_ref[:, cs])
        hu = _dot(xnb, wu_ref[:, cs])
        act = (hg * _sigmoid(hg) * hu).astype(BF)
        x2 = x2 + _dot(act, wd_ref[cs, :])
    gate = _sigmoid(_dot(_rms(x2, gple_ref[...]).astype(BF), wpg_ref[...]))
    x3 = x2 + gate * _dot(p.astype(BF), wple_ref[...])
    return _rms(x3, gfin_ref[...])


def _ffn_kernel(x1_ref, p_ref, gffn_ref, wg_ref, wu_ref, wd_ref, gple_ref, wpg_ref, wple_ref,
                gfin_ref, y_ref):
    y_ref[...] = _ffn_tail(x1_ref[...], p_ref[...], gffn_ref, wg_ref, wu_ref, wd_ref, gple_ref,
                           wpg_ref, wple_ref, gfin_ref)


def _front_kernel(x_ref, lbuf_ref, h0_ref, mbuf_ref, gmix_ref, wbr_ref, wmg_ref, wif_ref, bif_ref,
                  lcw_ref, lcb_ref, wax_ref, ba_ref, bx_ref, lam_ref, mcw_ref, mcb_ref,
                  wq_ref, wk_ref, wv_ref, m0_ref, n0_ref,
                  q_ref, kw_ref, v_ref, sc_ref, hv_ref, hc_ref, n_out, m_out,
                  yl_ref, gm_ref, lbuf_out, h_out, mbuf_out):
    xnb = _rms(x_ref[...], gmix_ref[...]).astype(BF)

    def per_head(cols):
        return jnp.concatenate([jnp.broadcast_to(cols[:, h:h + 1], (cols.shape[0], M_HD))
                                for h in range(M_HEADS)], axis=1)

    def head_sums(a):
        return jnp.concatenate([jnp.sum(a[:, h * M_HD:(h + 1) * M_HD], axis=1, keepdims=True)
                                for h in range(M_HEADS)], axis=1)

    def conv_step(buf_ref, buf_out, x_new, w_ref, b_ref):
        out = b_ref[...] + x_new * w_ref[CONV_W - 1:CONV_W, :]
        for j in range(CONV_W - 1):
            out = out + buf_ref[j] * w_ref[j:j + 1, :]
        for j in range(CONV_W - 2):
            buf_out[j] = buf_ref[j + 1]
        buf_out[CONV_W - 2] = x_new
        return out

    x_l = _dot(xnb, wbr_ref[:, 0:D_MODEL])
    xl_c = conv_step(lbuf_ref, lbuf_out, x_l, lcw_ref, lcb_ref)
    ga, gx = _lru_gates(xl_c.astype(BF), wax_ref, ba_ref[...], bx_ref[...])
    a, u = _lru_coeffs(xl_c, ga, gx, lam_ref[...])
    y_l = a * h0_ref[...] + u
    h_out[...] = y_l
    g_l = _dot(xnb, wmg_ref[:, 0:D_MODEL])
    yl_ref[...] = _sigmoid(g_l) * y_l

    x_m = _dot(xnb, wbr_ref[:, D_MODEL:2 * D_MODEL])
    xm_c = conv_step(mbuf_ref, mbuf_out, x_m, mcw_ref, mcb_ref)
    xcb = (xm_c * _sigmoid(xm_c)).astype(BF)
    xmb = x_m.astype(BF)
    q, k, v = [], [], []
    for h in range(M_HEADS):
        hs = slice(h * M_HD, (h + 1) * M_HD)
        q.append(_dot(xcb[:, hs], wq_ref[h]) * (M_HD ** -0.5))
        k.append(_dot(xcb[:, hs], wk_ref[h]))
        v.append(_dot(xmb[:, hs], wv_ref[h]))
    q, k, v = (jnp.concatenate(a, axis=1) for a in (q, k, v))
    o_m = _dot(xnb, wbr_ref[:, 2 * D_MODEL:3 * D_MODEL])
    g_m = _dot(xnb, wmg_ref[:, D_MODEL:2 * D_MODEL])
    gm_ref[...] = _sigmoid(g_m) * _sigmoid(o_m)

    pre = _dot(xnb, wif_ref[...]) + bif_ref[...]
    ig = pre[:, 0:M_HEADS]
    lf = -_softplus(-pre[:, M_HEADS:N_GATES])
    n_prev = n0_ref[...]
    m_inter = lf + m0_ref[...]
    m_t = jnp.maximum(m_inter, ig)
    wk = jnp.exp(ig - m_t)
    sc = jnp.exp(m_inter - m_t)
    s = head_sums(q * k) * wk
    den = s + sc * head_sums(q * n_prev)
    rden = 1.0 / jnp.maximum(jnp.abs(den), jnp.exp(-m_t))
    wk_d, sc_d = per_head(wk), per_head(sc)
    hv_ref[...] = per_head(s * rden) * v
    hc_ref[...] = per_head(sc * rden)
    q_ref[...] = q
    kw_ref[...] = wk_d * k
    v_ref[...] = v
    sc_ref[...] = sc
    n_out[...] = sc_d * n_prev + wk_d * k
    m_out[...] = m_t


def _state_kernel(sc_ref, q_ref, kw_ref, v_ref, c_ref, qc_out, c_out):
    base = pl.program_id(0) * STATE_BATCH
    heads = [slice(h * M_HD, (h + 1) * M_HD) for h in range(M_HEADS)]
    cols = []
    for bb in range(STATE_BATCH):
        row = pl.ds(base + bb, 1)
        q, kw = q_ref[row, :], kw_ref[row, :]
        cols.append(jnp.concatenate([q[:, hs] for hs in heads] + [kw[:, hs] for hs in heads],
                                    axis=0).T)
    for bb in range(STATE_BATCH):
        b = base + bb
        v = v_ref[pl.ds(b, 1), :]
        for h, hs in enumerate(heads):
            q_col, kw_col = cols[bb][:, h:h + 1], cols[bb][:, M_HEADS + h:M_HEADS + h + 1]
            c_prev = c_ref[bb, h]
            qc_out[pl.ds(b, 1), hs] = jnp.sum(q_col * c_prev, axis=0, keepdims=True)
            c_out[bb, h] = sc_ref[b, h] * c_prev + kw_col * v[:, hs]


def _back_kernel(x_ref, qc_ref, hv_ref, hc_ref, yl_ref, gm_ref, p_ref, mg_ref, wout_ref, gffn_ref,
                 wg_ref, wu_ref, wd_ref, gple_ref, wpg_ref, wple_ref, gfin_ref, y_ref, mrg_ref):
    for h in range(M_HEADS):
        hs = slice(h * M_HD, (h + 1) * M_HD)
        hh = hv_ref[:, hs] + hc_ref[:, hs] * qc_ref[:, hs]
        hn = hh * lax.rsqrt(jnp.mean(hh * hh, axis=-1, keepdims=True) + EPS) * mg_ref[:, hs]
        mrg_ref[:, hs] = (yl_ref[:, hs] + gm_ref[:, hs] * hn).astype(BF)
    x1 = x_ref[...] + _dot(mrg_ref[...], wout_ref[...])
    y_ref[...] = _ffn_tail(x1, p_ref[...], gffn_ref, wg_ref, wu_ref, wd_ref, gple_ref, wpg_ref,
                           wple_ref, gfin_ref)


def _resident(shape):
    nd = len(shape)
    return pl.BlockSpec(shape, lambda *_: (0,) * nd, pipeline_mode=pl.Buffered(1))


def _params(n_axes):
    return pltpu.CompilerParams(dimension_semantics=("arbitrary",) * n_axes,
                                vmem_limit_bytes=VMEM_LIMIT)


def _interleave_matrix():
    r = np.arange(SEQ_TILE)
    perm = np.zeros((SEQ_TILE, SEQ_TILE), np.float32)
    perm[r, (r % SUBLANES) * SEG + r // SUBLANES] = 1.0
    return perm


def kernel(x_prompt, x_sample, state_lru_conv, state_lru_h, state_mlstm_conv, state_mlstm_C, state_mlstm_n, state_mlstm_m, p_prompt, p_sample, norm_mix_g, w_in, b_gates, lru_conv_w, lru_conv_b, lru_w_a, lru_b_a, lru_w_x, lru_b_x, lru_lambda, mlstm_conv_w, mlstm_conv_b, w_q, w_k, w_v, mlstm_norm_g, w_out, norm_ffn_g, w_ffn_gate, w_ffn_up, w_ffn_down, norm_ple_g, w_ple_gate, w_ple, final_norm_g):
    assert w_in.shape[0] == 1, "single-layer trunk"
    B, T, _ = x_prompt.shape
    S = x_sample.shape[0]
    assert T % SEQ_TILE == 0 and (B * T) % FFN_TILE == 0 and x_sample.shape[1] == 1
    assert S % STATE_BATCH == 0

    g0 = 2 * D_MODEL + D_MODEL
    w0 = w_in[0]
    wbr = w0[:, :g0].astype(BF)
    wmg = w0[:, g0 + N_GATES:].astype(BF)
    wif =jnp.pad(w0[:, g0:g0 + N_GATES], ((0, 0), (0, GATE_PAD - N_GATES))).astype(BF)
    bif = jnp.pad(b_gates[0], (0, GATE_PAD - N_GATES)).reshape(1, GATE_PAD)
    wax = jnp.concatenate([lru_w_a[0], lru_w_x[0]], axis=2).astype(BF)
    row = lambda a: a.reshape(1, -1)
    gmix, ba, bx, lam = row(norm_mix_g[0]), row(lru_b_a[0]), row(lru_b_x[0]), row(lru_lambda[0])
    lcw, lcb = lru_conv_w[0], row(lru_conv_b[0])
    mcw, mcb = mlstm_conv_w[0], row(mlstm_conv_b[0])
    wq, wk, wv = w_q[0].astype(BF), w_k[0].astype(BF), w_v[0].astype(BF)
    mg = row(mlstm_norm_g[0])
    wout = w_out[0].astype(BF)
    gffn, gple, gfin = row(norm_ffn_g[0]), row(norm_ple_g[0]), row(final_norm_g)
    wg, wu, wd = w_ffn_gate[0].astype(BF), w_ffn_up[0].astype(BF), w_ffn_down[0].astype(BF)
    wpg, wple = w_ple_gate[0].astype(BF), w_ple[0].astype(BF)
    perm_np = _interleave_matrix()
    perm, perm_t = jnp.asarray(perm_np, BF), jnp.asarray(perm_np.T, BF)

    front_w = (gmix, wbr, wmg, wif, bif, lcw, lcb, wax, ba, bx, lam, mcw, mcb, wq, wk, wv)
    mixer_w = (perm, perm_t) + front_w + (mg, wout)
    ffn_w = (gffn, wg, wu, wd, gple, wpg, wple, gfin)

    nt = T // SEQ_TILE
    sds = jax.ShapeDtypeStruct
    x1, p_lconv, p_h, p_mconv, p_c, p_n, p_m = pl.pallas_call(
        _mixer_kernel,
        grid=(B, nt),
        in_specs=[pl.BlockSpec((1, SEQ_TILE, D_MODEL), lambda b, t: (b, t, 0))]
                 + [_resident(w.shape) for w in mixer_w],
        out_specs=[
            pl.BlockSpec((1, SEQ_TILE, D_MODEL), lambda b, t: (b, t, 0)),
            pl.BlockSpec((1, CONV_W - 1, D_MODEL), lambda b, t: (b, 0, 0)),
            pl.BlockSpec((1, 1, D_MODEL), lambda b, t: (b, 0, 0)),
            pl.BlockSpec((1, CONV_W - 1, D_MODEL), lambda b, t: (b, 0, 0)),
            pl.BlockSpec((1, M_HEADS, M_HD, M_HD), lambda b, t: (b, 0, 0, 0)),
            pl.BlockSpec((1, M_HEADS, M_HD), lambda b, t: (b, 0, 0)),
            pl.BlockSpec((1, 1, M_HEADS), lambda b, t: (b, 0, 0)),
        ],
        out_shape=[
            sds((B, T, D_MODEL), F32),
            sds((B, CONV_W - 1, D_MODEL), F32),
            sds((B, 1, D_MODEL), F32),
            sds((B, CONV_W - 1, D_MODEL), F32),
            sds((B, M_HEADS, M_HD, M_HD), F32),
            sds((B, M_HEADS, M_HD), F32),
            sds((B, 1, M_HEADS), F32),
        ],
        compiler_params=_params(2),
        name="prompt_mixer",
    )(x_prompt, *mixer_w)

    n_tok = B * T
    y_prompt = pl.pallas_call(
        _ffn_kernel,
        grid=(n_tok // FFN_TILE,),
        in_specs=[pl.BlockSpec((FFN_TILE, D_MODEL), lambda i: (i, 0)),
                  pl.BlockSpec((FFN_TILE, P_DIM), lambda i: (i, 0))]
                 + [_resident(w.shape) for w in ffn_w],
        out_specs=pl.BlockSpec((FFN_TILE, D_MODEL), lambda i: (i, 0)),
        out_shape=sds((n_tok, D_MODEL), F32),
        compiler_params=_params(1),
        name="prompt_ffn",
    )(x1.reshape(n_tok, D_MODEL), p_prompt[0].reshape(n_tok, P_DIM), *ffn_w)

    xs = x_sample.reshape(S, D_MODEL)
    lbuf = jnp.swapaxes(state_lru_conv[0], 0, 1)
    mbuf = jnp.swapaxes(state_mlstm_conv[0], 0, 1)
    tok = sds((S, D_MODEL), F32)
    buf = sds((CONV_W - 1, S, D_MODEL), F32)
    per_head = sds((S, M_HEADS), F32)
    q, kw, v, sc, hv, hc, s_n, s_m, yl, gm, s_lbuf, s_h, s_mbuf = pl.pallas_call(
        _front_kernel,
        out_shape=[tok, tok, tok, per_head, tok, tok, tok, per_head, tok, tok, buf, tok, buf],
        compiler_params=pltpu.CompilerParams(vmem_limit_bytes=VMEM_LIMIT),
        name="sample_front",
    )(xs, lbuf, state_lru_h[0], mbuf, *front_w, state_mlstm_m[0],
      state_mlstm_n[0].reshape(S, D_MODEL))

    whole = pl.BlockSpec((S, D_MODEL), lambda b: (0, 0))
    c_spec = pl.BlockSpec((STATE_BATCH, M_HEADS, M_HD, M_HD), lambda b: (b, 0, 0, 0))
    qc, s_c = pl.pallas_call(
        _state_kernel,
        grid=(S // STATE_BATCH,),
        in_specs=[pl.BlockSpec(memory_space=pltpu.SMEM), whole, whole, whole, c_spec],
        out_specs=[whole, c_spec],
        out_shape=[tok, sds((S, M_HEADS, M_HD, M_HD), F32)],
        compiler_params=_params(1),
        name="sample_state",
    )(sc, q, kw, v, state_mlstm_C[0])

    y_sample = pl.pallas_call(
        _back_kernel,
        out_shape=tok,
        scratch_shapes=[pltpu.VMEM((S, D_MODEL), BF)],
        compiler_params=pltpu.CompilerParams(vmem_limit_bytes=VMEM_LIMIT),
        name="sample_back",
    )(xs, qc, hv, hc, yl, gm, p_sample[0].reshape(S, P_DIM), mg, wout, *ffn_w)

    lead = lambda a: a[None]
    return (y_prompt.reshape(B, T, D_MODEL), y_sample.reshape(S, 1, D_MODEL),
            lead(p_lconv), p_h.reshape(1, B, D_MODEL), lead(p_mconv), lead(p_c), lead(p_n),
            p_m.reshape(1, B, M_HEADS),
            lead(jnp.swapaxes(s_lbuf, 0, 1)), lead(s_h), lead(jnp.swapaxes(s_mbuf, 0, 1)),
            lead(s_c), s_n.reshape(1, S, M_HEADS, M_HD), lead(s_m))
```

```python
import numpy as np

import jax
import jax.numpy as jnp
from jax import lax
from jax.experimental import pallas as pl
from jax.experimental.pallas import tpu as pltpu

D_MODEL = 1024
M_HEADS = 4
M_HD = D_MODEL // M_HEADS
LRU_BLOCKS = 8
LRU_BLK = D_MODEL // LRU_BLOCKS
LRU_C = 8.0
CONV_W = 4
D_FF = 2816
P_DIM = 256
EPS = 1e-6
M_INIT = -1e30
MASKED = -1e30
N_GATES = 2 * M_HEADS
GATE_PAD = 128
SUBLANES = 8
SEQ_TILE = 256
SEG = SEQ_TILE // SUBLANES
SUB_TILES = 2
FFN_TILE = 512
FF_CHUNK = 256
STATE_BATCH = 8
VMEM_LIMIT = 56 * 1024 * 1024

BF = jnp.bfloat16
F32 = jnp.float32


def _dot(a, b):
    return jnp.dot(a, b, preferred_element_type=F32)


def _dot_nt(a, b):
    return lax.dot_general(a, b, (((1,), (1,)), ((), ())), preferred_element_type=F32)


def _dot_tn(a, b):
    return lax.dot_general(a, b, (((0,), (0,)), ((), ())), preferred_element_type=F32)


def _sigmoid(x):
    return 1.0 / (1.0 + jnp.exp(-x))


def _softplus(x):
    return jnp.maximum(x, 0.0) + jnp.log1p(jnp.exp(-jnp.abs(x)))


def _rms(x, g):
    return x * lax.rsqrt(jnp.mean(x * x, axis=-1, keepdims=True) + EPS) * g


def _group(x, i):
    return x[i * SUBLANES:(i + 1) * SUBLANES, :]


def _lru_coeffs(xc, ga, gx, lam):
    r = _sigmoid(ga)
    ig = _sigmoid(gx)
    log_a = -LRU_C * r * _softplus(-lam)
    a = jnp.exp(log_a)
    om = 1.0 - a * a
    root = jnp.where(om > 0.0, om * lax.rsqrt(om), 0.0)
    return a, root * ig * xc


def _lru_gates(xcb, wax_ref, ba, bx):
    ga, gx = [], []
    for n in range(LRU_BLOCKS):
        g = _dot(xcb[:, n * LRU_BLK:(n + 1) * LRU_BLK], wax_ref[n])
        ga.append(g[:, :LRU_BLK])
        gx.append(g[:, LRU_BLK:])
    return jnp.concatenate(ga, axis=1) + ba, jnp.concatenate(gx, axis=1) + bx


def _scan_interleaved(a, u, h0):
    prods, sums = [], []
    p = s = None
    for i in range(SEG):
        ai, ui = _group(a, i), _group(u, i)
        p, s = (ai, ui) if i == 0 else (ai * p, ai * s + ui)
        prods.append(p)
        sums.append(s)
    c = h0
    starts = [c]
    for j in range(SUBLANES - 1):
        c = p[j:j + 1, :] * c + s[j:j + 1, :]
        starts.append(c)
    start = jnp.concatenate(starts, axis=0)
    return jnp.concatenate([sums[i] + prods[i] * start for i in range(SEG)], axis=0)


def _cumsum_interleaved(x):
    acc = []
    run = None
    for i in range(SEG):
        run = _group(x, i) if i == 0 else run + _group(x, i)
        acc.append(run)
    sub = lax.broadcasted_iota(jnp.int32, run.shape, 0)
    inc = run
    s = 1
    while s < SUBLANES:
        inc = inc + jnp.where(sub >= s, pltpu.roll(inc, s, 0), 0.0)
        s *= 2
    before = inc - run
    return jnp.concatenate([r + before for r in acc], axis=0)


def _conv_interleaved(carry, x_new, w, b):
    sub = lax.broadcasted_iota(jnp.int32, (SUBLANES, x_new.shape[1]), 0)
    head, new = [], []
    for r in range(CONV_W - 1):
        cur = _group(x_new, SEG - (CONV_W - 1) + r)
        head.append(jnp.where(sub == 0, carry[r:r + 1, :], pltpu.roll(cur, 1, 0)))
        new.append(cur[SUBLANES - 1:SUBLANES, :])
    ext = jnp.concatenate(head + [x_new], axis=0)
    out = b
    for j in range(CONV_W):
        out = out + ext[j * SUBLANES:j * SUBLANES + SEQ_TILE, :] * w[j:j + 1, :]
    return out, jnp.concatenate(new, axis=0)


def _time_of_row(r):
    return (r & (SUBLANES - 1)) * SEG + (r >> (SUBLANES.bit_length() - 1))


class _Plan:
    def __init__(self):
        self.tasks = {}

    def add(self, name, unit, cost, deps, fn):
        self.tasks[name] = (unit, cost, tuple(deps), fn)

    def order(self):
        succ = {n: [] for n in self.tasks}
        for n, (_, _, deps, _) in self.tasks.items():
            for p in deps:
                succ[p].append(n)
        tail = {}

        def path(n):
            if n not in tail:
                tail[n] = self.tasks[n][1] + max([path(s) for s in succ[n]], default=0)
            return tail[n]

        free = {"M": 0, "V": 0}
        done, order, left = {}, [], list(self.tasks)
        while left:
            ready = [n for n in left if all(p in done for p in self.tasks[n][2])]

            def start(n):
                unit, _, deps, _ = self.tasks[n]
                return max([free[unit]] + [done[p] for p in deps])

            n = min(ready, key=lambda n: (start(n), -path(n)))
            unit, cost, _, _ = self.tasks[n]
            st = start(n)
            done[n] = free[unit] = st + cost
            order.append((st, len(order), n))
            left.remove(n)
        return [n for _, _, n in sorted(order)]

    def run(self):
        for n in self.order():
            self.tasks[n][3]()


def _mixer_kernel(x_ref, perm_ref, permt_ref, gmix_ref, wbr_ref, wmg_ref, wif_ref, bif_ref, lcw_ref,
                  lcb_ref, wax_ref, ba_ref, bx_ref, lam_ref, mcw_ref, mcb_ref, wq_ref, wk_ref,
                  wv_ref, mg_ref, wout_ref,
                  x1_ref, lconv_ref, h_ref, mconv_ref, c_ref, n_ref, m_ref):
    tt = SEQ_TILE
    last = tt - 1

    @pl.when(pl.program_id(1) == 0)
    def _():
        lconv_ref[...] = jnp.zeros(lconv_ref.shape, F32)
        mconv_ref[...] = jnp.zeros(mconv_ref.shape, F32)
        h_ref[...] = jnp.zeros(h_ref.shape, F32)
        c_ref[...] = jnp.zeros(c_ref.shape, F32)
        n_ref[...] = jnp.zeros(n_ref.shape, F32)
        m_ref[...] = jnp.full(m_ref.shape, M_INIT, F32)

    lcw, lcb, mcw, mcb = lcw_ref[...], lcb_ref[...], mcw_ref[...], mcb_ref[...]
    ba, bx, lam, mg = ba_ref[...], bx_ref[...], lam_ref[...], mg_ref[...]

    v = {}
    for g in range(M_HEADS):
        gs = slice(g * M_HD, (g + 1) * M_HD)
        v["lconv", -1, g], v["mconv", -1, g] = lconv_ref[0, :, gs], mconv_ref[0, :, gs]
        v["h", -1, g] = h_ref[0, :, gs]
        v["c", -1, g], v["n", -1, g] = c_ref[0, g], n_ref[0, g:g + 1, :]
        v["m", -1, g] = m_ref[0, :, g:g + 1]
    plan = _Plan()

    for t in range(SUB_TILES):
        dep_prev = (lambda name, g, t=t: ((name, t - 1, g),)) if t else (lambda name, g: ())

        def t_norm(t=t):
            v["x", t] = x_ref[0, t * tt:(t + 1) * tt, :]
            v["xn_t", t] = _rms(v["x", t], gmix_ref[...]).astype(BF)
        plan.add(("norm", t), "V", 650, (), t_norm)

        def t_perm(t=t):
            v["xnb", t] = _dot(perm_ref[...], v["xn_t", t]).astype(BF)
        plan.add(("perm", t), "M", 260, (("norm", t),), t_perm)

        def proj(name, w_ref, col0, g, t=t):
            def run():
                v[name, t, g] = _dot(v["xnb", t], w_ref[:, col0 + g * M_HD:col0 + (g + 1) * M_HD])
            plan.add((name, t, g), "M", 260, (("perm", t),), run)

        def t_wif(t=t):
            v["pre", t] = _dot(v["xnb", t], wif_ref[...]) + bif_ref[...]
        plan.add(("wif", t), "M", 260, (("perm", t),), t_wif)

        def t_gcum(t=t):
            pre = v["pre", t]
            lane = lax.broadcasted_iota(jnp.int32, (tt, GATE_PAD), 1)
            v["gcol", t] = jnp.where(lane < M_HEADS, pre, _cumsum_interleaved(-_softplus(-pre)))
            v["grow", t] = v["gcol", t].T
            v["tri", t] = (_time_of_row(lax.broadcasted_iota(jnp.int32, (tt, 1), 0))
                           >= _time_of_row(lax.broadcasted_iota(jnp.int32, (1, tt), 1)))
        plan.add(("gcum", t), "V", 150, (("wif", t),), t_gcum)

        for g in range(M_HEADS):
            gs = slice(g * M_HD, (g + 1) * M_HD)
            for name, w_ref, col0 in (("xl", wbr_ref, 0), ("xm", wbr_ref, D_MODEL),
                                      ("om", wbr_ref, 2 * D_MODEL), ("gl", wmg_ref, 0),
                                      ("gm", wmg_ref, D_MODEL)):
                proj(name, w_ref, col0, g)

            def t_convl(t=t, g=g, gs=gs):
                v["xlc", t, g], v["lconv", t, g] = _conv_interleaved(
                    v["lconv", t - 1, g], v["xl", t, g], lcw[:, gs], lcb[:, gs])
            plan.add(("convl", t, g), "V", 135, (("xl", t, g),) + dep_prev("convl", g), t_convl)

            def t_gates(t=t, g=g, gs=gs):
                xlb = v["xlc", t, g].astype(BF)
                ga, gx = [], []
                for n in range(2):
                    gg = _dot(xlb[:, n * LRU_BLK:(n + 1) * LRU_BLK], wax_ref[2 * g + n])
                    ga.append(gg[:, :LRU_BLK])
                    gx.append(gg[:, LRU_BLK:])
                v["ga", t, g] = jnp.concatenate(ga, axis=1) + ba[:, gs]
                v["gx", t, g] = jnp.concatenate(gx, axis=1) + bx[:, gs]
            plan.add(("gates", t, g), "M", 130, (("convl", t, g),), t_gates)

            def t_coef(t=t, g=g, gs=gs):
                v["a", t, g], v["u", t, g] = _lru_coeffs(v["xlc", t, g], v["ga", t, g], v["gx", t, g],
                                                         lam[:, gs])
            plan.add(("coef", t, g), "V", 280, (("gates", t, g),), t_coef)

            def t_scan(t=t, g=g):
                v["yl", t, g] = _scan_interleaved(v["a", t, g], v["u", t, g], v["h", t - 1, g])
                v["h", t, g] = v["yl", t, g][last:last + 1, :]
            plan.add(("scan", t, g), "V", 120, (("coef", t, g),) + dep_prev("scan", g), t_scan)

            h = g

            def t_convm(t=t, h=h, gs=gs):
                xm_c, v["mconv", t, h] = _conv_interleaved(v["mconv", t - 1, h], v["xm", t, h],
                                                           mcw[:, gs], mcb[:, gs])
                v["xcb", t, h] = (xm_c * _sigmoid(xm_c)).astype(BF)
            plan.add(("convm", t, h), "V", 210, (("xm", t, h),) + dep_prev("convm", h), t_convm)

            def t_qkv(t=t, h=h):
                v["q", t, h] = _dot(v["xcb", t, h], wq_ref[h]) * (M_HD ** -0.5)
                v["k", t, h] = _dot(v["xcb", t, h], wk_ref[h])
                vv = _dot(v["xm", t, h].astype(BF), wv_ref[h])
                v["qb", t, h], v["kb", t, h], v["vb", t, h] = (
                    v["q", t, h].astype(BF), v["k", t, h].astype(BF), vv.astype(BF))
            plan.add(("qkv", t, h), "M", 200, (("convm", t, h),), t_qkv)

            def t_qk(t=t, h=h):
                v["qk", t, h] = _dot_nt(v["qb", t, h], v["kb", t, h])
            plan.add(("qk", t, h), "M", 64, (("qkv", t, h),), t_qk)

            def t_sp(t=t, h=h):
                gcol, grow = v["gcol", t], v["grow", t]
                b_col = gcol[:, M_HEADS + h:M_HEADS + h + 1]
                ig_row, b_row = grow[h:h + 1, :], grow[M_HEADS + h:M_HEADS + h + 1, :]
                dlog = jnp.where(v["tri", t], b_col - b_row + ig_row, MASKED)
                m_inter = b_col + v["m", t - 1, h]
                m_t = jnp.maximum(m_inter, jnp.max(dlog, axis=1, keepdims=True))
                s = v["qk", t, h] * jnp.exp(dlog - m_t)
                v["m_t", t, h], v["sc", t, h] = m_t, jnp.exp(m_inter - m_t)
                v["ssum", t, h] = jnp.sum(s, axis=1, keepdims=True)
                v["sb", t, h] = s.astype(BF)
                v["m", t, h] = m_t[last:last + 1, :]
            plan.add(("sp", t, h), "V", 200, (("qk", t, h), ("gcum", t)) + dep_prev("sp", h), t_sp)

            def t_sv(t=t, h=h):
                v["sv", t, h] = _dot(v["sb", t, h], v["vb", t, h])
                v["qc", t, h] = _dot(v["qb", t, h], v["c", t - 1, h].astype(BF))
            plan.add(("sv", t, h), "M", 130, (("sp", t, h),) + dep_prev("cnew", h), t_sv)

            def t_hn(t=t, h=h, gs=gs):
                sc, m_t = v["sc", t, h], v["m_t", t, h]
                num = v["sv", t, h] + sc * v["qc", t, h]
                den = v["ssum", t, h] + sc * jnp.sum(v["q", t, h] * v["n", t - 1, h], axis=1,
                                                     keepdims=True)
                hh = num / jnp.maximum(jnp.abs(den), jnp.exp(-m_t))
                v["hn", t, h] = (hh * lax.rsqrt(jnp.mean(hh * hh, axis=-1, keepdims=True) + EPS)
                                 * mg[:, gs])
            plan.add(("hn", t, h), "V", 170, (("sv", t, h),) + dep_prev("kw", h), t_hn)

            def t_kw(t=t, h=h):
                gcol = v["gcol", t]
                ig_col, b_col = gcol[:, h:h + 1], gcol[:, M_HEADS + h:M_HEADS + h + 1]
                m_last = v["m", t, h]
                b_last = b_col[last:last + 1, :]
                kw = v["k", t, h] * jnp.exp(b_last - b_col + ig_col - m_last)
                v["dec", t, h] = jnp.exp(b_last + v["m", t - 1, h] - m_last)
                v["n", t, h] = v["dec", t, h] * v["n", t - 1, h] + jnp.sum(kw, axis=0, keepdims=True)
                v["kwb", t, h] = kw.astype(BF)
            plan.add(("kw", t, h), "V", 80, (("sp", t, h),) + dep_prev("kw", h), t_kw)

            def t_ckv(t=t, h=h):
                v["ckv", t, h] = _dot_tn(v["kwb", t, h], v["vb", t, h])
            plan.add(("ckv", t, h), "M", 64, (("kw", t, h),), t_ckv)

            def t_cnew(t=t, h=h):
                v["c", t, h] = v["dec", t, h] * v["c", t - 1, h] + v["ckv", t, h]
            plan.add(("cnew", t, h), "V", 40, (("ckv", t, h),) + dep_prev("cnew", h), t_cnew)

            def t_sig(t=t, h=h):
                v["gate_l", t, h] = _sigmoid(v["gl", t, h]) * v["yl", t, h]
                v["gate_m", t, h] = _sigmoid(v["gm", t, h]) * _sigmoid(v["om", t, h])
            plan.add(("sig", t, h), "V", 180,
                     (("gl", t, h), ("gm", t, h), ("om", t, h), ("scan", t, h)), t_sig)

            def t_mrg(t=t, h=h):
                v["mrg", t, h] = (v["gate_l", t, h] + v["gate_m", t, h] * v["hn", t, h]).astype(BF)
            plan.add(("mrg", t, h), "V", 40, (("sig", t, h), ("hn", t, h)), t_mrg)

            def t_pt(t=t, h=h):
                v["mrg_t", t, h] = _dot(permt_ref[...], v["mrg", t, h]).astype(BF)
            plan.add(("pt", t, h), "M", 64, (("mrg", t, h),), t_pt)

            def t_wo(t=t, h=h, gs=gs):
                part = _dot(v["mrg_t", t, h], wout_ref[gs, :])
                v["out", t] = part if h == 0 else v["out", t] + part
            plan.add(("wo", t, h), "M", 260, (("pt", t, h),) + ((("wo", t, h - 1),) if h else ()),
                     t_wo)

        def t_fin(t=t):
            x1_ref[0, t * tt:(t + 1) * tt, :] = v["x", t] + v["out", t]
        plan.add(("fin", t), "V", 70, (("wo", t, M_HEADS - 1),), t_fin)

    def t_state():
        e = SUB_TILES - 1
        heads = range(M_HEADS)
        lconv_ref[0] = jnp.concatenate([v["lconv", e, g] for g in heads], axis=1)
        mconv_ref[0] = jnp.concatenate([v["mconv", e, g] for g in heads], axis=1)
        h_ref[0] = jnp.concatenate([v["h", e, g] for g in heads], axis=1)
        for h in heads:
            c_ref[0, h] = v["c", e, h]
        n_ref[0] = jnp.concatenate([v["n", e, h] for h in heads], axis=0)
        m_ref[0] = jnp.concatenate([v["m", e, h] for h in heads], axis=1)
    plan.add("state", "V", 60, tuple(("cnew", SUB_TILES - 1, h) for h in range(M_HEADS))
             + tuple(("scan", SUB_TILES - 1, h) for h in range(M_HEADS))
             + tuple(("fin", t) for t in range(SUB_TILES)), t_state)

    plan.run()


def _ffn_tail(x1, p, gffn_ref, wg_ref, wu_ref, wd_ref, gple_ref, wpg_ref, wple_ref, gfin_ref):
    xnb = _rms(x1, gffn_ref[...]).astype(BF)
    x2 = x1
    for c in range(D_FF // FF_CHUNK):
        cs = slice(c * FF_CHUNK, (c + 1) * FF_CHUNK)
        hg = _dot(xnb, wg_ref[:, cs])
        hu = _dot(xnb, wu_ref[:, cs])
        act = (hg * _sigmoid(hg) * hu).astype(BF)
        x2 = x2 + _dot(act, wd_ref[cs, :])
    gate = _sigmoid(_dot(_rms(x2, gple_ref[...]).astype(BF), wpg_ref[...]))
    x3 = x2 + gate * _dot(p.astype(BF), wple_ref[...])
    return _rms(x3, gfin_ref[...])


def _ffn_kernel(x1_ref, p_ref, gffn_ref, wg_ref, wu_ref, wd_ref, gple_ref, wpg_ref, wple_ref,
                gfin_ref, y_ref):
    y_ref[...] = _ffn_tail(x1_ref[...], p_ref[...], gffn_ref, wg_ref, wu_ref, wd_ref, gple_ref,
                           wpg_ref, wple_ref, gfin_ref)


def _front_kernel(x_ref, lbuf_ref, h0_ref, mbuf_ref, gmix_ref, wbr_ref, wmg_ref, wif_ref, bif_ref,
                  lcw_ref, lcb_ref, wax_ref, ba_ref, bx_ref, lam_ref, mcw_ref, mcb_ref,
                  wq_ref, wk_ref, wv_ref, m0_ref, n0_ref,
                  q_ref, kw_ref, v_ref, sc_ref, hv_ref, hc_ref, n_out, m_out,
                  yl_ref, gm_ref, lbuf_out, h_out, mbuf_out):
    xnb = _rms(x_ref[...], gmix_ref[...]).astype(BF)

    def per_head(cols):
        return jnp.concatenate([jnp.broadcast_to(cols[:, h:h + 1], (cols.shape[0], M_HD))
                                for h in range(M_HEADS)], axis=1)

    def head_sums(a):
        return jnp.concatenate([jnp.sum(a[:, h * M_HD:(h + 1) * M_HD], axis=1, keepdims=True)
                                for h in range(M_HEADS)], axis=1)

    def conv_step(buf_ref, buf_out, x_new, w_ref, b_ref):
        out = b_ref[...] + x_new * w_ref[CONV_W - 1:CONV_W, :]
        for j in range(CONV_W - 1):
            out = out + buf_ref[j] * w_ref[j:j + 1, :]
        for j in range(CONV_W - 2):
            buf_out[j] = buf_ref[j + 1]
        buf_out[CONV_W - 2] = x_new
        return out

    x_l = _dot(xnb, wbr_ref[:, 0:D_MODEL])
    xl_c = conv_step(lbuf_ref, lbuf_out, x_l, lcw_ref, lcb_ref)
    ga, gx = _lru_gates(xl_c.astype(BF), wax_ref, ba_ref[...], bx_ref[...])
    a, u = _lru_coeffs(xl_c, ga, gx, lam_ref[...])
    y_l = a * h0_ref[...] + u
    h_out[...] = y_l
    g_l = _dot(xnb, wmg_ref[:, 0:D_MODEL])
    yl_ref[...] = _sigmoid(g_l) * y_l

    x_m = _dot(xnb, wbr_ref[:, D_MODEL:2 * D_MODEL])
    xm_c = conv_step(mbuf_ref, mbuf_out, x_m, mcw_ref, mcb_ref)
    xcb = (xm_c * _sigmoid(xm_c)).astype(BF)
    xmb = x_m.astype(BF)
    q, k, v = [], [], []
    for h in range(M_HEADS):
        hs = slice(h * M_HD, (h + 1) * M_HD)
        q.append(_dot(xcb[:, hs], wq_ref[h]) * (M_HD ** -0.5))
        k.append(_dot(xcb[:, hs], wk_ref[h]))
        v.append(_dot(xmb[:, hs], wv_ref[h]))
    q, k, v = (jnp.concatenate(a, axis=1) for a in (q, k, v))
    o_m = _dot(xnb, wbr_ref[:, 2 * D_MODEL:3 * D_MODEL])
    g_m = _dot(xnb, wmg_ref[:, D_MODEL:2 * D_MODEL])
    gm_ref[...] = _sigmoid(g_m) * _sigmoid(o_m)

    pre = _dot(xnb, wif_ref[...]) + bif_ref[...]
    ig = pre[:, 0:M_HEADS]
    lf = -_softplus(-pre[:, M_HEADS:N_GATES])
    n_prev = n0_ref[...]
    m_inter = lf + m0_ref[...]
    m_t = jnp.maximum(m_inter, ig)
    wk = jnp.exp(ig - m_t)
    sc = jnp.exp(m_inter - m_t)
    s = head_sums(q * k) * wk
    den = s + sc * head_sums(q * n_prev)
    rden = 1.0 / jnp.maximum(jnp.abs(den), jnp.exp(-m_t))
    wk_d, sc_d = per_head(wk), per_head(sc)
    hv_ref[...] = per_head(s * rden) * v
    hc_ref[...] = per_head(sc * rden)
    q_ref[...] = q
    kw_ref[...] = wk_d * k
    v_ref[...] = v
    sc_ref[...] = sc
    n_out[...] = sc_d * n_prev + wk_d * k
    m_out[...] = m_t


def _state_kernel(sc_ref, q_ref, kw_ref, v_ref, c_ref, qc_out, c_out):
    base = pl.program_id(0) * STATE_BATCH
    heads = [slice(h * M_HD, (h + 1) * M_HD) for h in range(M_HEADS)]
    cols = []
    for bb in range(STATE_BATCH):
        row = pl.ds(base + bb, 1)
        q, kw = q_ref[row, :], kw_ref[row, :]
        cols.append(jnp.concatenate([q[:, hs] for hs in heads] + [kw[:, hs] for hs in heads],
                                    axis=0).T)
    for bb in range(STATE_BATCH):
        b = base + bb
        v = v_ref[pl.ds(b, 1), :]
        for h, hs in enumerate(heads):
            q_col, kw_col = cols[bb][:, h:h + 1], cols[bb][:, M_HEADS + h:M_HEADS + h + 1]
            c_prev = c_ref[bb, h]
            qc_out[pl.ds(b, 1), hs] = jnp.sum(q_col * c_prev, axis=0, keepdims=True)
            c_out[bb, h] = sc_ref[b, h] * c_prev + kw_col * v[:, hs]


def _back_kernel(x_ref, qc_ref, hv_ref, hc_ref, yl_ref, gm_ref, p_ref, mg_ref, wout_ref, gffn_ref,
                 wg_ref, wu_ref, wd_ref, gple_ref, wpg_ref, wple_ref, gfin_ref, y_ref, mrg_ref):
    for h in range(M_HEADS):
        hs = slice(h * M_HD, (h + 1) * M_HD)
        hh = hv_ref[:, hs] + hc_ref[:, hs] * qc_ref[:, hs]
        hn = hh * lax.rsqrt(jnp.mean(hh * hh, axis=-1, keepdims=True) + EPS) * mg_ref[:, hs]
        mrg_ref[:, hs] = (yl_ref[:, hs] + gm_ref[:, hs] * hn).astype(BF)
    x1 = x_ref[...] + _dot(mrg_ref[...], wout_ref[...])
    y_ref[...] = _ffn_tail(x1, p_ref[...], gffn_ref, wg_ref, wu_ref, wd_ref, gple_ref, wpg_ref,
                           wple_ref, gfin_ref)


def _resident(shape):
    nd = len(shape)
    return pl.BlockSpec(shape, lambda *_: (0,) * nd, pipeline_mode=pl.Buffered(1))


def _params(n_axes):
    return pltpu.CompilerParams(dimension_semantics=("arbitrary",) * n_axes,
                                vmem_limit_bytes=VMEM_LIMIT)


def _interleave_matrix():
    r = np.arange(SEQ_TILE)
    perm = np.zeros((SEQ_TILE, SEQ_TILE), np.float32)
    perm[r, (r % SUBLANES) * SEG + r // SUBLANES] = 1.0
    return perm


def kernel(x_prompt, x_sample, state_lru_conv, state_lru_h, state_mlstm_conv, state_mlstm_C, state_mlstm_n, state_mlstm_m, p_prompt, p_sample, norm_mix_g, w_in, b_gates, lru_conv_w, lru_conv_b, lru_w_a, lru_b_a, lru_w_x, lru_b_x, lru_lambda, mlstm_conv_w, mlstm_conv_b, w_q, w_k, w_v, mlstm_norm_g, w_out, norm_ffn_g, w_ffn_gate, w_ffn_up, w_ffn_down, norm_ple_g, w_ple_gate, w_ple, final_norm_g):
    assert w_in.shape[0] == 1, "single-layer trunk"
    B, T, _ = x_prompt.shape
    S = x_sample.shape[0]
    step = SEQ_TILE * SUB_TILES
    assert T % step == 0 and (B * T) % FFN_TILE == 0 and x_sample.shape[1] == 1
    assert S % STATE_BATCH == 0

    g0 = 2 * D_MODEL + D_MODEL
    w0 = w_in[0]
    wbr = w0[:, :g0].astype(BF)
    wmg = w0[:, g0 + N_GATES:].astype(BF)
    wif = jnp.pad(w0[:, g0:g0 + N_GATES], ((0, 0), (0, GATE_PAD - N_GATES))).astype(BF)
    bif = jnp.pad(b_gates[0], (0, GATE_PAD - N_GATES)).reshape(1, GATE_PAD)
    wax = jnp.concatenate([lru_w_a[0], lru_w_x[0]], axis=2).astype(BF)
    row = lambda a: a.reshape(1, -1)
    gmix, ba, bx, lam = row(norm_mix_g[0]), row(lru_b_a[0]), row(lru_b_x[0]), row(lru_lambda[0])
    lcw, lcb = lru_conv_w[0], row(lru_conv_b[0])
    mcw, mcb = mlstm_conv_w[0], row(mlstm_conv_b[0])
    wq, wk, wv = w_q[0].astype(BF), w_k[0].astype(BF), w_v[0].astype(BF)
    mg = row(mlstm_norm_g[0])
    wout = w_out[0].astype(BF)
    gffn, gple, gfin = row(norm_ffn_g[0]), row(norm_ple_g[0]), row(final_norm_g)
    wg, wu, wd = w_ffn_gate[0].astype(BF), w_ffn_up[0].astype(BF), w_ffn_down[0].astype(BF)
    wpg, wple = w_ple_gate[0].astype(BF), w_ple[0].astype(BF)
    perm_np = _interleave_matrix()
    perm, perm_t = jnp.asarray(perm_np, BF), jnp.asarray(perm_np.T, BF)

    front_w = (gmix, wbr, wmg, wif, bif, lcw, lcb, wax, ba, bx, lam, mcw, mcb, wq, wk, wv)
    mixer_w = (perm, perm_t) + front_w + (mg, wout)
    ffn_w = (gffn, wg, wu, wd, gple, wpg, wple, gfin)

    nt = T // step
    sds = jax.ShapeDtypeStruct
    x1, p_lconv, p_h, p_mconv, p_c, p_n, p_m = pl.pallas_call(
        _mixer_kernel,
        grid=(B, nt),
        in_specs=[pl.BlockSpec((1, step, D_MODEL), lambda b, t: (b, t, 0))]
                 + [_resident(w.shape) for w in mixer_w],
        out_specs=[
            pl.BlockSpec((1, step, D_MODEL), lambda b, t: (b, t, 0)),
            pl.BlockSpec((1, CONV_W - 1, D_MODEL), lambda b, t: (b, 0, 0)),
            pl.BlockSpec((1, 1, D_MODEL), lambda b, t: (b, 0, 0)),
            pl.BlockSpec((1, CONV_W - 1, D_MODEL), lambda b, t: (b, 0, 0)),
            pl.BlockSpec((1, M_HEADS, M_HD, M_HD), lambda b, t: (b, 0, 0, 0)),
            pl.BlockSpec((1, M_HEADS, M_HD), lambda b, t: (b, 0, 0)),
            pl.BlockSpec((1, 1, M_HEADS), lambda b, t: (b, 0, 0)),
        ],
        out_shape=[
            sds((B, T, D_MODEL), F32),
            sds((B, CONV_W - 1, D_MODEL), F32),
            sds((B, 1, D_MODEL), F32),
            sds((B, CONV_W - 1, D_MODEL), F32),
            sds((B, M_HEADS, M_HD, M_HD), F32),
            sds((B, M_HEADS, M_HD), F32),
            sds((B, 1, M_HEADS), F32),
        ],
        compiler_params=_params(2),
        name="prompt_mixer",
    )(x_prompt, *mixer_w)

    n_tok = B * T
    y_prompt = pl.pallas_call(
        _ffn_kernel,
        grid=(n_tok // FFN_TILE,),
        in_specs=[pl.BlockSpec((FFN_TILE, D_MODEL), lambda i: (i, 0)),
                  pl.BlockSpec((FFN_TILE, P_DIM), lambda i: (i, 0))]
                 + [_resident(w.shape) for w in ffn_w],
        out_specs=pl.BlockSpec((FFN_TILE, D_MODEL), lambda i: (i, 0)),
        out_shape=sds((n_tok, D_MODEL), F32),
        compiler_params=_params(1),
        name="prompt_ffn",
    )(x1.reshape(n_tok, D_MODEL), p_prompt[0].reshape(n_tok, P_DIM), *ffn_w)

    xs = x_sample.reshape(S, D_MODEL)
    lbuf = jnp.swapaxes(state_lru_conv[0], 0, 1)
    mbuf = jnp.swapaxes(state_mlstm_conv[0], 0, 1)
    tok = sds((S, D_MODEL), F32)
    buf = sds((CONV_W - 1, S, D_MODEL), F32)
    per_head = sds((S, M_HEADS), F32)
    q, kw, v, sc, hv, hc, s_n, s_m, yl, gm, s_lbuf, s_h, s_mbuf = pl.pallas_call(
        _front_kernel,
        out_shape=[tok, tok, tok, per_head, tok, tok, tok, per_head, tok, tok, buf, tok, buf],
        compiler_params=pltpu.CompilerParams(vmem_limit_bytes=VMEM_LIMIT),
        name="sample_front",
    )(xs, lbuf, state_lru_h[0], mbuf, *front_w, state_mlstm_m[0],
      state_mlstm_n[0].reshape(S, D_MODEL))

    whole = pl.BlockSpec((S, D_MODEL), lambda b: (0, 0))
    c_spec = pl.BlockSpec((STATE_BATCH, M_HEADS, M_HD, M_HD), lambda b: (b, 0, 0, 0))
    qc, s_c = pl.pallas_call(
        _state_kernel,
        grid=(S // STATE_BATCH,),
        in_specs=[pl.BlockSpec(memory_space=pltpu.SMEM), whole, whole, whole, c_spec],
        out_specs=[whole, c_spec],
        out_shape=[tok, sds((S, M_HEADS, M_HD, M_HD), F32)],
        compiler_params=_params(1),
        name="sample_state",
    )(sc, q, kw, v, state_mlstm_C[0])

    y_sample = pl.pallas_call(
        _back_kernel,
        out_shape=tok,
        scratch_shapes=[pltpu.VMEM((S, D_MODEL), BF)],
        compiler_params=pltpu.CompilerParams(vmem_limit_bytes=VMEM_LIMIT),
        name="sample_back",
    )(xs, qc, hv, hc, yl, gm, p_sample[0].reshape(S, P_DIM), mg, wout, *ffn_w)

    lead = lambda a: a[None]
    return (y_prompt.reshape(B, T, D_MODEL), y_sample.reshape(S, 1, D_MODEL),
            lead(p_lconv), p_h.reshape(1, B, D_MODEL), lead(p_mconv), lead(p_c), lead(p_n),
            p_m.reshape(1, B, M_HEADS),
            lead(jnp.swapaxes(s_lbuf, 0, 1)), lead(s_h), lead(jnp.swapaxes(s_mbuf, 0, 1)),
            lead(s_c), s_n.reshape(1, S, M_HEADS, M_HD), lead(s_m))
```

```python
import numpy as np

import jax
import jax.numpy as jnp
from jax import lax
from jax.experimental import pallas as pl
from jax.experimental.pallas import tpu as pltpu

D_MODEL = 1024
M_HEADS = 4
M_HD = D_MODEL // M_HEADS
LRU_BLOCKS = 8
LRU_BLK = D_MODEL // LRU_BLOCKS
LRU_C = 8.0
CONV_W = 4
D_FF = 2816
P_DIM = 256
EPS = 1e-6
M_INIT = -1e30
MASKED = -1e30
N_GATES = 2 * M_HEADS
GATE_PAD = 128
SUBLANES = 8
SEQ_TILE = 256
SEG = SEQ_TILE // SUBLANES
SUB_TILES = 2
FFN_TILE = 512
FFN_SUB = FFN_TILE // SEQ_TILE
FF_CHUNK = 256
STATE_BATCH = 4
VMEM_LIMIT = 56 * 1024 * 1024

BF = jnp.bfloat16
F32 = jnp.float32


def _dot(a, b):
    return jnp.dot(a, b, preferred_element_type=F32)


def _dot_nt(a, b):
    return lax.dot_general(a, b, (((1,), (1,)), ((), ())), preferred_element_type=F32)


def _dot_tn(a, b):
    return lax.dot_general(a, b, (((0,), (0,)), ((), ())), preferred_element_type=F32)


def _sigmoid(x):
    return 1.0 / (1.0 + jnp.exp(-x))


def _softplus(x):
    return jnp.maximum(x, 0.0) + jnp.log1p(jnp.exp(-jnp.abs(x)))


def _rms(x, g):
    return x * lax.rsqrt(jnp.mean(x * x, axis=-1, keepdims=True) + EPS) * g


def _group(x, i):
    return x[i * SUBLANES:(i + 1) * SUBLANES, :]


def _lru_coeffs(xc, ga, gx, lam):
    r = _sigmoid(ga)
    ig = _sigmoid(gx)
    log_a = -LRU_C * r * _softplus(-lam)
    a = jnp.exp(log_a)
    om = 1.0 - a * a
    root = jnp.where(om > 0.0, om * lax.rsqrt(om), 0.0)
    return a, root * ig * xc


def _lru_gates(xcb, wax_ref, ba, bx):
    ga, gx = [], []
    for n in range(LRU_BLOCKS):
        g = _dot(xcb[:, n * LRU_BLK:(n + 1) * LRU_BLK], wax_ref[n])
        ga.append(g[:, :LRU_BLK])
        gx.append(g[:, LRU_BLK:])
    return jnp.concatenate(ga, axis=1) + ba, jnp.concatenate(gx, axis=1) + bx


def _scan_interleaved(a, u, h0):
    prods, sums = [], []
    p = s = None
    for i in range(SEG):
        ai, ui = _group(a, i), _group(u, i)
        p, s = (ai, ui) if i == 0 else (ai * p, ai * s + ui)
        prods.append(p)
        sums.append(s)
    c = h0
    starts = [c]
    for j in range(SUBLANES - 1):
        c = p[j:j + 1, :] * c + s[j:j + 1, :]
        starts.append(c)
    start = jnp.concatenate(starts, axis=0)
    return jnp.concatenate([sums[i] + prods[i] * start for i in range(SEG)], axis=0)


def _cumsum_interleaved(x):
    acc = []
    run = None
    for i in range(SEG):
        run = _group(x, i) if i == 0 else run + _group(x, i)
        acc.append(run)
    sub = lax.broadcasted_iota(jnp.int32, run.shape, 0)
    inc = run
    s = 1
    while s < SUBLANES:
        inc = inc + jnp.where(sub >= s, pltpu.roll(inc, s, 0), 0.0)
        s *= 2
    before = inc - run
    return jnp.concatenate([r + before for r in acc], axis=0)


def _conv_interleaved(carry, x_new, w, b):
    sub = lax.broadcasted_iota(jnp.int32, (SUBLANES, x_new.shape[1]), 0)
    head, new = [], []
    for r in range(CONV_W - 1):
        cur = _group(x_new, SEG - (CONV_W - 1) + r)
        head.append(jnp.where(sub == 0, carry[r:r + 1, :], pltpu.roll(cur, 1, 0)))
        new.append(cur[SUBLANES - 1:SUBLANES, :])
    ext = jnp.concatenate(head + [x_new], axis=0)
    out = b
    for j in range(CONV_W):
        out = out + ext[j * SUBLANES:j * SUBLANES + SEQ_TILE, :] * w[j:j + 1, :]
    return out, jnp.concatenate(new, axis=0)


def _time_of_row(r):
    return (r & (SUBLANES - 1)) * SEG + (r >> (SUBLANES.bit_length() - 1))


class _Plan:
    def __init__(self):
        self.tasks = {}

    def add(self, name, unit, cost, deps, fn):
        self.tasks[name] = (unit, cost, tuple(deps), fn)

    def order(self):
        succ = {n: [] for n in self.tasks}
        for n, (_, _, deps, _) in self.tasks.items():
            for p in deps:
                succ[p].append(n)
        tail = {}

        def path(n):
            if n not in tail:
                tail[n] = self.tasks[n][1] + max([path(s) for s in succ[n]], default=0)
            return tail[n]

        free = {"M": 0, "V": 0}
        done, order, left = {}, [], list(self.tasks)
        while left:
            ready = [n for n in left if all(p in done for p in self.tasks[n][2])]

            def start(n):
                unit, _, deps, _ = self.tasks[n]
                return max([free[unit]] + [done[p] for p in deps])

            n = min(ready, key=lambda n: (start(n), -path(n)))
            unit, cost, _, _ = self.tasks[n]
            st = start(n)
            done[n] = free[unit] = st + cost
            order.append((st, len(order), n))
            left.remove(n)
        return [n for _, _, n in sorted(order)]

    def run(self):
        for n in self.order():
            self.tasks[n][3]()


def _mixer_kernel(x_ref, perm_ref, permt_ref, gmix_ref, wbr_ref, wmg_ref, wif_ref, bif_ref, lcw_ref,
                  lcb_ref, wax_ref, ba_ref, bx_ref, lam_ref, mcw_ref, mcb_ref, wq_ref, wk_ref,
                  wv_ref, mg_ref, wout_ref,
                  x1_ref, lconv_ref, h_ref, mconv_ref, c_ref, n_ref, m_ref):
    tt = SEQ_TILE
    last = tt - 1

    @pl.when(pl.program_id(1) == 0)
    def _():
        lconv_ref[...] = jnp.zeros(lconv_ref.shape, F32)
        mconv_ref[...] = jnp.zeros(mconv_ref.shape, F32)
        h_ref[...] = jnp.zeros(h_ref.shape, F32)
        c_ref[...] = jnp.zeros(c_ref.shape, F32)
        n_ref[...] = jnp.zeros(n_ref.shape, F32)
        m_ref[...] = jnp.full(m_ref.shape, M_INIT, F32)

    lcw, lcb, mcw, mcb = lcw_ref[...], lcb_ref[...], mcw_ref[...], mcb_ref[...]
    ba, bx, lam, mg = ba_ref[...], bx_ref[...], lam_ref[...], mg_ref[...]

    v = {}
    for g in range(M_HEADS):
        gs = slice(g * M_HD, (g + 1) * M_HD)
        v["lconv", -1, g], v["mconv", -1, g] = lconv_ref[0, :, gs], mconv_ref[0, :, gs]
        v["h", -1, g] = h_ref[0, :, gs]
        v["c", -1, g], v["n", -1, g] = c_ref[0, g], n_ref[0, g:g + 1, :]
        v["m", -1, g] = m_ref[0, :, g:g + 1]
    plan = _Plan()

    for t in range(SUB_TILES):
        dep_prev = (lambda name, g, t=t: ((name, t - 1, g),)) if t else (lambda name, g: ())

        def t_norm(t=t):
            v["x", t] = x_ref[0, t * tt:(t + 1) * tt, :]
            v["xn_t", t] = _rms(v["x", t], gmix_ref[...]).astype(BF)
        plan.add(("norm", t), "V", 650, (), t_norm)

        def t_perm(t=t):
            v["xnb", t] = _dot(perm_ref[...], v["xn_t", t]).astype(BF)
        plan.add(("perm", t), "M", 260, (("norm", t),), t_perm)

        def proj(name, w_ref, col0, g, t=t):
            def run():
                v[name, t, g] = _dot(v["xnb", t], w_ref[:, col0 + g * M_HD:col0 + (g + 1) * M_HD])
            plan.add((name, t, g), "M", 260, (("perm", t),), run)

        def t_wif(t=t):
            v["pre", t] = _dot(v["xnb", t], wif_ref[...]) + bif_ref[...]
        plan.add(("wif", t), "M", 260, (("perm", t),), t_wif)

        def t_gcum(t=t):
            pre = v["pre", t]
            lane = lax.broadcasted_iota(jnp.int32, (tt, GATE_PAD), 1)
            v["gcol", t] = jnp.where(lane < M_HEADS, pre, _cumsum_interleaved(-_softplus(-pre)))
            v["grow", t] = v["gcol", t].T
            v["tri", t] = (_time_of_row(lax.broadcasted_iota(jnp.int32, (tt, 1), 0))
                           >= _time_of_row(lax.broadcasted_iota(jnp.int32, (1, tt), 1)))
        plan.add(("gcum", t), "V", 150, (("wif", t),), t_gcum)

        for g in range(M_HEADS):
            gs = slice(g * M_HD, (g + 1) * M_HD)
            for name, w_ref, col0 in (("xl", wbr_ref, 0), ("xm", wbr_ref, D_MODEL),
                                      ("om", wbr_ref, 2 * D_MODEL), ("gl", wmg_ref, 0),
                                      ("gm", wmg_ref, D_MODEL)):
                proj(name, w_ref, col0, g)

            def t_convl(t=t, g=g, gs=gs):
                v["xlc", t, g], v["lconv", t, g] = _conv_interleaved(
                    v["lconv", t - 1, g], v["xl", t, g], lcw[:, gs], lcb[:, gs])
            plan.add(("convl", t, g), "V", 135, (("xl", t, g),) + dep_prev("convl", g), t_convl)

            def t_gates(t=t, g=g, gs=gs):
                xlb = v["xlc", t, g].astype(BF)
                ga, gx = [], []
                for n in range(2):
                    gg = _dot(xlb[:, n * LRU_BLK:(n + 1) * LRU_BLK], wax_ref[2 * g + n])
                    ga.append(gg[:, :LRU_BLK])
                    gx.append(gg[:, LRU_BLK:])
                v["ga", t, g] = jnp.concatenate(ga, axis=1) + ba[:, gs]
                v["gx", t, g] = jnp.concatenate(gx, axis=1) + bx[:, gs]
            plan.add(("gates", t, g), "M", 130, (("convl", t, g),), t_gates)

            def t_coef(t=t, g=g, gs=gs):
                v["a", t, g], v["u", t, g] = _lru_coeffs(v["xlc", t, g], v["ga", t, g], v["gx", t, g],
                                                         lam[:, gs])
            plan.add(("coef", t, g), "V", 280, (("gates", t, g),), t_coef)

            def t_scan(t=t, g=g):
                v["yl", t, g] = _scan_interleaved(v["a", t, g], v["u", t, g], v["h", t - 1, g])
                v["h", t, g] = v["yl", t, g][last:last + 1, :]
            plan.add(("scan", t, g), "V", 120, (("coef", t, g),) + dep_prev("scan", g), t_scan)

            h = g

            def t_convm(t=t, h=h, gs=gs):
                xm_c, v["mconv", t, h] = _conv_interleaved(v["mconv", t - 1, h], v["xm", t, h],
                                                           mcw[:, gs], mcb[:, gs])
                v["xcb", t, h] = (xm_c * _sigmoid(xm_c)).astype(BF)
            plan.add(("convm", t, h), "V", 210, (("xm", t, h),) + dep_prev("convm", h), t_convm)

            def t_qkv(t=t, h=h):
                v["q", t, h] = _dot(v["xcb", t, h], wq_ref[h]) * (M_HD ** -0.5)
                v["k", t, h] = _dot(v["xcb", t, h], wk_ref[h])
                vv = _dot(v["xm", t, h].astype(BF), wv_ref[h])
                v["qb", t, h], v["kb", t, h], v["vb", t, h] = (
                    v["q", t, h].astype(BF), v["k", t, h].astype(BF), vv.astype(BF))
            plan.add(("qkv", t, h), "M", 200, (("convm", t, h),), t_qkv)

            def t_qk(t=t, h=h):
                v["qk", t, h] = _dot_nt(v["qb", t, h], v["kb", t, h])
            plan.add(("qk", t, h), "M", 64, (("qkv", t, h),), t_qk)

            def t_sp(t=t, h=h):
                gcol, grow = v["gcol", t], v["grow", t]
                b_col = gcol[:, M_HEADS + h:M_HEADS + h + 1]
                ig_row, b_row = grow[h:h + 1, :], grow[M_HEADS + h:M_HEADS + h + 1, :]
                dlog = jnp.where(v["tri", t], b_col - b_row + ig_row, MASKED)
                m_inter = b_col + v["m", t - 1, h]
                m_t = jnp.maximum(m_inter, jnp.max(dlog, axis=1, keepdims=True))
                s = v["qk", t, h] * jnp.exp(dlog - m_t)
                v["m_t", t, h], v["sc", t, h] = m_t, jnp.exp(m_inter - m_t)
                v["ssum", t, h] = jnp.sum(s, axis=1, keepdims=True)
                v["sb", t, h] = s.astype(BF)
                v["m", t, h] = m_t[last:last + 1, :]
            plan.add(("sp", t, h), "V", 200, (("qk", t, h), ("gcum", t)) + dep_prev("sp", h), t_sp)

            def t_sv(t=t, h=h):
                v["sv", t, h] = _dot(v["sb", t, h], v["vb", t, h])
                v["qc", t, h] = _dot(v["qb", t, h], v["c", t - 1, h].astype(BF))
            plan.add(("sv", t, h), "M", 130, (("sp", t, h),) + dep_prev("cnew", h), t_sv)

            def t_hn(t=t, h=h, gs=gs):
                sc, m_t = v["sc", t, h], v["m_t", t, h]
                num = v["sv", t, h] + sc * v["qc", t, h]
                den = v["ssum", t, h] + sc * jnp.sum(v["q", t, h] * v["n", t - 1, h], axis=1,
                                                     keepdims=True)
                hh = num / jnp.maximum(jnp.abs(den), jnp.exp(-m_t))
                v["hn", t, h] = (hh * lax.rsqrt(jnp.mean(hh * hh, axis=-1, keepdims=True) + EPS)
                                 * mg[:, gs])
            plan.add(("hn", t, h), "V", 170, (("sv", t, h),) + dep_prev("kw", h), t_hn)

            def t_kw(t=t, h=h):
                gcol = v["gcol", t]
                ig_col, b_col = gcol[:, h:h + 1], gcol[:, M_HEADS + h:M_HEADS + h + 1]
                m_last = v["m", t, h]
                b_last = b_col[last:last + 1, :]
                kw = v["k", t, h] * jnp.exp(b_last - b_col + ig_col - m_last)
                v["dec", t, h] = jnp.exp(b_last + v["m", t - 1, h] - m_last)
                v["n", t, h] = v["dec", t, h] * v["n", t - 1, h] + jnp.sum(kw, axis=0, keepdims=True)
                v["kwb", t, h] = kw.astype(BF)
            plan.add(("kw", t, h), "V", 80, (("sp", t, h),) + dep_prev("kw", h), t_kw)

            def t_ckv(t=t, h=h):
                v["ckv", t, h] = _dot_tn(v["kwb", t, h], v["vb", t, h])
            plan.add(("ckv", t, h), "M", 64, (("kw", t, h),), t_ckv)

            def t_cnew(t=t, h=h):
                v["c", t, h] = v["dec", t, h] * v["c", t - 1, h] + v["ckv", t, h]
            plan.add(("cnew", t, h), "V", 40, (("ckv", t, h),) + dep_prev("cnew", h), t_cnew)

            def t_sig(t=t, h=h):
                v["gate_l", t, h] = _sigmoid(v["gl", t, h]) * v["yl", t, h]
                v["gate_m", t, h] = _sigmoid(v["gm", t, h]) * _sigmoid(v["om", t, h])
            plan.add(("sig", t, h), "V", 180,
                     (("gl", t, h), ("gm", t, h), ("om", t, h), ("scan", t, h)), t_sig)

            def t_mrg(t=t, h=h):
                v["mrg", t, h] = (v["gate_l", t, h] + v["gate_m", t, h] * v["hn", t, h]).astype(BF)
            plan.add(("mrg", t, h), "V", 40, (("sig", t, h), ("hn", t, h)), t_mrg)

            def t_pt(t=t, h=h):
                v["mrg_t", t, h] = _dot(permt_ref[...], v["mrg", t, h]).astype(BF)
            plan.add(("pt", t, h), "M", 64, (("mrg", t, h),), t_pt)

            def t_wo(t=t, h=h, gs=gs):
                part = _dot(v["mrg_t", t, h], wout_ref[gs, :])
                v["out", t] = part if h == 0 else v["out", t] + part
            plan.add(("wo", t, h), "M", 260, (("pt", t, h),) + ((("wo", t, h - 1),) if h else ()),
                     t_wo)

        def t_fin(t=t):
            x1_ref[0, t * tt:(t + 1) * tt, :] = v["x", t] + v["out", t]
        plan.add(("fin", t), "V", 70, (("wo", t, M_HEADS - 1),), t_fin)

    def t_state():
        e = SUB_TILES - 1
        heads = range(M_HEADS)
        lconv_ref[0] = jnp.concatenate([v["lconv", e, g] for g in heads], axis=1)
        mconv_ref[0] = jnp.concatenate([v["mconv", e, g] for g in heads], axis=1)
        h_ref[0] = jnp.concatenate([v["h", e, g] for g in heads], axis=1)
        for h in heads:
            c_ref[0, h] = v["c", e, h]
        n_ref[0] = jnp.concatenate([v["n", e, h] for h in heads], axis=0)
        m_ref[0] = jnp.concatenate([v["m", e, h] for h in heads], axis=1)
    plan.add("state", "V", 60, tuple(("cnew", SUB_TILES - 1, h) for h in range(M_HEADS))
             + tuple(("scan", SUB_TILES - 1, h) for h in range(M_HEADS))
             + tuple(("fin", t) for t in range(SUB_TILES)), t_state)

    plan.run()


def _ffn_tail(x1, p, gffn_ref, wg_ref, wu_ref, wd_ref, gple_ref, wpg_ref, wple_ref, gfin_ref):
    xnb = _rms(x1, gffn_ref[...]).astype(BF)
    x2 = x1
    for c in range(D_FF // FF_CHUNK):
        cs = slice(c * FF_CHUNK, (c + 1) * FF_CHUNK)
        hg = _dot(xnb, wg_ref[:, cs])
        hu = _dot(xnb, wu_ref[:, cs])
        act = (hg * _sigmoid(hg) * hu).astype(BF)
        x2 = x2 + _dot(act, wd_ref[cs, :])
    gate = _sigmoid(_dot(_rms(x2, gple_ref[...]).astype(BF), wpg_ref[...]))
    x3 = x2 + gate * _dot(p.astype(BF), wple_ref[...])
    return _rms(x3, gfin_ref[...])


def _ffn_kernel(sc_ref, x1_ref, p_ref, q_ref, kw_ref, v_ref, c_ref, gffn_ref, wg_ref, wu_ref,
                wd_ref, gple_ref, wpg_ref, wple_ref, gfin_ref, y_ref, qc_out, c_out):
    tt = SEQ_TILE
    v = {}
    plan = _Plan()

    for t in range(FFN_SUB):
        rows = slice(t * tt, (t + 1) * tt)

        def t_norm(t=t, rows=rows):
            v["x", t] = x1_ref[rows, :]
            v["xnb", t] = _rms(v["x", t], gffn_ref[...]).astype(BF)
        plan.add(("norm", t), "V", 650, (), t_norm)

        for c in range(D_FF // FF_CHUNK):
            cs = slice(c * FF_CHUNK, (c + 1) * FF_CHUNK)

            def t_gu(t=t, c=c, cs=cs):
                v["hg", t, c] = _dot(v["xnb", t], wg_ref[:, cs])
                v["hu", t, c] = _dot(v["xnb", t], wu_ref[:, cs])
            plan.add(("gu", t, c), "M", 512, (("norm", t),), t_gu)

            def t_act(t=t, c=c):
                hg = v["hg", t, c]
                v["act", t, c] = (hg * _sigmoid(hg) * v["hu", t, c]).astype(BF)
            plan.add(("act", t, c), "V", 110, (("gu", t, c),), t_act)

            def t_dn(t=t, c=c, cs=cs):
                prev = v["x", t] if c == 0 else v["x2", t]
                v["x2", t] = prev + _dot(v["act", t, c], wd_ref[cs, :])
            plan.add(("dn", t, c), "M", 300, (("act", t, c),) + ((("dn", t, c - 1),) if c else ()),
                     t_dn)

        last_dn = ("dn", t, D_FF // FF_CHUNK - 1)

        def t_norm2(t=t):
            v["xgb", t] = _rms(v["x2", t], gple_ref[...]).astype(BF)
        plan.add(("norm2", t), "V", 650, (last_dn,), t_norm2)

        def t_ple(t=t, rows=rows):
            v["pe", t] = _dot(p_ref[rows, :].astype(BF), wple_ref[...])
        plan.add(("ple", t), "M", 300, (), t_ple)

        def t_wpg(t=t):
            v["gpre", t] = _dot(v["xgb", t], wpg_ref[...])
        plan.add(("wpg", t), "M", 1024, (("norm2", t),), t_wpg)

        def t_fin(t=t, rows=rows):
            x3 = v["x2", t] + _sigmoid(v["gpre", t]) * v["pe", t]
            y_ref[rows, :] = _rms(x3, gfin_ref[...])
        plan.add(("fin", t), "V", 1000, (("wpg", t), ("ple", t)), t_fin)

    base = pl.program_id(0) * STATE_BATCH
    heads = [slice(h * M_HD, (h + 1) * M_HD) for h in range(M_HEADS)]
    for bb in range(STATE_BATCH):
        def t_cols(bb=bb):
            row = pl.ds(base + bb, 1)
            q, kw = q_ref[row, :], kw_ref[row, :]
            v["cols", bb] = jnp.concatenate([q[:, hs] for hs in heads] + [kw[:, hs] for hs in heads],
                                            axis=0).T
            v["vrow", bb] = v_ref[row, :]
        plan.add(("cols", bb), "V", 40, (), t_cols)

        for h, hs in enumerate(heads):
            def t_pair(bb=bb, h=h, hs=hs):
                cols = v["cols", bb]
                q_col, kw_col = cols[:, h:h + 1], cols[:, M_HEADS + h:M_HEADS + h + 1]
                c_prev = c_ref[bb, h]
                qc_out[pl.ds(base + bb, 1), hs] = jnp.sum(q_col * c_prev, axis=0, keepdims=True)
                c_out[bb, h] = sc_ref[base + bb, h] * c_prev + kw_col * v["vrow", bb][:, hs]
            plan.add(("pair", bb, h), "V", 170, (("cols", bb),), t_pair)

    plan.run()


def _front_kernel(x_ref, lbuf_ref, h0_ref, mbuf_ref, gmix_ref, wbr_ref, wmg_ref, wif_ref, bif_ref,
                  lcw_ref, lcb_ref, wax_ref, ba_ref, bx_ref, lam_ref, mcw_ref, mcb_ref,
                  wq_ref, wk_ref, wv_ref, m0_ref, n0_ref,
                  q_ref, kw_ref, v_ref, sc_ref, hv_ref, hc_ref, n_out, m_out,
                  yl_ref, gm_ref, lbuf_out, h_out, mbuf_out):
    xnb = _rms(x_ref[...], gmix_ref[...]).astype(BF)

    def per_head(cols):
        return jnp.concatenate([jnp.broadcast_to(cols[:, h:h + 1], (cols.shape[0], M_HD))
                                for h in range(M_HEADS)], axis=1)

    def head_sums(a):
        return jnp.concatenate([jnp.sum(a[:, h * M_HD:(h + 1) * M_HD], axis=1, keepdims=True)
                                for h in range(M_HEADS)], axis=1)

    def conv_step(buf_ref, buf_out, x_new, w_ref, b_ref):
        out = b_ref[...] + x_new * w_ref[CONV_W - 1:CONV_W, :]
        for j in range(CONV_W - 1):
            out = out + buf_ref[j] * w_ref[j:j + 1, :]
        for j in range(CONV_W - 2):
            buf_out[j] = buf_ref[j + 1]
        buf_out[CONV_W - 2] = x_new
        return out

    x_l = _dot(xnb, wbr_ref[:, 0:D_MODEL])
    xl_c = conv_step(lbuf_ref, lbuf_out, x_l, lcw_ref, lcb_ref)
    ga, gx = _lru_gates(xl_c.astype(BF), wax_ref, ba_ref[...], bx_ref[...])
    a, u = _lru_coeffs(xl_c, ga, gx, lam_ref[...])
    y_l = a * h0_ref[...] + u
    h_out[...] = y_l
    g_l = _dot(xnb, wmg_ref[:, 0:D_MODEL])
    yl_ref[...] = _sigmoid(g_l) * y_l

    x_m = _dot(xnb, wbr_ref[:, D_MODEL:2 * D_MODEL])
    xm_c = conv_step(mbuf_ref, mbuf_out, x_m, mcw_ref, mcb_ref)
    xcb = (xm_c * _sigmoid(xm_c)).astype(BF)
    xmb = x_m.astype(BF)
    q, k, v = [], [], []
    for h in range(M_HEADS):
        hs = slice(h * M_HD, (h + 1) * M_HD)
        q.append(_dot(xcb[:, hs], wq_ref[h]) * (M_HD ** -0.5))
        k.append(_dot(xcb[:, hs], wk_ref[h]))
        v.append(_dot(xmb[:, hs], wv_ref[h]))
    q, k, v = (jnp.concatenate(a, axis=1) for a in (q, k, v))
    o_m = _dot(xnb, wbr_ref[:, 2 * D_MODEL:3 * D_MODEL])
    g_m = _dot(xnb, wmg_ref[:, D_MODEL:2 * D_MODEL])
    gm_ref[...] = _sigmoid(g_m) * _sigmoid(o_m)

    pre = _dot(xnb, wif_ref[...]) + bif_ref[...]
    ig = pre[:, 0:M_HEADS]
    lf = -_softplus(-pre[:, M_HEADS:N_GATES])
    n_prev = n0_ref[...]
    m_inter = lf + m0_ref[...]
    m_t = jnp.maximum(m_inter, ig)
    wk = jnp.exp(ig - m_t)
    sc = jnp.exp(m_inter - m_t)
    s = head_sums(q * k) * wk
    den = s + sc * head_sums(q * n_prev)
    rden = 1.0 / jnp.maximum(jnp.abs(den), jnp.exp(-m_t))
    wk_d, sc_d = per_head(wk), per_head(sc)
    hv_ref[...] = per_head(s * rden) * v
    hc_ref[...] = per_head(sc * rden)
    q_ref[...] = q
    kw_ref[...] = wk_d * k
    v_ref[...] = v
    sc_ref[...] = sc
    n_out[...] = sc_d * n_prev + wk_d * k
    m_out[...] = m_t


def _back_kernel(x_ref, qc_ref, hv_ref, hc_ref, yl_ref, gm_ref, p_ref, mg_ref, wout_ref, gffn_ref,
                 wg_ref, wu_ref, wd_ref, gple_ref, wpg_ref, wple_ref, gfin_ref, y_ref, mrg_ref):
    for h in range(M_HEADS):
        hs = slice(h * M_HD, (h + 1) * M_HD)
        hh = hv_ref[:, hs] + hc_ref[:, hs] * qc_ref[:, hs]
        hn = hh * lax.rsqrt(jnp.mean(hh * hh, axis=-1, keepdims=True) + EPS) * mg_ref[:, hs]
        mrg_ref[:, hs] = (yl_ref[:, hs] + gm_ref[:, hs] * hn).astype(BF)
    x1 = x_ref[...] + _dot(mrg_ref[...], wout_ref[...])
    y_ref[...] = _ffn_tail(x1, p_ref[...], gffn_ref, wg_ref, wu_ref, wd_ref, gple_ref, wpg_ref,
                           wple_ref, gfin_ref)


def _resident(shape):
    nd = len(shape)
    return pl.BlockSpec(shape, lambda *_: (0,) * nd, pipeline_mode=pl.Buffered(1))


def _params(n_axes):
    return pltpu.CompilerParams(dimension_semantics=("arbitrary",) * n_axes,
                                vmem_limit_bytes=VMEM_LIMIT)


def _interleave_matrix():
    r = np.arange(SEQ_TILE)
    perm = np.zeros((SEQ_TILE, SEQ_TILE), np.float32)
    perm[r, (r % SUBLANES) * SEG + r // SUBLANES] = 1.0
    return perm


def kernel(x_prompt, x_sample, state_lru_conv, state_lru_h, state_mlstm_conv, state_mlstm_C, state_mlstm_n, state_mlstm_m, p_prompt, p_sample, norm_mix_g, w_in, b_gates, lru_conv_w, lru_conv_b, lru_w_a, lru_b_a, lru_w_x, lru_b_x, lru_lambda, mlstm_conv_w, mlstm_conv_b, w_q, w_k, w_v, mlstm_norm_g, w_out, norm_ffn_g, w_ffn_gate, w_ffn_up, w_ffn_down, norm_ple_g, w_ple_gate, w_ple, final_norm_g):
    assert w_in.shape[0] == 1, "single-layer trunk"
    B, T, _ = x_prompt.shape
    S = x_sample.shape[0]
    step = SEQ_TILE * SUB_TILES
    assert T % step == 0 and (B * T) % FFN_TILE == 0 and x_sample.shape[1] == 1
    assert S == STATE_BATCH * ((B * T) // FFN_TILE), "one state batch per ffn grid step"

    g0 = 2 * D_MODEL + D_MODEL
    w0 = w_in[0]
    wbr = w0[:, :g0].astype(BF)
    wmg = w0[:, g0 + N_GATES:].astype(BF)
    wif = jnp.pad(w0[:, g0:g0 + N_GATES], ((0, 0), (0, GATE_PAD - N_GATES))).astype(BF)
    bif = jnp.pad(b_gates[0], (0, GATE_PAD - N_GATES)).reshape(1, GATE_PAD)
    wax = jnp.concatenate([lru_w_a[0], lru_w_x[0]], axis=2).astype(BF)
    row = lambda a: a.reshape(1, -1)
    gmix, ba, bx, lam = row(norm_mix_g[0]), row(lru_b_a[0]), row(lru_b_x[0]), row(lru_lambda[0])
    lcw, lcb = lru_conv_w[0], row(lru_conv_b[0])
    mcw, mcb = mlstm_conv_w[0], row(mlstm_conv_b[0])
    wq, wk, wv = w_q[0].astype(BF), w_k[0].astype(BF), w_v[0].astype(BF)
    mg = row(mlstm_norm_g[0])
    wout = w_out[0].astype(BF)
    gffn, gple, gfin = row(norm_ffn_g[0]), row(norm_ple_g[0]), row(final_norm_g)
    wg, wu, wd = w_ffn_gate[0].astype(BF), w_ffn_up[0].astype(BF), w_ffn_down[0].astype(BF)
    wpg, wple = w_ple_gate[0].astype(BF), w_ple[0].astype(BF)
    perm_np = _interleave_matrix()
    perm, perm_t = jnp.asarray(perm_np, BF), jnp.asarray(perm_np.T, BF)

    front_w = (gmix, wbr, wmg, wif, bif, lcw, lcb, wax, ba, bx, lam, mcw, mcb, wq, wk, wv)
    mixer_w = (perm, perm_t) + front_w + (mg, wout)
    ffn_w = (gffn, wg, wu, wd, gple, wpg, wple, gfin)

    nt = T // step
    sds = jax.ShapeDtypeStruct
    x1, p_lconv, p_h, p_mconv, p_c, p_n, p_m = pl.pallas_call(
        _mixer_kernel,
        grid=(B, nt),
        in_specs=[pl.BlockSpec((1, step, D_MODEL), lambda b, t: (b, t, 0))]
                 + [_resident(w.shape) for w in mixer_w],
        out_specs=[
            pl.BlockSpec((1, step, D_MODEL), lambda b, t: (b, t, 0)),
            pl.BlockSpec((1, CONV_W - 1, D_MODEL), lambda b, t: (b, 0, 0)),
            pl.BlockSpec((1, 1, D_MODEL), lambda b, t: (b, 0, 0)),
            pl.BlockSpec((1, CONV_W - 1, D_MODEL), lambda b, t: (b, 0, 0)),
            pl.BlockSpec((1, M_HEADS, M_HD, M_HD), lambda b, t: (b, 0, 0, 0)),
            pl.BlockSpec((1, M_HEADS, M_HD), lambda b, t: (b, 0, 0)),
            pl.BlockSpec((1, 1, M_HEADS), lambda b, t: (b, 0, 0)),
        ],
        out_shape=[
            sds((B, T, D_MODEL), F32),
            sds((B, CONV_W - 1, D_MODEL), F32),
            sds((B, 1, D_MODEL), F32),
            sds((B, CONV_W - 1, D_MODEL), F32),
            sds((B, M_HEADS, M_HD, M_HD), F32),
            sds((B, M_HEADS, M_HD), F32),
            sds((B, 1, M_HEADS), F32),
        ],
        compiler_params=_params(2),
        name="prompt_mixer",
    )(x_prompt, *mixer_w)

    xs = x_sample.reshape(S, D_MODEL)
    lbuf = jnp.swapaxes(state_lru_conv[0], 0, 1)
    mbuf = jnp.swapaxes(state_mlstm_conv[0], 0, 1)
    tok = sds((S, D_MODEL), F32)
    buf = sds((CONV_W - 1, S, D_MODEL), F32)
    per_head = sds((S, M_HEADS), F32)
    q, kw, v, sc, hv, hc, s_n, s_m, yl, gm, s_lbuf, s_h, s_mbuf = pl.pallas_call(
        _front_kernel,
        out_shape=[tok, tok, tok, per_head, tok, tok, tok, per_head, tok, tok, buf, tok, buf],
        compiler_params=pltpu.CompilerParams(vmem_limit_bytes=VMEM_LIMIT),
        name="sample_front",
    )(xs, lbuf, state_lru_h[0], mbuf, *front_w, state_mlstm_m[0],
      state_mlstm_n[0].reshape(S, D_MODEL))

    n_tok = B * T
    tile = lambda width: pl.BlockSpec((FFN_TILE, width), lambda i: (i, 0))
    c_spec = pl.BlockSpec((STATE_BATCH, M_HEADS, M_HD, M_HD), lambda i: (i, 0, 0, 0))
    y_prompt, qc, s_c = pl.pallas_call(
        _ffn_kernel,
        grid=(n_tok // FFN_TILE,),
        in_specs=[pl.BlockSpec(memory_space=pltpu.SMEM), tile(D_MODEL), tile(P_DIM),
                  _resident(tok.shape), _resident(tok.shape), _resident(tok.shape), c_spec]
                 + [_resident(w.shape) for w in ffn_w],
        out_specs=[tile(D_MODEL), pl.BlockSpec(tok.shape, lambda i: (0, 0)), c_spec],
        out_shape=[sds((n_tok, D_MODEL), F32), tok, sds((S, M_HEADS, M_HD, M_HD), F32)],
        compiler_params=_params(1),
        name="prompt_ffn",
    )(sc, x1.reshape(n_tok, D_MODEL), p_prompt[0].reshape(n_tok, P_DIM), q, kw, v,
      state_mlstm_C[0], *ffn_w)

    y_sample = pl.pallas_call(
        _back_kernel,
        out_shape=tok,
        scratch_shapes=[pltpu.VMEM((S, D_MODEL), BF)],
        compiler_params=pltpu.CompilerParams(vmem_limit_bytes=VMEM_LIMIT),
        name="sample_back",
    )(xs, qc, hv, hc, yl, gm, p_sample[0].reshape(S, P_DIM), mg, wout, *ffn_w)

    lead = lambda a: a[None]
    return (y_prompt.reshape(B, T, D_MODEL), y_sample.reshape(S, 1, D_MODEL),
            lead(p_lconv), p_h.reshape(1, B, D_MODEL), lead(p_mconv), lead(p_c), lead(p_n),
            p_m.reshape(1, B, M_HEADS),
            lead(jnp.swapaxes(s_lbuf, 0, 1)), lead(s_h), lead(jnp.swapaxes(s_mbuf, 0, 1)),
            lead(s_c), s_n.reshape(1, S, M_HEADS, M_HD), lead(s_m))
```

```python
import numpy as np

import jax
import jax.numpy as jnp
from jax import lax
from jax.experimental import pallas as pl
from jax.experimental.pallas import tpu as pltpu

D_MODEL = 1024
M_HEADS = 4
M_HD = D_MODEL // M_HEADS
LRU_BLOCKS = 8
LRU_BLK = D_MODEL // LRU_BLOCKS
LRU_C = 8.0
CONV_W = 4
D_FF = 2816
P_DIM = 256
EPS = 1e-6
M_INIT = -1e30
NEG_LOG2E = -1.4426950408889634
MASKED = -1e30
N_GATES = 2 * M_HEADS
GATE_PAD = 128
SUBLANES = 8
SEQ_TILE = 256
SEG = SEQ_TILE // SUBLANES
SUB_TILES = 2
FFN_TILE = 512
FFN_SUB = FFN_TILE // SEQ_TILE
FF_CHUNK = 256
STATE_BATCH = 4
PREP_ROWS = 128
VMEM_LIMIT = 56 * 1024 * 1024

BF = jnp.bfloat16
F32 = jnp.float32


def _dot(a, b):
    return jnp.dot(a, b, preferred_element_type=F32)


def _dot_nt(a, b):
    return lax.dot_general(a, b, (((1,), (1,)), ((), ())), preferred_element_type=F32)


def _dot_tn(a, b):
    return lax.dot_general(a, b, (((0,), (0,)), ((), ())), preferred_element_type=F32)


def _sigmoid(x):
    return 1.0 / (1.0 + jnp.exp2(x * NEG_LOG2E))


def _softplus(x):
    return jnp.maximum(x, 0.0) + jnp.log1p(jnp.exp(-jnp.abs(x)))


def _rms(x, g):
    return x * lax.rsqrt(jnp.mean(x * x, axis=-1, keepdims=True) + EPS) * g


def _group(x, i):
    return x[i * SUBLANES:(i + 1) * SUBLANES, :]


def _lru_coeffs(xc, ga, gx, lam):
    r = _sigmoid(ga)
    ig = _sigmoid(gx)
    log_a = -LRU_C * r * _softplus(-lam)
    a = jnp.exp(log_a)
    om = 1.0 - a * a
    root = jnp.where(om > 0.0, om * lax.rsqrt(om), 0.0)
    return a, root * ig * xc


def _lru_gates(xcb, wax_ref, ba, bx):
    ga, gx = [], []
    for n in range(LRU_BLOCKS):
        g = _dot(xcb[:, n * LRU_BLK:(n + 1) * LRU_BLK], wax_ref[n])
        ga.append(g[:, :LRU_BLK])
        gx.append(g[:, LRU_BLK:])
    return jnp.concatenate(ga, axis=1) + ba, jnp.concatenate(gx, axis=1) + bx


def _scan_interleaved(a, u, h0):
    prods, sums = [], []
    p = s = None
    for i in range(SEG):
        ai, ui = _group(a, i), _group(u, i)
        p, s = (ai, ui) if i == 0 else (ai * p, ai * s + ui)
        prods.append(p)
        sums.append(s)
    c = h0
    starts = [c]
    for j in range(SUBLANES - 1):
        c = p[j:j + 1, :] * c + s[j:j + 1, :]
        starts.append(c)
    start = jnp.concatenate(starts, axis=0)
    return jnp.concatenate([sums[i] + prods[i] * start for i in range(SEG)], axis=0)


def _cumsum_interleaved(x):
    acc = []
    run = None
    for i in range(SEG):
        run = _group(x, i) if i == 0 else run + _group(x, i)
        acc.append(run)
    sub = lax.broadcasted_iota(jnp.int32, run.shape, 0)
    inc = run
    s = 1
    while s < SUBLANES:
        inc = inc + jnp.where(sub >= s, pltpu.roll(inc, s, 0), 0.0)
        s *= 2
    before = inc - run
    return jnp.concatenate([r + before for r in acc], axis=0)


def _conv_interleaved(carry, x_new, w, b):
    sub = lax.broadcasted_iota(jnp.int32, (SUBLANES, x_new.shape[1]), 0)
    head, new = [], []
    for r in range(CONV_W - 1):
        cur = _group(x_new, SEG - (CONV_W - 1) + r)
        head.append(jnp.where(sub == 0, carry[r:r + 1, :], pltpu.roll(cur, 1, 0)))
        new.append(cur[SUBLANES - 1:SUBLANES, :])
    ext = jnp.concatenate(head + [x_new], axis=0)
    out = b
    for j in range(CONV_W):
        out = out + ext[j * SUBLANES:j * SUBLANES + SEQ_TILE, :] * w[j:j + 1, :]
    return out, jnp.concatenate(new, axis=0)


def _time_of_row(r):
    return (r & (SUBLANES - 1)) * SEG + (r >> (SUBLANES.bit_length() - 1))


class _Plan:
    def __init__(self):
        self.tasks = {}

    def add(self, name, unit, cost, deps, fn):
        self.tasks[name] = (unit, cost, tuple(deps), fn)

    def order(self):
        succ = {n: [] for n in self.tasks}
        for n, (_, _, deps, _) in self.tasks.items():
            for p in deps:
                succ[p].append(n)
        tail = {}

        def path(n):
            if n not in tail:
                tail[n] = self.tasks[n][1] + max([path(s) for s in succ[n]], default=0)
            return tail[n]

        free = {"M": 0, "V": 0}
        done, order, left = {}, [], list(self.tasks)
        while left:
            ready = [n for n in left if all(p in done for p in self.tasks[n][2])]

            def start(n):
                unit, _, deps, _ = self.tasks[n]
                return max([free[unit]] + [done[p] for p in deps])

            n = min(ready, key=lambda n: (start(n), -path(n)))
            unit, cost, _, _ = self.tasks[n]
            st = start(n)
            done[n] = free[unit] = st + cost
            order.append((st, len(order), n))
            left.remove(n)
        return [n for _, _, n in sorted(order)]

    def run(self):
        for n in self.order():
            self.tasks[n][3]()


def _mixer_kernel(x_ref, perm_ref, permt_ref, gmix_ref, wbr_ref, wmg_ref, wif_ref, bif_ref, lcw_ref,
                  lcb_ref, wax_ref, ba_ref, bx_ref, lam_ref, mcw_ref, mcb_ref, wq_ref, wk_ref,
                  wv_ref, mg_ref, wout_ref,
                  x1_ref, lconv_ref, h_ref, mconv_ref, c_ref, n_ref, m_ref):
    tt = SEQ_TILE
    last = tt - 1

    @pl.when(pl.program_id(1) == 0)
    def _():
        lconv_ref[...] = jnp.zeros(lconv_ref.shape, F32)
        mconv_ref[...] = jnp.zeros(mconv_ref.shape, F32)
        h_ref[...] = jnp.zeros(h_ref.shape, F32)
        c_ref[...] = jnp.zeros(c_ref.shape, F32)
        n_ref[...] = jnp.zeros(n_ref.shape, F32)
        m_ref[...] = jnp.full(m_ref.shape, M_INIT, F32)

    lcw, lcb, mcw, mcb = lcw_ref[...], lcb_ref[...], mcw_ref[...], mcb_ref[...]
    ba, bx, lam, mg = ba_ref[...], bx_ref[...], lam_ref[...], mg_ref[...]

    v = {}
    for g in range(M_HEADS):
        gs = slice(g * M_HD, (g + 1) * M_HD)
        v["lconv", -1, g], v["mconv", -1, g] = lconv_ref[0, :, gs], mconv_ref[0, :, gs]
        v["h", -1, g] = h_ref[0, :, gs]
        v["c", -1, g], v["n", -1, g] = c_ref[0, g], n_ref[0, g:g + 1, :]
        v["m", -1, g] = m_ref[0, :, g:g + 1]
    plan = _Plan()

    for t in range(SUB_TILES):
        dep_prev = (lambda name, g, t=t: ((name, t - 1, g),)) if t else (lambda name, g: ())

        def t_norm(t=t):
            v["x", t] = x_ref[0, t * tt:(t + 1) * tt, :]
            v["xn_t", t] = _rms(v["x", t], gmix_ref[...]).astype(BF)
        plan.add(("norm", t), "V", 650, (), t_norm)

        def t_perm(t=t):
            v["xnb", t] = _dot(perm_ref[...], v["xn_t", t]).astype(BF)
        plan.add(("perm", t), "M", 260, (("norm", t),), t_perm)

        def proj(name, w_ref, col0, g, t=t):
            def run():
                v[name, t, g] = _dot(v["xnb", t], w_ref[:, col0 + g * M_HD:col0 + (g + 1) * M_HD])
            plan.add((name, t, g), "M", 260, (("perm", t),), run)

        def t_wif(t=t):
            v["pre", t] = _dot(v["xnb", t], wif_ref[...]) + bif_ref[...]
        plan.add(("wif", t), "M", 260, (("perm", t),), t_wif)

        def t_gcum(t=t):
            pre = v["pre", t]
            lane = lax.broadcasted_iota(jnp.int32, (tt, GATE_PAD), 1)
            v["gcol", t] = jnp.where(lane < M_HEADS, pre, _cumsum_interleaved(-_softplus(-pre)))
            v["grow", t] = v["gcol", t].T
            v["tri", t] = (_time_of_row(lax.broadcasted_iota(jnp.int32, (tt, 1), 0))
                           >= _time_of_row(lax.broadcasted_iota(jnp.int32, (1, tt), 1)))
        plan.add(("gcum", t), "V", 150, (("wif", t),), t_gcum)

        for g in range(M_HEADS):
            gs = slice(g * M_HD, (g + 1) * M_HD)
            for name, w_ref, col0 in (("xl", wbr_ref, 0), ("xm", wbr_ref, D_MODEL),
                                      ("om", wbr_ref, 2 * D_MODEL), ("gl", wmg_ref, 0),
                                      ("gm", wmg_ref, D_MODEL)):
                proj(name, w_ref, col0, g)

            def t_convl(t=t, g=g, gs=gs):
                v["xlc", t, g], v["lconv", t, g] = _conv_interleaved(
                    v["lconv", t - 1, g], v["xl", t, g], lcw[:, gs], lcb[:, gs])
            plan.add(("convl", t, g), "V", 135, (("xl", t, g),) + dep_prev("convl", g), t_convl)

            def t_gates(t=t, g=g, gs=gs):
                xlb = v["xlc", t, g].astype(BF)
                ga, gx = [], []
                for n in range(2):
                    gg = _dot(xlb[:, n * LRU_BLK:(n + 1) * LRU_BLK], wax_ref[2 * g + n])
                    ga.append(gg[:, :LRU_BLK])
                    gx.append(gg[:, LRU_BLK:])
                v["ga", t, g] = jnp.concatenate(ga, axis=1) + ba[:, gs]
                v["gx", t, g] = jnp.concatenate(gx, axis=1) + bx[:, gs]
            plan.add(("gates", t, g), "M", 130, (("convl", t, g),), t_gates)

            def t_coef(t=t, g=g, gs=gs):
                v["a", t, g], v["u", t, g] = _lru_coeffs(v["xlc", t, g], v["ga", t, g], v["gx", t, g],
                                                         lam[:, gs])
            plan.add(("coef", t, g), "V", 280, (("gates", t, g),), t_coef)

            def t_scan(t=t, g=g):
                v["yl", t, g] = _scan_interleaved(v["a", t, g], v["u", t, g], v["h", t - 1, g])
                v["h", t, g] = v["yl", t, g][last:last + 1, :]
            plan.add(("scan", t, g), "V", 120, (("coef", t, g),) + dep_prev("scan", g), t_scan)

            h = g

            def t_convm(t=t, h=h, gs=gs):
                xm_c, v["mconv", t, h] = _conv_interleaved(v["mconv", t - 1, h], v["xm", t, h],
                                                           mcw[:, gs], mcb[:, gs])
                v["xcb", t, h] = (xm_c * _sigmoid(xm_c)).astype(BF)
            plan.add(("convm", t, h), "V", 210, (("xm", t, h),) + dep_prev("convm", h), t_convm)

            def t_qkv(t=t, h=h):
                v["q", t, h] = _dot(v["xcb", t, h], wq_ref[h]) * (M_HD ** -0.5)
                v["k", t, h] = _dot(v["xcb", t, h], wk_ref[h])
                vv = _dot(v["xm", t, h].astype(BF), wv_ref[h])
                v["qb", t, h], v["kb", t, h], v["vb", t, h] = (
                    v["q", t, h].astype(BF), v["k", t, h].astype(BF), vv.astype(BF))
            plan.add(("qkv", t, h), "M", 200, (("convm", t, h),), t_qkv)

            def t_qk(t=t, h=h):
                v["qk", t, h] = _dot_nt(v["qb", t, h], v["kb", t, h])
            plan.add(("qk", t, h), "M", 64, (("qkv", t, h),), t_qk)

            def t_sp(t=t, h=h):
                gcol, grow = v["gcol", t], v["grow", t]
                b_col = gcol[:, M_HEADS + h:M_HEADS + h + 1]
                ig_row, b_row = grow[h:h + 1, :], grow[M_HEADS + h:M_HEADS + h + 1, :]
                dlog = jnp.where(v["tri", t], b_col - b_row + ig_row, MASKED)
                m_inter = b_col + v["m", t - 1, h]
                m_t = jnp.maximum(m_inter, jnp.max(dlog, axis=1, keepdims=True))
                s = v["qk", t, h] * jnp.exp(dlog - m_t)
                v["m_t", t, h], v["sc", t, h] = m_t, jnp.exp(m_inter - m_t)
                v["ssum", t, h] = jnp.sum(s, axis=1, keepdims=True)
                v["sb", t, h] = s.astype(BF)
                v["m", t, h] = m_t[last:last + 1, :]
            plan.add(("sp", t, h), "V", 200, (("qk", t, h), ("gcum", t)) + dep_prev("sp", h), t_sp)

            def t_sv(t=t, h=h):
                v["sv", t, h] = _dot(v["sb", t, h], v["vb", t, h])
                v["qc", t, h] = _dot(v["qb", t, h], v["c", t - 1, h].astype(BF))
            plan.add(("sv", t, h), "M", 130, (("sp", t, h),) + dep_prev("cnew", h), t_sv)

            def t_hn(t=t, h=h, gs=gs):
                sc, m_t = v["sc", t, h], v["m_t", t, h]
                num = v["sv", t, h] + sc * v["qc", t, h]
                den = v["ssum", t, h] + sc * jnp.sum(v["q", t, h] * v["n", t - 1, h], axis=1,
                                                     keepdims=True)
                hh = num / jnp.maximum(jnp.abs(den), jnp.exp(-m_t))
                v["hn", t, h] = (hh * lax.rsqrt(jnp.mean(hh * hh, axis=-1, keepdims=True) + EPS)
                                 * mg[:, gs])
            plan.add(("hn", t, h), "V", 170, (("sv", t, h),) + dep_prev("kw", h), t_hn)

            def t_kw(t=t, h=h):
                gcol = v["gcol", t]
                ig_col, b_col = gcol[:, h:h + 1], gcol[:, M_HEADS + h:M_HEADS + h + 1]
                m_last = v["m", t, h]
                b_last = b_col[last:last + 1, :]
                kw = v["k", t, h] * jnp.exp(b_last - b_col + ig_col - m_last)
                v["dec", t, h] = jnp.exp(b_last + v["m", t - 1, h] - m_last)
                v["n", t, h] = v["dec", t, h] * v["n", t - 1, h] + jnp.sum(kw, axis=0, keepdims=True)
                v["kwb", t, h] = kw.astype(BF)
            plan.add(("kw", t, h), "V", 80, (("sp", t, h),) + dep_prev("kw", h), t_kw)

            def t_ckv(t=t, h=h):
                v["ckv", t, h] = _dot_tn(v["kwb", t, h], v["vb", t, h])
            plan.add(("ckv", t, h), "M", 64, (("kw", t, h),), t_ckv)

            def t_cnew(t=t, h=h):
                v["c", t, h] = v["dec", t, h] * v["c", t - 1, h] + v["ckv", t, h]
            plan.add(("cnew", t, h), "V", 40, (("ckv", t, h),) + dep_prev("cnew", h), t_cnew)

            def t_sig(t=t, h=h):
                v["gate_l", t, h] = _sigmoid(v["gl", t, h]) * v["yl", t, h]
                v["gate_m", t, h] = _sigmoid(v["gm", t, h]) * _sigmoid(v["om", t, h])
            plan.add(("sig", t, h), "V", 180,
                     (("gl", t, h), ("gm", t, h), ("om", t, h), ("scan", t, h)), t_sig)

            def t_mrg(t=t, h=h):
                v["mrg", t, h] = (v["gate_l", t, h] + v["gate_m", t, h] * v["hn", t, h]).astype(BF)
            plan.add(("mrg", t, h), "V", 40, (("sig", t, h), ("hn", t, h)), t_mrg)

            def t_pt(t=t, h=h):
                v["mrg_t", t, h] = _dot(permt_ref[...], v["mrg", t, h]).astype(BF)
            plan.add(("pt", t, h), "M", 64, (("mrg", t, h),), t_pt)

            def t_wo(t=t, h=h, gs=gs):
                part = _dot(v["mrg_t", t, h], wout_ref[gs, :])
                v["out", t] = part if h == 0 else v["out", t] + part
            plan.add(("wo", t, h), "M", 260, (("pt", t, h),) + ((("wo", t, h - 1),) if h else ()),
                     t_wo)

        def t_fin(t=t):
            x1_ref[0, t * tt:(t + 1) * tt, :] = v["x", t] + v["out", t]
        plan.add(("fin", t), "V", 70, (("wo", t, M_HEADS - 1),), t_fin)

    def t_state():
        e = SUB_TILES - 1
        heads = range(M_HEADS)
        lconv_ref[0] = jnp.concatenate([v["lconv", e, g] for g in heads], axis=1)
        mconv_ref[0] = jnp.concatenate([v["mconv", e, g] for g in heads], axis=1)
        h_ref[0] = jnp.concatenate([v["h", e, g] for g in heads], axis=1)
        for h in heads:
            c_ref[0, h] = v["c", e, h]
        n_ref[0] = jnp.concatenate([v["n", e, h] for h in heads], axis=0)
        m_ref[0] = jnp.concatenate([v["m", e, h] for h in heads], axis=1)
    plan.add("state", "V", 60, tuple(("cnew", SUB_TILES - 1, h) for h in range(M_HEADS))
             + tuple(("scan", SUB_TILES - 1, h) for h in range(M_HEADS))
             + tuple(("fin", t) for t in range(SUB_TILES)), t_state)

    plan.run()


def _split_kernel(w_ref, wbr_ref, wmg_ref, wif_ref):
    g0 = 3 * D_MODEL
    wbr_ref[...] = w_ref[:, 0:g0].astype(BF)
    wmg_ref[...] = w_ref[:, g0 + N_GATES:g0 + N_GATES + 2 * D_MODEL].astype(BF)
    gates = w_ref[:, g0:g0 + GATE_PAD]
    lane = lax.broadcasted_iota(jnp.int32, gates.shape, 1)
    wif_ref[...] = jnp.where(lane < N_GATES, gates, 0.0).astype(BF)


def _ffn_tail(x1, p, gffn_ref, wg_ref, wu_ref, wd_ref, gple_ref, wpg_ref, wple_ref, gfin_ref):
    xnb = _rms(x1, gffn_ref[...]).astype(BF)
    x2 = x1
    for c in range(D_FF // FF_CHUNK):
        cs = slice(c * FF_CHUNK, (c + 1) * FF_CHUNK)
        hg = _dot(xnb, wg_ref[:, cs])
        hu = _dot(xnb, wu_ref[:, cs])
        act = (hg * _sigmoid(hg) * hu).astype(BF)
        x2 = x2 + _dot(act, wd_ref[cs, :])
    gate = _sigmoid(_dot(_rms(x2, gple_ref[...]).astype(BF), wpg_ref[...]))
    x3 = x2 + gate * _dot(p.astype(BF), wple_ref[...])
    return _rms(x3, gfin_ref[...])


def _ffn_kernel(sc_ref, x1_ref, p_ref, q_ref, kw_ref, v_ref, c_ref, gffn_ref, wg_ref, wu_ref,
                wd_ref, gple_ref, wpg_ref, wple_ref, gfin_ref, y_ref, qc_out, c_out):
    tt = SEQ_TILE
    v = {}
    plan = _Plan()

    for t in range(FFN_SUB):
        rows = slice(t * tt, (t + 1) * tt)

        def t_norm(t=t, rows=rows):
            v["x", t] = x1_ref[rows, :]
            v["xnb", t] = _rms(v["x", t], gffn_ref[...]).astype(BF)
        plan.add(("norm", t), "V", 650, (), t_norm)

        for c in range(D_FF // FF_CHUNK):
            cs = slice(c * FF_CHUNK, (c + 1) * FF_CHUNK)

            def t_gu(t=t, c=c, cs=cs):
                v["hg", t, c] = _dot(v["xnb", t], wg_ref[:, cs])
                v["hu", t, c] = _dot(v["xnb", t], wu_ref[:, cs])
            plan.add(("gu", t, c), "M", 512, (("norm", t),), t_gu)

            def t_act(t=t, c=c):
                hg = v["hg", t, c]
                v["act", t, c] = (hg * _sigmoid(hg) * v["hu", t, c]).astype(BF)
            plan.add(("act", t, c), "V", 110, (("gu", t, c),), t_act)

            def t_dn(t=t, c=c, cs=cs):
                prev = v["x", t] if c == 0 else v["x2", t]
                v["x2", t] = prev + _dot(v["act", t, c], wd_ref[cs, :])
            plan.add(("dn", t, c), "M", 300, (("act", t, c),) + ((("dn", t, c - 1),) if c else ()),
                     t_dn)

        last_dn = ("dn", t, D_FF // FF_CHUNK - 1)

        def t_norm2(t=t):
            v["xgb", t] = _rms(v["x2", t], gple_ref[...]).astype(BF)
        plan.add(("norm2", t), "V", 650, (last_dn,), t_norm2)

        def t_ple(t=t, rows=rows):
            v["pe", t] = _dot(p_ref[rows, :].astype(BF), wple_ref[...])
        plan.add(("ple", t), "M", 300, (), t_ple)

        def t_wpg(t=t):
            v["gpre", t] = _dot(v["xgb", t], wpg_ref[...])
        plan.add(("wpg", t), "M", 1024, (("norm2", t),), t_wpg)

        def t_fin(t=t, rows=rows):
            x3 = v["x2", t] + _sigmoid(v["gpre", t]) * v["pe", t]
            y_ref[rows, :] = _rms(x3, gfin_ref[...])
        plan.add(("fin", t), "V", 1000, (("wpg", t), ("ple", t)), t_fin)

    base = pl.program_id(0) * STATE_BATCH
    heads = [slice(h * M_HD, (h + 1) * M_HD) for h in range(M_HEADS)]
    for bb in range(STATE_BATCH):
        def t_cols(bb=bb):
            row = pl.ds(base + bb, 1)
            q, kw = q_ref[row, :], kw_ref[row, :]
            v["cols", bb] = jnp.concatenate([q[:, hs] for hs in heads] + [kw[:, hs] for hs in heads],
                                            axis=0).T
            v["vrow", bb] = v_ref[row, :]
        plan.add(("cols", bb), "V", 40, (), t_cols)

        for h, hs in enumerate(heads):
            def t_pair(bb=bb, h=h, hs=hs):
                cols = v["cols", bb]
                q_col, kw_col = cols[:, h:h + 1], cols[:, M_HEADS + h:M_HEADS + h + 1]
                c_prev = c_ref[bb, h]
                qc_out[pl.ds(base + bb, 1), hs] = jnp.sum(q_col * c_prev, axis=0, keepdims=True)
                c_out[bb, h] = sc_ref[base + bb, h] * c_prev + kw_col * v["vrow", bb][:, hs]
            plan.add(("pair", bb, h), "V", 170, (("cols", bb),), t_pair)

    plan.run()


def _front_kernel(x_ref, lbuf_ref, h0_ref, mbuf_ref, gmix_ref, wbr_ref, wmg_ref, wif_ref, bif_ref,
                  lcw_ref, lcb_ref, wax_ref, ba_ref, bx_ref, lam_ref, mcw_ref, mcb_ref,
                  wq_ref, wk_ref, wv_ref, m0_ref, n0_ref,
                  q_ref, kw_ref, v_ref, sc_ref, hv_ref, hc_ref, n_out, m_out,
                  yl_ref, gm_ref, lbuf_out, h_out, mbuf_out):
    xnb = _rms(x_ref[...], gmix_ref[...]).astype(BF)

    def per_head(cols):
        return jnp.concatenate([jnp.broadcast_to(cols[:, h:h + 1], (cols.shape[0], M_HD))
                                for h in range(M_HEADS)], axis=1)

    def head_sums(a):
        return jnp.concatenate([jnp.sum(a[:, h * M_HD:(h + 1) * M_HD], axis=1, keepdims=True)
                                for h in range(M_HEADS)], axis=1)

    def conv_step(buf_ref, buf_out, x_new, w_ref, b_ref):
        out = b_ref[...] + x_new * w_ref[CONV_W - 1:CONV_W, :]
        for j in range(CONV_W - 1):
            out = out + buf_ref[j] * w_ref[j:j + 1, :]
        for j in range(CONV_W - 2):
            buf_out[j] = buf_ref[j + 1]
        buf_out[CONV_W - 2] = x_new
        return out

    x_l = _dot(xnb, wbr_ref[:, 0:D_MODEL])
    xl_c = conv_step(lbuf_ref, lbuf_out, x_l, lcw_ref, lcb_ref)
    ga, gx = _lru_gates(xl_c.astype(BF), wax_ref, ba_ref[...], bx_ref[...])
    a, u = _lru_coeffs(xl_c, ga, gx, lam_ref[...])
    y_l = a * h0_ref[...] + u
    h_out[...] = y_l
    g_l = _dot(xnb, wmg_ref[:, 0:D_MODEL])
    yl_ref[...] = _sigmoid(g_l) * y_l

    x_m = _dot(xnb, wbr_ref[:, D_MODEL:2 * D_MODEL])
    xm_c = conv_step(mbuf_ref, mbuf_out, x_m, mcw_ref, mcb_ref)
    xcb = (xm_c * _sigmoid(xm_c)).astype(BF)
    xmb = x_m.astype(BF)
    q, k, v = [], [], []
    for h in range(M_HEADS):
        hs = slice(h * M_HD, (h + 1) * M_HD)
        q.append(_dot(xcb[:, hs], wq_ref[h]) * (M_HD ** -0.5))
        k.append(_dot(xcb[:, hs], wk_ref[h]))
        v.append(_dot(xmb[:, hs], wv_ref[h]))
    q, k, v = (jnp.concatenate(a, axis=1) for a in (q, k, v))
    o_m = _dot(xnb, wbr_ref[:, 2 * D_MODEL:3 * D_MODEL])
    g_m = _dot(xnb, wmg_ref[:, D_MODEL:2 * D_MODEL])
    gm_ref[...] = _sigmoid(g_m) * _sigmoid(o_m)

    pre = _dot(xnb, wif_ref[...]) + bif_ref[...]
    ig = pre[:, 0:M_HEADS]
    lf = -_softplus(-pre[:, M_HEADS:N_GATES])
    n_prev = n0_ref[...]
    m_inter = lf + m0_ref[...]
    m_t = jnp.maximum(m_inter, ig)
    wk = jnp.exp(ig - m_t)
    sc = jnp.exp(m_inter - m_t)
    s = head_sums(q * k) * wk
    den = s + sc * head_sums(q * n_prev)
    rden = 1.0 / jnp.maximum(jnp.abs(den), jnp.exp(-m_t))
    wk_d, sc_d = per_head(wk), per_head(sc)
    hv_ref[...] = per_head(s * rden) * v
    hc_ref[...] = per_head(sc * rden)
    q_ref[...] = q
    kw_ref[...] = wk_d * k
    v_ref[...] = v
    sc_ref[...] = sc
    n_out[...] = sc_d * n_prev + wk_d * k
    m_out[...] = m_t


def _back_kernel(x_ref, qc_ref, hv_ref, hc_ref, yl_ref, gm_ref, p_ref, mg_ref, wout_ref, gffn_ref,
                 wg_ref, wu_ref, wd_ref, gple_ref, wpg_ref, wple_ref, gfin_ref, y_ref, mrg_ref):
    for h in range(M_HEADS):
        hs = slice(h * M_HD, (h + 1) * M_HD)
        hh = hv_ref[:, hs] + hc_ref[:, hs] * qc_ref[:, hs]
        hn = hh * lax.rsqrt(jnp.mean(hh * hh, axis=-1, keepdims=True) + EPS) * mg_ref[:, hs]
        mrg_ref[:, hs] = (yl_ref[:, hs] + gm_ref[:, hs] * hn).astype(BF)
    x1 = x_ref[...] + _dot(mrg_ref[...], wout_ref[...])
    y_ref[...] = _ffn_tail(x1, p_ref[...], gffn_ref, wg_ref, wu_ref, wd_ref, gple_ref, wpg_ref,
                           wple_ref, gfin_ref)


def _resident(shape):
    nd = len(shape)
    return pl.BlockSpec(shape, lambda *_: (0,) * nd, pipeline_mode=pl.Buffered(1))


def _params(n_axes):
    return pltpu.CompilerParams(dimension_semantics=("arbitrary",) * n_axes,
                                vmem_limit_bytes=VMEM_LIMIT)


def _interleave_matrix():
    r = np.arange(SEQ_TILE)
    perm = np.zeros((SEQ_TILE, SEQ_TILE), np.float32)
    perm[r, (r % SUBLANES) * SEG + r // SUBLANES] = 1.0
    return perm


def kernel(x_prompt, x_sample, state_lru_conv, state_lru_h, state_mlstm_conv, state_mlstm_C, state_mlstm_n, state_mlstm_m, p_prompt, p_sample, norm_mix_g, w_in, b_gates, lru_conv_w, lru_conv_b, lru_w_a, lru_b_a, lru_w_x, lru_b_x, lru_lambda, mlstm_conv_w, mlstm_conv_b, w_q, w_k, w_v, mlstm_norm_g, w_out, norm_ffn_g, w_ffn_gate, w_ffn_up, w_ffn_down, norm_ple_g, w_ple_gate, w_ple, final_norm_g):
    assert w_in.shape[0] == 1, "single-layer trunk"
    B, T, _ = x_prompt.shape
    S = x_sample.shape[0]
    step = SEQ_TILE * SUB_TILES
    assert T % step == 0 and (B * T) % FFN_TILE == 0 and x_sample.shape[1] == 1
    assert S == STATE_BATCH * ((B * T) // FFN_TILE), "one state batch per ffn grid step"

    sds = jax.ShapeDtypeStruct
    w0 = w_in[0]
    assert w0.shape == (D_MODEL, 5 * D_MODEL + N_GATES)
    piece = lambda width: pl.BlockSpec((PREP_ROWS, width), lambda i: (i, 0))
    wbr, wmg, wif = pl.pallas_call(
        _split_kernel,
        grid=(D_MODEL // PREP_ROWS,),
        in_specs=[piece(w0.shape[1])],
        out_specs=[piece(3 * D_MODEL), piece(2 * D_MODEL), piece(GATE_PAD)],
        out_shape=[sds((D_MODEL, 3 * D_MODEL), BF), sds((D_MODEL, 2 * D_MODEL), BF),
                   sds((D_MODEL, GATE_PAD), BF)],
        compiler_params=_params(1),
        name="split_w_in",
    )(w0)
    bif = jnp.pad(b_gates[0], (0, GATE_PAD - N_GATES)).reshape(1, GATE_PAD)
    wax = jnp.concatenate([lru_w_a[0], lru_w_x[0]], axis=2).astype(BF)
    row = lambda a: a.reshape(1, -1)
    gmix, ba, bx, lam = row(norm_mix_g[0]), row(lru_b_a[0]), row(lru_b_x[0]), row(lru_lambda[0])
    lcw, lcb = lru_conv_w[0], row(lru_conv_b[0])
    mcw, mcb = mlstm_conv_w[0], row(mlstm_conv_b[0])
    wq, wk, wv = w_q[0].astype(BF), w_k[0].astype(BF), w_v[0].astype(BF)
    mg = row(mlstm_norm_g[0])
    wout = w_out[0].astype(BF)
    gffn, gple, gfin = row(norm_ffn_g[0]), row(norm_ple_g[0]), row(final_norm_g)
    wg, wu, wd = w_ffn_gate[0].astype(BF), w_ffn_up[0].astype(BF), w_ffn_down[0].astype(BF)
    wpg, wple = w_ple_gate[0].astype(BF), w_ple[0].astype(BF)
    perm_np = _interleave_matrix()
    perm, perm_t = jnp.asarray(perm_np, BF), jnp.asarray(perm_np.T, BF)

    front_w = (gmix, wbr, wmg, wif, bif, lcw, lcb, wax, ba, bx, lam, mcw, mcb, wq, wk, wv)
    mixer_w = (perm, perm_t) + front_w + (mg, wout)
    ffn_w = (gffn, wg, wu, wd, gple, wpg, wple, gfin)

    nt = T // step
    x1, p_lconv, p_h, p_mconv, p_c, p_n, p_m = pl.pallas_call(
        _mixer_kernel,
        grid=(B, nt),
        in_specs=[pl.BlockSpec((1, step, D_MODEL), lambda b, t: (b, t, 0))]
                 + [_resident(w.shape) for w in mixer_w],
        out_specs=[
            pl.BlockSpec((1, step, D_MODEL), lambda b, t: (b, t, 0)),
            pl.BlockSpec((1, CONV_W - 1, D_MODEL), lambda b, t: (b, 0, 0)),
            pl.BlockSpec((1, 1, D_MODEL), lambda b, t: (b, 0, 0)),
            pl.BlockSpec((1, CONV_W - 1, D_MODEL), lambda b, t: (b, 0, 0)),
            pl.BlockSpec((1, M_HEADS, M_HD, M_HD), lambda b, t: (b, 0, 0, 0)),
            pl.BlockSpec((1, M_HEADS, M_HD), lambda b, t: (b, 0, 0)),
            pl.BlockSpec((1, 1, M_HEADS), lambda b, t: (b, 0, 0)),
        ],
        out_shape=[
            sds((B, T, D_MODEL), F32),
            sds((B, CONV_W - 1, D_MODEL), F32),
            sds((B, 1, D_MODEL), F32),
            sds((B, CONV_W - 1, D_MODEL), F32),
            sds((B, M_HEADS, M_HD, M_HD), F32),
            sds((B, M_HEADS, M_HD), F32),
            sds((B, 1, M_HEADS), F32),
        ],
        compiler_params=_params(2),
        name="prompt_mixer",
    )(x_prompt, *mixer_w)

    xs = x_sample.reshape(S, D_MODEL)
    lbuf = jnp.swapaxes(state_lru_conv[0], 0, 1)
    mbuf = jnp.swapaxes(state_mlstm_conv[0], 0, 1)
    tok = sds((S, D_MODEL), F32)
    buf = sds((CONV_W - 1, S, D_MODEL), F32)
    per_head = sds((S, M_HEADS), F32)
    q, kw, v, sc, hv, hc, s_n, s_m, yl, gm, s_lbuf, s_h, s_mbuf = pl.pallas_call(
        _front_kernel,
        out_shape=[tok, tok, tok, per_head, tok, tok, tok, per_head, tok, tok, buf, tok, buf],
        compiler_params=pltpu.CompilerParams(vmem_limit_bytes=VMEM_LIMIT),
        name="sample_front",
    )(xs, lbuf, state_lru_h[0], mbuf, *front_w, state_mlstm_m[0],
      state_mlstm_n[0].reshape(S, D_MODEL))

    n_tok = B * T
    tile = lambda width: pl.BlockSpec((FFN_TILE, width), lambda i: (i, 0))
    c_spec = pl.BlockSpec((STATE_BATCH, M_HEADS, M_HD, M_HD), lambda i: (i, 0, 0, 0))
    y_prompt, qc, s_c = pl.pallas_call(
        _ffn_kernel,
        grid=(n_tok // FFN_TILE,),
        in_specs=[pl.BlockSpec(memory_space=pltpu.SMEM), tile(D_MODEL), tile(P_DIM),
                  _resident(tok.shape), _resident(tok.shape), _resident(tok.shape), c_spec]
                 + [_resident(w.shape) for w in ffn_w],
        out_specs=[tile(D_MODEL), pl.BlockSpec(tok.shape, lambda i: (0, 0)), c_spec],
        out_shape=[sds((n_tok, D_MODEL), F32), tok, sds((S, M_HEADS, M_HD, M_HD), F32)],
        compiler_params=_params(1),
        name="prompt_ffn",
    )(sc, x1.reshape(n_tok, D_MODEL), p_prompt[0].reshape(n_tok, P_DIM), q, kw, v,
      state_mlstm_C[0], *ffn_w)

    y_sample = pl.pallas_call(
        _back_kernel,
        out_shape=tok,
        scratch_shapes=[pltpu.VMEM((S, D_MODEL), BF)],
        compiler_params=pltpu.CompilerParams(vmem_limit_bytes=VMEM_LIMIT),
        name="sample_back",
    )(xs, qc, hv, hc, yl, gm, p_sample[0].reshape(S, P_DIM), mg, wout, *ffn_w)

    lead = lambda a: a[None]
    return (y_prompt.reshape(B, T, D_MODEL), y_sample.reshape(S, 1, D_MODEL),
            lead(p_lconv), p_h.reshape(1, B, D_MODEL), lead(p_mconv), lead(p_c), lead(p_n),
            p_m.reshape(1, B, M_HEADS),
            lead(jnp.swapaxes(s_lbuf, 0, 1)), lead(s_h), lead(jnp.swapaxes(s_mbuf, 0, 1)),
            lead(s_c), s_n.reshape(1, S, M_HEADS, M_HD), lead(s_m))
```

```python
import numpy as np

import jax
import jax.numpy as jnp
from jax import lax
from jax.experimental import pallas as pl
from jax.experimental.pallas import tpu as pltpu

D_MODEL = 1024
M_HEADS = 4
M_HD = D_MODEL // M_HEADS
LRU_BLOCKS = 8
LRU_BLK = D_MODEL // LRU_BLOCKS
LRU_C = 8.0
CONV_W = 4
D_FF = 2816
P_DIM = 256
EPS = 1e-6
M_INIT = -1e30
NEG_LOG2E = -1.4426950408889634
MASKED = -1e30
N_GATES = 2 * M_HEADS
GATE_PAD = 128
SUBLANES = 8
SEQ_TILE = 256
SEG = SEQ_TILE // SUBLANES
SUB_TILES = 2
FFN_TILE = 512
FFN_SUB = FFN_TILE // SEQ_TILE
FF_CHUNK = 256
STATE_BATCH = 4
PREP_COLS = 256
BRANCH_STEPS = 3 * D_MODEL // PREP_COLS
VMEM_LIMIT = 56 * 1024 * 1024

BF = jnp.bfloat16
F32 = jnp.float32


def _dot(a, b):
    return jnp.dot(a, b, preferred_element_type=F32)


def _dot_nt(a, b):
    return lax.dot_general(a, b, (((1,), (1,)), ((), ())), preferred_element_type=F32)


def _dot_tn(a, b):
    return lax.dot_general(a, b, (((0,), (0,)), ((), ())), preferred_element_type=F32)


def _sigmoid(x):
    return 1.0 / (1.0 + jnp.exp2(x * NEG_LOG2E))


def _softplus(x):
    return jnp.maximum(x, 0.0) + jnp.log1p(jnp.exp(-jnp.abs(x)))


def _rms(x, g):
    return x * lax.rsqrt(jnp.mean(x * x, axis=-1, keepdims=True) + EPS) * g


def _group(x, i):
    return x[i * SUBLANES:(i + 1) * SUBLANES, :]


def _lru_coeffs(xc, ga, gx, lam):
    r = _sigmoid(ga)
    ig = _sigmoid(gx)
    log_a = -LRU_C * r * _softplus(-lam)
    a = jnp.exp(log_a)
    om = 1.0 - a * a
    root = jnp.where(om > 0.0, om * lax.rsqrt(om), 0.0)
    return a, root * ig * xc


def _lru_gates(xcb, wax_ref, ba, bx):
    ga, gx = [], []
    for n in range(LRU_BLOCKS):
        g = _dot(xcb[:, n * LRU_BLK:(n + 1) * LRU_BLK], wax_ref[n])
        ga.append(g[:, :LRU_BLK])
        gx.append(g[:, LRU_BLK:])
    return jnp.concatenate(ga, axis=1) + ba, jnp.concatenate(gx, axis=1) + bx


def _scan_interleaved(a, u, h0):
    prods, sums = [], []
    p = s = None
    for i in range(SEG):
        ai, ui = _group(a, i), _group(u, i)
        p, s = (ai, ui) if i == 0 else (ai * p, ai * s + ui)
        prods.append(p)
        sums.append(s)
    c = h0
    starts = [c]
    for j in range(SUBLANES - 1):
        c = p[j:j + 1, :] * c + s[j:j + 1, :]
        starts.append(c)
    start = jnp.concatenate(starts, axis=0)
    return jnp.concatenate([sums[i] + prods[i] * start for i in range(SEG)], axis=0)


def _cumsum_interleaved(x):
    acc = []
    run = None
    for i in range(SEG):
        run = _group(x, i) if i == 0 else run + _group(x, i)
        acc.append(run)
    sub = lax.broadcasted_iota(jnp.int32, run.shape, 0)
    inc = run
    s = 1
    while s < SUBLANES:
        inc = inc + jnp.where(sub >= s, pltpu.roll(inc, s, 0), 0.0)
        s *= 2
    before = inc - run
    return jnp.concatenate([r + before for r in acc], axis=0)


def _conv_interleaved(carry, x_new, w, b):
    sub = lax.broadcasted_iota(jnp.int32, (SUBLANES, x_new.shape[1]), 0)
    head, new = [], []
    for r in range(CONV_W - 1):
        cur = _group(x_new, SEG - (CONV_W - 1) + r)
        head.append(jnp.where(sub == 0, carry[r:r + 1, :], pltpu.roll(cur, 1, 0)))
        new.append(cur[SUBLANES - 1:SUBLANES, :])
    ext = jnp.concatenate(head + [x_new], axis=0)
    out = b
    for j in range(CONV_W):
        out = out + ext[j * SUBLANES:j * SUBLANES + SEQ_TILE, :] * w[j:j + 1, :]
    return out, jnp.concatenate(new, axis=0)


def _time_of_row(r):
    return (r & (SUBLANES - 1)) * SEG + (r >> (SUBLANES.bit_length() - 1))


class _Plan:
    def __init__(self):
        self.tasks = {}

    def add(self, name, unit, cost, deps, fn):
        self.tasks[name] = (unit, cost, tuple(deps), fn)

    def order(self):
        succ = {n: [] for n in self.tasks}
        for n, (_, _, deps, _) in self.tasks.items():
            for p in deps:
                succ[p].append(n)
        tail = {}

        def path(n):
            if n not in tail:
                tail[n] = self.tasks[n][1] + max([path(s) for s in succ[n]], default=0)
            return tail[n]

        free = {"M": 0, "V": 0}
        done, order, left = {}, [], list(self.tasks)
        while left:
            ready = [n for n in left if all(p in done for p in self.tasks[n][2])]

            def start(n):
                unit, _, deps, _ = self.tasks[n]
                return max([free[unit]] + [done[p] for p in deps])

            n = min(ready, key=lambda n: (start(n), -path(n)))
            unit, cost, _, _ = self.tasks[n]
            st = start(n)
            done[n] = free[unit] = st + cost
            order.append((st, len(order), n))
            left.remove(n)
        return [n for _, _, n in sorted(order)]

    def run(self):
        for n in self.order():
            self.tasks[n][3]()


def _mixer_kernel(x_ref, perm_ref, permt_ref, gmix_ref, wbr_ref, wmg_ref, wif_ref, bif_ref, lcw_ref,
                  lcb_ref, wax_ref, ba_ref, bx_ref, lam_ref, mcw_ref, mcb_ref, wq_ref, wk_ref,
                  wv_ref, mg_ref, wout_ref,
                  x1_ref, lconv_ref, h_ref, mconv_ref, c_ref, n_ref, m_ref):
    tt = SEQ_TILE
    last = tt - 1

    @pl.when(pl.program_id(1) == 0)
    def _():
        lconv_ref[...] = jnp.zeros(lconv_ref.shape, F32)
        mconv_ref[...] = jnp.zeros(mconv_ref.shape, F32)
        h_ref[...] = jnp.zeros(h_ref.shape, F32)
        c_ref[...] = jnp.zeros(c_ref.shape, F32)
        n_ref[...] = jnp.zeros(n_ref.shape, F32)
        m_ref[...] = jnp.full(m_ref.shape, M_INIT, F32)

    lcw, lcb, mcw, mcb = lcw_ref[...], lcb_ref[...], mcw_ref[...], mcb_ref[...]
    ba, bx, lam, mg = ba_ref[...], bx_ref[...], lam_ref[...], mg_ref[...]

    v = {}
    for g in range(M_HEADS):
        gs = slice(g * M_HD, (g + 1) * M_HD)
        v["lconv", -1, g], v["mconv", -1, g] = lconv_ref[0, :, gs], mconv_ref[0, :, gs]
        v["h", -1, g] = h_ref[0, :, gs]
        v["c", -1, g], v["n", -1, g] = c_ref[0, g], n_ref[0, g:g + 1, :]
        v["m", -1, g] = m_ref[0, :, g:g + 1]
    plan = _Plan()

    for t in range(SUB_TILES):
        dep_prev = (lambda name, g, t=t: ((name, t - 1, g),)) if t else (lambda name, g: ())

        def t_norm(t=t):
            v["x", t] = x_ref[0, t * tt:(t + 1) * tt, :]
            v["xn_t", t] = _rms(v["x", t], gmix_ref[...]).astype(BF)
        plan.add(("norm", t), "V", 650, (), t_norm)

        def t_perm(t=t):
            v["xnb", t] = _dot(perm_ref[...], v["xn_t", t]).astype(BF)
        plan.add(("perm", t), "M", 260, (("norm", t),), t_perm)

        def proj(name, w_ref, col0, g, t=t):
            def run():
                v[name, t, g] = _dot(v["xnb", t], w_ref[:, col0 + g * M_HD:col0 + (g + 1) * M_HD])
            plan.add((name, t, g), "M", 260, (("perm", t),), run)

        def t_wif(t=t):
            v["pre", t] = _dot(v["xnb", t], wif_ref[...]) + bif_ref[...]
        plan.add(("wif", t), "M", 260, (("perm", t),), t_wif)

        def t_gcum(t=t):
            pre = v["pre", t]
            lane = lax.broadcasted_iota(jnp.int32, (tt, GATE_PAD), 1)
            v["gcol", t] = jnp.where(lane < M_HEADS, pre, _cumsum_interleaved(-_softplus(-pre)))
            v["grow", t] = v["gcol", t].T
            v["tri", t] = (_time_of_row(lax.broadcasted_iota(jnp.int32, (tt, 1), 0))
                           >= _time_of_row(lax.broadcasted_iota(jnp.int32, (1, tt), 1)))
        plan.add(("gcum", t), "V", 150, (("wif", t),), t_gcum)

        for g in range(M_HEADS):
            gs = slice(g * M_HD, (g + 1) * M_HD)
            for name, w_ref, col0 in (("xl", wbr_ref, 0), ("xm", wbr_ref, D_MODEL),
                                      ("om", wbr_ref, 2 * D_MODEL), ("gl", wmg_ref, 0),
                                      ("gm", wmg_ref, D_MODEL)):
                proj(name, w_ref, col0, g)

            def t_convl(t=t, g=g, gs=gs):
                v["xlc", t, g], v["lconv", t, g] = _conv_interleaved(
                    v["lconv", t - 1, g], v["xl", t, g], lcw[:, gs], lcb[:, gs])
            plan.add(("convl", t, g), "V", 135, (("xl", t, g),) + dep_prev("convl", g), t_convl)

            def t_gates(t=t, g=g, gs=gs):
                xlb = v["xlc", t, g].astype(BF)
                ga, gx = [], []
                for n in range(2):
                    gg = _dot(xlb[:, n * LRU_BLK:(n + 1) * LRU_BLK], wax_ref[2 * g + n])
                    ga.append(gg[:, :LRU_BLK])
                    gx.append(gg[:, LRU_BLK:])
                v["ga", t, g] = jnp.concatenate(ga, axis=1) + ba[:, gs]
                v["gx", t, g] = jnp.concatenate(gx, axis=1) + bx[:, gs]
            plan.add(("gates", t, g), "M", 130, (("convl", t, g),), t_gates)

            def t_coef(t=t, g=g, gs=gs):
                v["a", t, g], v["u", t, g] = _lru_coeffs(v["xlc", t, g], v["ga", t, g], v["gx", t, g],
                                                         lam[:, gs])
            plan.add(("coef", t, g), "V", 280, (("gates", t, g),), t_coef)

            def t_scan(t=t, g=g):
                v["yl", t, g] = _scan_interleaved(v["a", t, g], v["u", t, g], v["h", t - 1, g])
                v["h", t, g] = v["yl", t, g][last:last + 1, :]
            plan.add(("scan", t, g), "V", 120, (("coef", t, g),) + dep_prev("scan", g), t_scan)

            h = g

            def t_convm(t=t, h=h, gs=gs):
                xm_c, v["mconv", t, h] = _conv_interleaved(v["mconv", t - 1, h], v["xm", t, h],
                                                           mcw[:, gs], mcb[:, gs])
                v["xcb", t, h] = (xm_c * _sigmoid(xm_c)).astype(BF)
            plan.add(("convm", t, h), "V", 210, (("xm", t, h),) + dep_prev("convm", h), t_convm)

            def t_qkv(t=t, h=h):
                v["q", t, h] = _dot(v["xcb", t, h], wq_ref[h]) * (M_HD ** -0.5)
                v["k", t, h] = _dot(v["xcb", t, h], wk_ref[h])
                vv = _dot(v["xm", t, h].astype(BF), wv_ref[h])
                v["qb", t, h], v["kb", t, h], v["vb", t, h] = (
                    v["q", t, h].astype(BF), v["k", t, h].astype(BF), vv.astype(BF))
            plan.add(("qkv", t, h), "M", 200, (("convm", t, h),), t_qkv)

            def t_qk(t=t, h=h):
                v["qk", t, h] = _dot_nt(v["qb", t, h], v["kb", t, h])
            plan.add(("qk", t, h), "M", 64, (("qkv", t, h),), t_qk)

            def t_sp(t=t, h=h):
                gcol, grow = v["gcol", t], v["grow", t]
                b_col = gcol[:, M_HEADS + h:M_HEADS + h + 1]
                ig_row, b_row = grow[h:h + 1, :], grow[M_HEADS + h:M_HEADS + h + 1, :]
                dlog = jnp.where(v["tri", t], b_col - b_row + ig_row, MASKED)
                m_inter = b_col + v["m", t - 1, h]
                m_t = jnp.maximum(m_inter, jnp.max(dlog, axis=1, keepdims=True))
                s = v["qk", t, h] * jnp.exp(dlog - m_t)
                v["m_t", t, h], v["sc", t, h] = m_t, jnp.exp(m_inter - m_t)
                v["ssum", t, h] = jnp.sum(s, axis=1, keepdims=True)
                v["sb", t, h] = s.astype(BF)
                v["m", t, h] = m_t[last:last + 1, :]
            plan.add(("sp", t, h), "V", 200, (("qk", t, h), ("gcum", t)) + dep_prev("sp", h), t_sp)

            def t_sv(t=t, h=h):
                v["sv", t, h] = _dot(v["sb", t, h], v["vb", t, h])
                v["qc", t, h] = _dot(v["qb", t, h], v["c", t - 1, h].astype(BF))
            plan.add(("sv", t, h), "M", 130, (("sp", t, h),) + dep_prev("cnew", h), t_sv)

            def t_hn(t=t, h=h, gs=gs):
                sc, m_t = v["sc", t, h], v["m_t", t, h]
                num = v["sv", t, h] + sc * v["qc", t, h]
                den = v["ssum", t, h] + sc * jnp.sum(v["q", t, h] * v["n", t - 1, h], axis=1,
                                                     keepdims=True)
                hh = num / jnp.maximum(jnp.abs(den), jnp.exp(-m_t))
                v["hn", t, h] = (hh * lax.rsqrt(jnp.mean(hh * hh, axis=-1, keepdims=True) + EPS)
                                 * mg[:, gs])
            plan.add(("hn", t, h), "V", 170, (("sv", t, h),) + dep_prev("kw", h), t_hn)

            def t_kw(t=t, h=h):
                gcol = v["gcol", t]
                ig_col, b_col = gcol[:, h:h + 1], gcol[:, M_HEADS + h:M_HEADS + h + 1]
                m_last = v["m", t, h]
                b_last = b_col[last:last + 1, :]
                kw = v["k", t, h] * jnp.exp(b_last - b_col + ig_col - m_last)
                v["dec", t, h] = jnp.exp(b_last + v["m", t - 1, h] - m_last)
                v["n", t, h] = v["dec", t, h] * v["n", t - 1, h] + jnp.sum(kw, axis=0, keepdims=True)
                v["kwb", t, h] = kw.astype(BF)
            plan.add(("kw", t, h), "V", 80, (("sp", t, h),) + dep_prev("kw", h), t_kw)

            def t_ckv(t=t, h=h):
                v["ckv", t, h] = _dot_tn(v["kwb", t, h], v["vb", t, h])
            plan.add(("ckv", t, h), "M", 64, (("kw", t, h),), t_ckv)

            def t_cnew(t=t, h=h):
                v["c", t, h] = v["dec", t, h] * v["c", t - 1, h] + v["ckv", t, h]
            plan.add(("cnew", t, h), "V", 40, (("ckv", t, h),) + dep_prev("cnew", h), t_cnew)

            def t_sig(t=t, h=h):
                v["gate_l", t, h] = _sigmoid(v["gl", t, h]) * v["yl", t, h]
                v["gate_m", t, h] = _sigmoid(v["gm", t, h]) * _sigmoid(v["om", t, h])
            plan.add(("sig", t, h), "V", 180,
                     (("gl", t, h), ("gm", t, h), ("om", t, h), ("scan", t, h)), t_sig)

            def t_mrg(t=t, h=h):
                v["mrg", t, h] = (v["gate_l", t, h] + v["gate_m", t, h] * v["hn", t, h]).astype(BF)
            plan.add(("mrg", t, h), "V", 40, (("sig", t, h), ("hn", t, h)), t_mrg)

            def t_pt(t=t, h=h):
                v["mrg_t", t, h] = _dot(permt_ref[...], v["mrg", t, h]).astype(BF)
            plan.add(("pt", t, h), "M", 64, (("mrg", t, h),), t_pt)

            def t_wo(t=t, h=h, gs=gs):
                part = _dot(v["mrg_t", t, h], wout_ref[gs, :])
                v["out", t] = part if h == 0 else v["out", t] + part
            plan.add(("wo", t, h), "M", 260, (("pt", t, h),) + ((("wo", t, h - 1),) if h else ()),
                     t_wo)

        def t_fin(t=t):
            x1_ref[0, t * tt:(t + 1) * tt, :] = v["x", t] + v["out", t]
        plan.add(("fin", t), "V", 70, (("wo", t, M_HEADS - 1),), t_fin)

    def t_state():
        e = SUB_TILES - 1
        heads = range(M_HEADS)
        lconv_ref[0] = jnp.concatenate([v["lconv", e, g] for g in heads], axis=1)
        mconv_ref[0] = jnp.concatenate([v["mconv", e, g] for g in heads], axis=1)
        h_ref[0] = jnp.concatenate([v["h", e, g] for g in heads], axis=1)
        for h in heads:
            c_ref[0, h] = v["c", e, h]
        n_ref[0] = jnp.concatenate([v["n", e, h] for h in heads], axis=0)
        m_ref[0] = jnp.concatenate([v["m", e, h] for h in heads], axis=1)
    plan.add("state", "V", 60, tuple(("cnew", SUB_TILES - 1, h) for h in range(M_HEADS))
             + tuple(("scan", SUB_TILES - 1, h) for h in range(M_HEADS))
             + tuple(("fin", t) for t in range(SUB_TILES)), t_state)

    plan.run()


def _split_kernel(wt_ref, gt_ref, wbr_ref, wmg_ref, wif_ref):
    j = pl.program_id(0)
    piece = wt_ref[...].T.astype(BF)

    @pl.when(j < BRANCH_STEPS)
    def _():
        wbr_ref[...] = piece

    @pl.when(j >= BRANCH_STEPS)
    def _():
        wmg_ref[...] = piece

    @pl.when(j == 0)
    def _():
        gates = gt_ref[...].T
        lane = lax.broadcasted_iota(jnp.int32, gates.shape, 1)
        wif_ref[...] = jnp.where(lane < N_GATES, gates, 0.0).astype(BF)


def _ffn_tail(x1, p, gffn_ref, wg_ref, wu_ref, wd_ref, gple_ref, wpg_ref, wple_ref, gfin_ref):
    xnb = _rms(x1, gffn_ref[...]).astype(BF)
    x2 = x1
    for c in range(D_FF // FF_CHUNK):
        cs = slice(c * FF_CHUNK, (c + 1) * FF_CHUNK)
        hg = _dot(xnb, wg_ref[:, cs])
        hu = _dot(xnb, wu_ref[:, cs])
        act = (hg * _sigmoid(hg) * hu).astype(BF)
        x2 = x2 + _dot(act, wd_ref[cs, :])
    gate = _sigmoid(_dot(_rms(x2, gple_ref[...]).astype(BF), wpg_ref[...]))
    x3 = x2 + gate * _dot(p.astype(BF), wple_ref[...])
    return _rms(x3, gfin_ref[...])


def _ffn_kernel(sc_ref, x1_ref, p_ref, q_ref, kw_ref, v_ref, c_ref, gffn_ref, wg_ref, wu_ref,
                wd_ref, gple_ref, wpg_ref, wple_ref, gfin_ref, y_ref, qc_out, c_out):
    tt = SEQ_TILE
    v = {}
    plan = _Plan()

    for t in range(FFN_SUB):
        rows = slice(t * tt, (t + 1) * tt)

        def t_norm(t=t, rows=rows):
            v["x", t] = x1_ref[rows, :]
            v["xnb", t] = _rms(v["x", t], gffn_ref[...]).astype(BF)
        plan.add(("norm", t), "V", 650, (), t_norm)

        for c in range(D_FF // FF_CHUNK):
            cs = slice(c * FF_CHUNK, (c + 1) * FF_CHUNK)

            def t_gu(t=t, c=c, cs=cs):
                v["hg", t, c] = _dot(v["xnb", t], wg_ref[:, cs])
                v["hu", t, c] = _dot(v["xnb", t], wu_ref[:, cs])
            plan.add(("gu", t, c), "M", 512, (("norm", t),), t_gu)

            def t_act(t=t, c=c):
                hg = v["hg", t, c]
                v["act", t, c] = (hg * _sigmoid(hg) * v["hu", t, c]).astype(BF)
            plan.add(("act", t, c), "V", 110, (("gu", t, c),), t_act)

            def t_dn(t=t, c=c, cs=cs):
                prev = v["x", t] if c == 0 else v["x2", t]
                v["x2", t] = prev + _dot(v["act", t, c], wd_ref[cs, :])
            plan.add(("dn", t, c), "M", 300, (("act", t, c),) + ((("dn", t, c - 1),) if c else ()),
                     t_dn)

        last_dn = ("dn", t, D_FF // FF_CHUNK - 1)

        def t_norm2(t=t):
            v["xgb", t] = _rms(v["x2", t], gple_ref[...]).astype(BF)
        plan.add(("norm2", t), "V", 650, (last_dn,), t_norm2)

        def t_ple(t=t, rows=rows):
            v["pe", t] = _dot(p_ref[rows, :].astype(BF), wple_ref[...])
        plan.add(("ple", t), "M", 300, (), t_ple)

        def t_wpg(t=t):
            v["gpre", t] = _dot(v["xgb", t], wpg_ref[...])
        plan.add(("wpg", t), "M", 1024, (("norm2", t),), t_wpg)

        def t_fin(t=t, rows=rows):
            x3 = v["x2", t] + _sigmoid(v["gpre", t]) * v["pe", t]
            y_ref[rows, :] = _rms(x3, gfin_ref[...])
        plan.add(("fin", t), "V", 1000, (("wpg", t), ("ple", t)), t_fin)

    base = pl.program_id(0) * STATE_BATCH
    heads = [slice(h * M_HD, (h + 1) * M_HD) for h in range(M_HEADS)]
    for bb in range(STATE_BATCH):
        def t_cols(bb=bb):
            row = pl.ds(base + bb, 1)
            q, kw = q_ref[row, :], kw_ref[row, :]
            v["cols", bb] = jnp.concatenate([q[:, hs] for hs in heads] + [kw[:, hs] for hs in heads],
                                            axis=0).T
            v["vrow", bb] = v_ref[row, :]
        plan.add(("cols", bb), "V", 40, (), t_cols)

        for h, hs in enumerate(heads):
            def t_pair(bb=bb, h=h, hs=hs):
                cols = v["cols", bb]
                q_col, kw_col = cols[:, h:h + 1], cols[:, M_HEADS + h:M_HEADS + h + 1]
                c_prev = c_ref[bb, h]
                qc_out[pl.ds(base + bb, 1), hs] = jnp.sum(q_col * c_prev, axis=0, keepdims=True)
                c_out[bb, h] = sc_ref[base + bb, h] * c_prev + kw_col * v["vrow", bb][:, hs]
            plan.add(("pair", bb, h), "V", 170, (("cols", bb),), t_pair)

    plan.run()


def _front_kernel(x_ref, lbuf_ref, h0_ref, mbuf_ref, gmix_ref, wbr_ref, wmg_ref, wif_ref, bif_ref,
                  lcw_ref, lcb_ref, wax_ref, ba_ref, bx_ref, lam_ref, mcw_ref, mcb_ref,
                  wq_ref, wk_ref, wv_ref, m0_ref, n0_ref,
                  q_ref, kw_ref, v_ref, sc_ref, hv_ref, hc_ref, n_out, m_out,
                  yl_ref, gm_ref, lbuf_out, h_out, mbuf_out):
    xnb = _rms(x_ref[...], gmix_ref[...]).astype(BF)

    def per_head(cols):
        return jnp.concatenate([jnp.broadcast_to(cols[:, h:h + 1], (cols.shape[0], M_HD))
                                for h in range(M_HEADS)], axis=1)

    def head_sums(a):
        return jnp.concatenate([jnp.sum(a[:, h * M_HD:(h + 1) * M_HD], axis=1, keepdims=True)
                                for h in range(M_HEADS)], axis=1)

    def conv_step(buf_ref, buf_out, x_new, w_ref, b_ref):
        out = b_ref[...] + x_new * w_ref[CONV_W - 1:CONV_W, :]
        for j in range(CONV_W - 1):
            out = out + buf_ref[j] * w_ref[j:j + 1, :]
        for j in range(CONV_W - 2):
            buf_out[j] = buf_ref[j + 1]
        buf_out[CONV_W - 2] = x_new
        return out

    x_l = _dot(xnb, wbr_ref[:, 0:D_MODEL])
    xl_c = conv_step(lbuf_ref, lbuf_out, x_l, lcw_ref, lcb_ref)
    ga, gx = _lru_gates(xl_c.astype(BF), wax_ref, ba_ref[...], bx_ref[...])
    a, u = _lru_coeffs(xl_c, ga, gx, lam_ref[...])
    y_l = a * h0_ref[...] + u
    h_out[...] = y_l
    g_l = _dot(xnb, wmg_ref[:, 0:D_MODEL])
    yl_ref[...] = _sigmoid(g_l) * y_l

    x_m = _dot(xnb, wbr_ref[:, D_MODEL:2 * D_MODEL])
    xm_c = conv_step(mbuf_ref, mbuf_out, x_m, mcw_ref, mcb_ref)
    xcb = (xm_c * _sigmoid(xm_c)).astype(BF)
    xmb = x_m.astype(BF)
    q, k, v = [], [], []
    for h in range(M_HEADS):
        hs = slice(h * M_HD, (h + 1) * M_HD)
        q.append(_dot(xcb[:, hs], wq_ref[h]) * (M_HD ** -0.5))
        k.append(_dot(xcb[:, hs], wk_ref[h]))
        v.append(_dot(xmb[:, hs], wv_ref[h]))
    q, k, v = (jnp.concatenate(a, axis=1) for a in (q, k, v))
    o_m = _dot(xnb, wbr_ref[:, 2 * D_MODEL:3 * D_MODEL])
    g_m = _dot(xnb, wmg_ref[:, D_MODEL:2 * D_MODEL])
    gm_ref[...] = _sigmoid(g_m) * _sigmoid(o_m)

    pre = _dot(xnb, wif_ref[...]) + bif_ref[...]
    ig = pre[:, 0:M_HEADS]
    lf = -_softplus(-pre[:, M_HEADS:N_GATES])
    n_prev = n0_ref[...]
    m_inter = lf + m0_ref[...]
    m_t = jnp.maximum(m_inter, ig)
    wk = jnp.exp(ig - m_t)
    sc = jnp.exp(m_inter - m_t)
    s = head_sums(q * k) * wk
    den = s + sc * head_sums(q * n_prev)
    rden = 1.0 / jnp.maximum(jnp.abs(den), jnp.exp(-m_t))
    wk_d, sc_d = per_head(wk), per_head(sc)
    hv_ref[...] = per_head(s * rden) * v
    hc_ref[...] = per_head(sc * rden)
    q_ref[...] = q
    kw_ref[...] = wk_d * k
    v_ref[...] = v
    sc_ref[...] = sc
    n_out[...] = sc_d * n_prev + wk_d * k
    m_out[...] = m_t


def _back_kernel(x_ref, qc_ref, hv_ref, hc_ref, yl_ref, gm_ref, p_ref, mg_ref, wout_ref, gffn_ref,
                 wg_ref, wu_ref, wd_ref, gple_ref, wpg_ref, wple_ref, gfin_ref, y_ref, mrg_ref):
    for h in range(M_HEADS):
        hs = slice(h * M_HD, (h + 1) * M_HD)
        hh = hv_ref[:, hs] + hc_ref[:, hs] * qc_ref[:, hs]
        hn = hh * lax.rsqrt(jnp.mean(hh * hh, axis=-1, keepdims=True) + EPS) * mg_ref[:, hs]
        mrg_ref[:, hs] = (yl_ref[:, hs] + gm_ref[:, hs] * hn).astype(BF)
    x1 = x_ref[...] + _dot(mrg_ref[...], wout_ref[...])
    y_ref[...] = _ffn_tail(x1, p_ref[...], gffn_ref, wg_ref, wu_ref, wd_ref, gple_ref, wpg_ref,
                           wple_ref, gfin_ref)


def _resident(shape):
    nd = len(shape)
    return pl.BlockSpec(shape, lambda *_: (0,) * nd, pipeline_mode=pl.Buffered(1))


def _params(n_axes):
    return pltpu.CompilerParams(dimension_semantics=("arbitrary",) * n_axes,
                                vmem_limit_bytes=VMEM_LIMIT)


def _interleave_matrix():
    r = np.arange(SEQ_TILE)
    perm = np.zeros((SEQ_TILE, SEQ_TILE), np.float32)
    perm[r, (r % SUBLANES) * SEG + r // SUBLANES] = 1.0
    return perm


def kernel(x_prompt, x_sample, state_lru_conv, state_lru_h, state_mlstm_conv, state_mlstm_C, state_mlstm_n, state_mlstm_m, p_prompt, p_sample, norm_mix_g, w_in, b_gates, lru_conv_w, lru_conv_b, lru_w_a, lru_b_a, lru_w_x, lru_b_x, lru_lambda, mlstm_conv_w, mlstm_conv_b, w_q, w_k, w_v, mlstm_norm_g, w_out, norm_ffn_g, w_ffn_gate, w_ffn_up, w_ffn_down, norm_ple_g, w_ple_gate, w_ple, final_norm_g):
    assert w_in.shape[0] == 1, "single-layer trunk"
    B, T, _ = x_prompt.shape
    S = x_sample.shape[0]
    step = SEQ_TILE * SUB_TILES
    assert T % step == 0 and (B * T) % FFN_TILE == 0 and x_sample.shape[1] == 1
    assert S == STATE_BATCH * ((B * T) // FFN_TILE), "one state batch per ffn grid step"

    sds = jax.ShapeDtypeStruct
    assert w_in.shape[1:] == (D_MODEL, 5 * D_MODEL + N_GATES)
    wt = jnp.swapaxes(w_in[0], 0, 1)
    g0 = 3 * D_MODEL
    src_row = lambda j: pl.multiple_of(
        jnp.where(j < BRANCH_STEPS, j * PREP_COLS, g0 + N_GATES + (j - BRANCH_STEPS) * PREP_COLS),
        SUBLANES)
    slab = lambda rows, index: pl.BlockSpec((pl.Element(rows), pl.Element(D_MODEL)), index)
    cols = lambda index: pl.BlockSpec((D_MODEL, PREP_COLS), index)
    wbr, wmg, wif = pl.pallas_call(
        _split_kernel,
        grid=(5 * D_MODEL // PREP_COLS,),
        in_specs=[slab(PREP_COLS, lambda j: (src_row(j), 0)), slab(GATE_PAD, lambda j: (g0, 0))],
        out_specs=[cols(lambda j: (0, jnp.minimum(j, BRANCH_STEPS - 1))),
                   cols(lambda j: (0, jnp.maximum(j - BRANCH_STEPS, 0))),
                   pl.BlockSpec((D_MODEL, GATE_PAD), lambda j: (0, 0))],
        out_shape=[sds((D_MODEL, 3 * D_MODEL), BF), sds((D_MODEL, 2 * D_MODEL), BF),
                   sds((D_MODEL, GATE_PAD), BF)],
        compiler_params=_params(1),
        name="split_w_in",
    )(wt, wt)
    bif = jnp.pad(b_gates[0], (0, GATE_PAD - N_GATES)).reshape(1, GATE_PAD)
    wax = jnp.concatenate([lru_w_a[0], lru_w_x[0]], axis=2).astype(BF)
    row = lambda a: a.reshape(1, -1)
    gmix, ba, bx, lam = row(norm_mix_g[0]), row(lru_b_a[0]), row(lru_b_x[0]), row(lru_lambda[0])
    lcw, lcb = lru_conv_w[0], row(lru_conv_b[0])
    mcw, mcb = mlstm_conv_w[0], row(mlstm_conv_b[0])
    wq, wk, wv = w_q[0].astype(BF), w_k[0].astype(BF), w_v[0].astype(BF)
    mg = row(mlstm_norm_g[0])
    wout = w_out[0].astype(BF)
    gffn, gple, gfin = row(norm_ffn_g[0]), row(norm_ple_g[0]), row(final_norm_g)
    wg, wu, wd = w_ffn_gate[0].astype(BF), w_ffn_up[0].astype(BF), w_ffn_down[0].astype(BF)
    wpg, wple = w_ple_gate[0].astype(BF), w_ple[0].astype(BF)
    perm_np = _interleave_matrix()
    perm, perm_t = jnp.asarray(perm_np, BF), jnp.asarray(perm_np.T, BF)

    front_w = (gmix, wbr, wmg, wif, bif, lcw, lcb, wax, ba, bx, lam, mcw, mcb, wq, wk, wv)
    mixer_w = (perm, perm_t) + front_w + (mg, wout)
    ffn_w = (gffn, wg, wu, wd, gple, wpg, wple, gfin)

    nt = T // step
    x1, p_lconv, p_h, p_mconv, p_c, p_n, p_m = pl.pallas_call(
        _mixer_kernel,
        grid=(B, nt),
        in_specs=[pl.BlockSpec((1, step, D_MODEL), lambda b, t: (b, t, 0))]
                 + [_resident(w.shape) for w in mixer_w],
        out_specs=[
            pl.BlockSpec((1, step, D_MODEL), lambda b, t: (b, t, 0)),
            pl.BlockSpec((1, CONV_W - 1, D_MODEL), lambda b, t: (b, 0, 0)),
            pl.BlockSpec((1, 1, D_MODEL), lambda b, t: (b, 0, 0)),
            pl.BlockSpec((1, CONV_W - 1, D_MODEL), lambda b, t: (b, 0, 0)),
            pl.BlockSpec((1, M_HEADS, M_HD, M_HD), lambda b, t: (b, 0, 0, 0)),
            pl.BlockSpec((1, M_HEADS, M_HD), lambda b, t: (b, 0, 0)),
            pl.BlockSpec((1, 1, M_HEADS), lambda b, t: (b, 0, 0)),
        ],
        out_shape=[
            sds((B, T, D_MODEL), F32),
            sds((B, CONV_W - 1, D_MODEL), F32),
            sds((B, 1, D_MODEL), F32),
            sds((B, CONV_W - 1, D_MODEL), F32),
            sds((B, M_HEADS, M_HD, M_HD), F32),
            sds((B, M_HEADS, M_HD), F32),
            sds((B, 1, M_HEADS), F32),
        ],
        compiler_params=_params(2),
        name="prompt_mixer",
    )(x_prompt, *mixer_w)

    xs = x_sample.reshape(S, D_MODEL)
    lbuf = jnp.swapaxes(state_lru_conv[0], 0, 1)
    mbuf = jnp.swapaxes(state_mlstm_conv[0], 0, 1)
    tok = sds((S, D_MODEL), F32)
    buf = sds((CONV_W - 1, S, D_MODEL), F32)
    per_head = sds((S, M_HEADS), F32)
    q, kw, v, sc, hv, hc, s_n, s_m, yl, gm, s_lbuf, s_h, s_mbuf = pl.pallas_call(
        _front_kernel,
        out_shape=[tok, tok, tok, per_head, tok, tok, tok, per_head, tok, tok, buf, tok, buf],
        compiler_params=pltpu.CompilerParams(vmem_limit_bytes=VMEM_LIMIT),
        name="sample_front",
    )(xs, lbuf, state_lru_h[0], mbuf, *front_w, state_mlstm_m[0],
      state_mlstm_n[0].reshape(S, D_MODEL))

    n_tok = B * T
    tile = lambda width: pl.BlockSpec((FFN_TILE, width), lambda i: (i, 0))
    c_spec = pl.BlockSpec((STATE_BATCH, M_HEADS, M_HD, M_HD), lambda i: (i, 0, 0, 0))
    y_prompt, qc, s_c = pl.pallas_call(
        _ffn_kernel,
        grid=(n_tok // FFN_TILE,),
        in_specs=[pl.BlockSpec(memory_space=pltpu.SMEM), tile(D_MODEL), tile(P_DIM),
                  _resident(tok.shape), _resident(tok.shape), _resident(tok.shape), c_spec]
                 + [_resident(w.shape) for w in ffn_w],
        out_specs=[tile(D_MODEL), pl.BlockSpec(tok.shape, lambda i: (0, 0)), c_spec],
        out_shape=[sds((n_tok, D_MODEL), F32), tok, sds((S, M_HEADS, M_HD, M_HD), F32)],
        compiler_params=_params(1),
        name="prompt_ffn",
    )(sc, x1.reshape(n_tok, D_MODEL), p_prompt[0].reshape(n_tok, P_DIM), q, kw, v,
      state_mlstm_C[0], *ffn_w)

    y_sample = pl.pallas_call(
        _back_kernel,
        out_shape=tok,
        scratch_shapes=[pltpu.VMEM((S, D_MODEL), BF)],
        compiler_params=pltpu.CompilerParams(vmem_limit_bytes=VMEM_LIMIT),
        name="sample_back",
    )(xs, qc, hv, hc, yl, gm, p_sample[0].reshape(S, P_DIM), mg, wout, *ffn_w)

    lead = lambda a: a[None]
    return (y_prompt.reshape(B, T, D_MODEL), y_sample.reshape(S, 1, D_MODEL),
            lead(p_lconv), p_h.reshape(1, B, D_MODEL), lead(p_mconv), lead(p_c), lead(p_n),
            p_m.reshape(1, B, M_HEADS),
            lead(jnp.swapaxes(s_lbuf, 0, 1)), lead(s_h), lead(jnp.swapaxes(s_mbuf, 0, 1)),
            lead(s_c), s_n.reshape(1, S, M_HEADS, M_HD), lead(s_m))
```

```python
import numpy as np

import jax
import jax.numpy as jnp
from jax import lax
from jax.experimental import pallas as pl
from jax.experimental.pallas import tpu as pltpu

D_MODEL = 1024
M_HEADS = 4
M_HD = D_MODEL // M_HEADS
LRU_BLOCKS = 8
LRU_BLK = D_MODEL // LRU_BLOCKS
LRU_C = 8.0
CONV_W = 4
D_FF = 2816
P_DIM = 256
EPS = 1e-6
M_INIT = -1e30
NEG_LOG2E = -1.4426950408889634
MASKED = -1e30
N_GATES = 2 * M_HEADS
GATE_PAD = 128
SUBLANES = 8
SEQ_TILE = 256
SEG = SEQ_TILE // SUBLANES
SUB_TILES = 2
FFN_TILE = 512
FFN_ROWS = 512
FFN_SUB = FFN_TILE // FFN_ROWS
FF_CHUNK = 256
STATE_BATCH = 4
PREP_COLS = 512
BRANCH_STEPS = 3 * D_MODEL // PREP_COLS
VMEM_LIMIT = 56 * 1024 * 1024

BF = jnp.bfloat16
F32 = jnp.float32


def _dot(a, b):
    return jnp.dot(a, b, preferred_element_type=F32)


def _dot_nt(a, b):
    return lax.dot_general(a, b, (((1,), (1,)), ((), ())), preferred_element_type=F32)


def _dot_tn(a, b):
    return lax.dot_general(a, b, (((0,), (0,)), ((), ())), preferred_element_type=F32)


def _sigmoid(x):
    return 1.0 / (1.0 + jnp.exp2(x * NEG_LOG2E))


def _softplus(x):
    return jnp.maximum(x, 0.0) + jnp.log1p(jnp.exp(-jnp.abs(x)))


def _rms(x, g):
    return x * lax.rsqrt(jnp.mean(x * x, axis=-1, keepdims=True) + EPS) * g


def _group(x, i):
    return x[i * SUBLANES:(i + 1) * SUBLANES, :]


def _lru_coeffs(xc, ga, gx, lam):
    r = _sigmoid(ga)
    ig = _sigmoid(gx)
    log_a = -LRU_C * r * _softplus(-lam)
    a = jnp.exp(log_a)
    om = 1.0 - a * a
    root = jnp.where(om > 0.0, om * lax.rsqrt(om), 0.0)
    return a, root * ig * xc


def _lru_gates(xcb, wax_ref, ba, bx):
    ga, gx = [], []
    for n in range(LRU_BLOCKS):
        g = _dot(xcb[:, n * LRU_BLK:(n + 1) * LRU_BLK], wax_ref[n])
        ga.append(g[:, :LRU_BLK])
        gx.append(g[:, LRU_BLK:])
    return jnp.concatenate(ga, axis=1) + ba, jnp.concatenate(gx, axis=1) + bx


def _scan_interleaved(a, u, h0):
    prods, sums = [], []
    p = s = None
    for i in range(SEG):
        ai, ui = _group(a, i), _group(u, i)
        p, s = (ai, ui) if i == 0 else (ai * p, ai * s + ui)
        prods.append(p)
        sums.append(s)
    c = h0
    starts = [c]
    for j in range(SUBLANES - 1):
        c = p[j:j + 1, :] * c + s[j:j + 1, :]
        starts.append(c)
    start = jnp.concatenate(starts, axis=0)
    return jnp.concatenate([sums[i] + prods[i] * start for i in range(SEG)], axis=0)


def _cumsum_interleaved(x):
    acc = []
    run = None
    for i in range(SEG):
        run = _group(x, i) if i == 0 else run + _group(x, i)
        acc.append(run)
    sub = lax.broadcasted_iota(jnp.int32, run.shape, 0)
    inc = run
    s = 1
    while s < SUBLANES:
        inc = inc + jnp.where(sub >= s, pltpu.roll(inc, s, 0), 0.0)
        s *= 2
    before = inc - run
    return jnp.concatenate([r + before for r in acc], axis=0)


def _conv_interleaved(carry, x_new, w, b):
    sub = lax.broadcasted_iota(jnp.int32, (SUBLANES, x_new.shape[1]), 0)
    head, new = [], []
    for r in range(CONV_W - 1):
        cur = _group(x_new, SEG - (CONV_W - 1) + r)
        head.append(jnp.where(sub == 0, carry[r:r + 1, :], pltpu.roll(cur, 1, 0)))
        new.append(cur[SUBLANES - 1:SUBLANES, :])
    ext = jnp.concatenate(head + [x_new], axis=0)
    out = b
    for j in range(CONV_W):
        out = out + ext[j * SUBLANES:j * SUBLANES + SEQ_TILE, :] * w[j:j + 1, :]
    return out, jnp.concatenate(new, axis=0)


def _time_of_row(r):
    return (r & (SUBLANES - 1)) * SEG + (r >> (SUBLANES.bit_length() - 1))


class _Plan:
    def __init__(self):
        self.tasks = {}

    def add(self, name, unit, cost, deps, fn):
        self.tasks[name] = (unit, cost, tuple(deps), fn)

    def order(self):
        succ = {n: [] for n in self.tasks}
        for n, (_, _, deps, _) in self.tasks.items():
            for p in deps:
                succ[p].append(n)
        tail = {}

        def path(n):
            if n not in tail:
                tail[n] = self.tasks[n][1] + max([path(s) for s in succ[n]], default=0)
            return tail[n]

        free = {"M": 0, "V": 0}
        done, order, left = {}, [], list(self.tasks)
        while left:
            ready = [n for n in left if all(p in done for p in self.tasks[n][2])]

            def start(n):
                unit, _, deps, _ = self.tasks[n]
                return max([free[unit]] + [done[p] for p in deps])

            n = min(ready, key=lambda n: (start(n), -path(n)))
            unit, cost, _, _ = self.tasks[n]
            st = start(n)
            done[n] = free[unit] = st + cost
            order.append((st, len(order), n))
            left.remove(n)
        return [n for _, _, n in sorted(order)]

    def run(self):
        for n in self.order():
            self.tasks[n][3]()


def _mixer_kernel(x_ref, perm_ref, permt_ref, gmix_ref, wbr_ref, wmg_ref, wif_ref, bif_ref, lcw_ref,
                  lcb_ref, wax_ref, ba_ref, bx_ref, lam_ref, mcw_ref, mcb_ref, wq_ref, wk_ref,
                  wv_ref, mg_ref, wout_ref,
                  x1_ref, lconv_ref, h_ref, mconv_ref, c_ref, n_ref, m_ref):
    tt = SEQ_TILE
    last = tt - 1

    @pl.when(pl.program_id(1) == 0)
    def _():
        lconv_ref[...] = jnp.zeros(lconv_ref.shape, F32)
        mconv_ref[...] = jnp.zeros(mconv_ref.shape, F32)
        h_ref[...] = jnp.zeros(h_ref.shape, F32)
        c_ref[...] = jnp.zeros(c_ref.shape, F32)
        n_ref[...] = jnp.zeros(n_ref.shape, F32)
        m_ref[...] = jnp.full(m_ref.shape, M_INIT, F32)

    lcw, lcb, mcw, mcb = lcw_ref[...], lcb_ref[...], mcw_ref[...], mcb_ref[...]
    ba, bx, lam, mg = ba_ref[...], bx_ref[...], lam_ref[...], mg_ref[...]

    v = {}
    for g in range(M_HEADS):
        gs = slice(g * M_HD, (g + 1) * M_HD)
        v["lconv", -1, g], v["mconv", -1, g] = lconv_ref[0, :, gs], mconv_ref[0, :, gs]
        v["h", -1, g] = h_ref[0, :, gs]
        v["c", -1, g], v["n", -1, g] = c_ref[0, g], n_ref[0, g:g + 1, :]
        v["m", -1, g] = m_ref[0, :, g:g + 1]
    plan = _Plan()

    def rows_of(full, t):
        return full[t * tt:(t + 1) * tt, :]

    def t_cat():
        v["xnb_all"] = jnp.concatenate([v["xnb", t] for t in range(SUB_TILES)], axis=0)
    plan.add("cat", "V", 1, tuple(("perm", t) for t in range(SUB_TILES)), t_cat)

    def proj(name, w_ref, col0, g):
        def run():
            full = _dot(v["xnb_all"], w_ref[:, col0 + g * M_HD:col0 + (g + 1) * M_HD])
            for t in range(SUB_TILES):
                v[name, t, g] = rows_of(full, t)
        plan.add((name, g), "M", 260 * SUB_TILES, ("cat",), run)

    for g in range(M_HEADS):
        for name, w_ref, col0 in (("xl", wbr_ref, 0), ("xm", wbr_ref, D_MODEL),
                                  ("om", wbr_ref, 2 * D_MODEL), ("gl", wmg_ref, 0),
                                  ("gm", wmg_ref, D_MODEL)):
            proj(name, w_ref, col0, g)

    def t_wif():
        full = _dot(v["xnb_all"], wif_ref[...]) + bif_ref[...]
        for t in range(SUB_TILES):
            v["pre", t] = rows_of(full, t)
    plan.add("wif", "M", 260 * SUB_TILES, ("cat",), t_wif)

    for t in range(SUB_TILES):
        dep_prev =(lambda name, g, t=t: ((name, t - 1, g),)) if t else (lambda name, g: ())

        def t_norm(t=t):
            v["x", t] = x_ref[0, t * tt:(t + 1) * tt, :]
            v["xn_t", t] = _rms(v["x", t], gmix_ref[...]).astype(BF)
        plan.add(("norm", t), "V", 650, (), t_norm)

        def t_perm(t=t):
            v["xnb", t] = _dot(perm_ref[...], v["xn_t", t]).astype(BF)
        plan.add(("perm", t), "M", 260, (("norm", t),), t_perm)

        def t_gcum(t=t):
            pre = v["pre", t]
            lane = lax.broadcasted_iota(jnp.int32, (tt, GATE_PAD), 1)
            v["gcol", t] = jnp.where(lane < M_HEADS, pre, _cumsum_interleaved(-_softplus(-pre)))
            v["grow", t] = v["gcol", t].T
            v["tri", t] = (_time_of_row(lax.broadcasted_iota(jnp.int32, (tt, 1), 0))
                           >= _time_of_row(lax.broadcasted_iota(jnp.int32, (1, tt), 1)))
        plan.add(("gcum", t), "V", 150, ("wif",), t_gcum)

        for g in range(M_HEADS):
            gs = slice(g * M_HD, (g + 1) * M_HD)
            def t_convl(t=t, g=g, gs=gs):
                v["xlc", t, g], v["lconv", t, g] = _conv_interleaved(
                    v["lconv", t - 1, g], v["xl", t, g], lcw[:, gs], lcb[:, gs])
            plan.add(("convl", t, g), "V", 135, (("xl", g),) + dep_prev("convl", g), t_convl)

            def t_gates(t=t, g=g, gs=gs):
                xlb = v["xlc", t, g].astype(BF)
                ga, gx = [], []
                for n in range(2):
                    gg = _dot(xlb[:, n * LRU_BLK:(n + 1) * LRU_BLK], wax_ref[2 * g + n])
                    ga.append(gg[:, :LRU_BLK])
                    gx.append(gg[:, LRU_BLK:])
                v["ga", t, g] = jnp.concatenate(ga, axis=1) + ba[:, gs]
                v["gx", t, g] = jnp.concatenate(gx, axis=1) + bx[:, gs]
            plan.add(("gates", t, g), "M", 130, (("convl", t, g),), t_gates)

            def t_coef(t=t, g=g, gs=gs):
                v["a", t, g], v["u", t, g] = _lru_coeffs(v["xlc", t, g], v["ga", t, g], v["gx", t, g],
                                                         lam[:, gs])
            plan.add(("coef", t, g), "V", 280, (("gates", t, g),), t_coef)

            def t_scan(t=t, g=g):
                v["yl", t, g] = _scan_interleaved(v["a", t, g], v["u", t, g], v["h", t - 1, g])
                v["h", t, g] = v["yl", t, g][last:last + 1, :]
            plan.add(("scan", t, g), "V", 120, (("coef", t, g),) + dep_prev("scan", g), t_scan)

            h = g

            def t_convm(t=t, h=h, gs=gs):
                xm_c, v["mconv", t, h] = _conv_interleaved(v["mconv", t - 1, h], v["xm", t, h],
                                                           mcw[:, gs], mcb[:, gs])
                v["xcb", t, h] = (xm_c * _sigmoid(xm_c)).astype(BF)
            plan.add(("convm", t, h), "V", 210, (("xm", h),) + dep_prev("convm", h), t_convm)

            def t_qkv(t=t, h=h):
                v["q", t, h] = _dot(v["xcb", t, h], wq_ref[h]) * (M_HD ** -0.5)
                v["k", t, h] = _dot(v["xcb", t, h], wk_ref[h])
                vv = _dot(v["xm", t, h].astype(BF), wv_ref[h])
                v["qb", t, h], v["kb", t, h], v["vb", t, h] = (
                    v["q", t, h].astype(BF), v["k", t, h].astype(BF), vv.astype(BF))
            plan.add(("qkv", t, h), "M", 200, (("convm", t, h),), t_qkv)

            def t_qk(t=t, h=h):
                v["qk", t, h] = _dot_nt(v["qb", t, h], v["kb", t, h])
            plan.add(("qk", t, h), "M", 64, (("qkv", t, h),), t_qk)

            def t_sp(t=t, h=h):
                gcol, grow = v["gcol", t], v["grow", t]
                b_col = gcol[:, M_HEADS + h:M_HEADS + h + 1]
                ig_row, b_row = grow[h:h + 1, :], grow[M_HEADS + h:M_HEADS + h + 1, :]
                dlog = jnp.where(v["tri", t], b_col - b_row + ig_row, MASKED)
                m_inter = b_col + v["m", t - 1, h]
                m_t = jnp.maximum(m_inter, jnp.max(dlog, axis=1, keepdims=True))
                s = v["qk", t, h] * jnp.exp(dlog - m_t)
                v["m_t", t, h], v["sc", t, h] = m_t, jnp.exp(m_inter - m_t)
                v["ssum", t, h] = jnp.sum(s, axis=1, keepdims=True)
                v["sb", t, h] = s.astype(BF)
                v["m", t, h] = m_t[last:last + 1, :]
            plan.add(("sp", t, h), "V", 200, (("qk", t, h), ("gcum", t)) + dep_prev("sp", h), t_sp)

            def t_sv(t=t, h=h):
                v["sv", t, h] = _dot(v["sb", t, h], v["vb", t, h])
                v["qc", t, h] = _dot(v["qb", t, h], v["c", t - 1, h].astype(BF))
            plan.add(("sv", t, h), "M", 130, (("sp", t, h),) + dep_prev("cnew", h), t_sv)

            def t_hn(t=t, h=h, gs=gs):
                sc, m_t = v["sc", t, h], v["m_t", t, h]
                num = v["sv", t, h] + sc * v["qc", t, h]
                den = v["ssum", t, h] + sc * jnp.sum(v["q", t, h] * v["n", t - 1, h], axis=1,
                                                     keepdims=True)
                hh = num / jnp.maximum(jnp.abs(den), jnp.exp(-m_t))
                v["hn", t, h] = (hh * lax.rsqrt(jnp.mean(hh * hh, axis=-1, keepdims=True) + EPS)
                                 * mg[:, gs])
            plan.add(("hn", t, h), "V", 170, (("sv", t, h),) + dep_prev("kw", h), t_hn)

            def t_kw(t=t, h=h):
                gcol = v["gcol", t]
                ig_col, b_col = gcol[:, h:h + 1], gcol[:, M_HEADS + h:M_HEADS + h + 1]
                m_last = v["m", t, h]
                b_last = b_col[last:last + 1, :]
                kw = v["k", t, h] * jnp.exp(b_last - b_col + ig_col - m_last)
                v["dec", t, h] = jnp.exp(b_last + v["m", t - 1, h] - m_last)
                v["n", t, h] = v["dec", t, h] * v["n", t - 1, h] + jnp.sum(kw, axis=0, keepdims=True)
                v["kwb", t, h] = kw.astype(BF)
            plan.add(("kw", t, h), "V", 80, (("sp", t, h),) + dep_prev("kw", h), t_kw)

            def t_ckv(t=t, h=h):
                v["ckv", t, h] = _dot_tn(v["kwb", t, h], v["vb", t, h])
            plan.add(("ckv", t, h), "M", 64, (("kw", t, h),), t_ckv)

            def t_cnew(t=t, h=h):
                v["c", t, h] = v["dec", t, h] * v["c", t - 1, h] + v["ckv", t, h]
            plan.add(("cnew", t, h), "V", 40, (("ckv", t, h),) + dep_prev("cnew", h), t_cnew)

            def t_sig(t=t, h=h):
                v["gate_l", t, h] = _sigmoid(v["gl", t, h]) * v["yl", t, h]
                v["gate_m", t, h] = _sigmoid(v["gm", t, h]) * _sigmoid(v["om", t, h])
            plan.add(("sig", t, h), "V", 180,
                     (("gl", h), ("gm", h), ("om", h), ("scan", t, h)), t_sig)

            def t_mrg(t=t, h=h):
                v["mrg", t, h] = (v["gate_l", t, h] + v["gate_m", t, h] * v["hn", t, h]).astype(BF)
            plan.add(("mrg", t, h), "V", 40, (("sig", t, h), ("hn", t, h)), t_mrg)

            def t_pt(t=t, h=h):
                v["mrg_t", t, h] = _dot(permt_ref[...], v["mrg", t, h]).astype(BF)
            plan.add(("pt", t, h), "M", 64, (("mrg", t, h),), t_pt)

            def t_wo(t=t, h=h, gs=gs):
                part = _dot(v["mrg_t", t, h], wout_ref[gs, :])
                v["out", t] = part if h == 0 else v["out", t] + part
            plan.add(("wo", t, h), "M", 260, (("pt", t, h),) + ((("wo", t, h - 1),) if h else ()),
                     t_wo)

        def t_fin(t=t):
            x1_ref[0, t * tt:(t + 1) * tt, :] = v["x", t] + v["out", t]
        plan.add(("fin", t), "V", 70, (("wo", t, M_HEADS - 1),), t_fin)

    def t_state():
        e = SUB_TILES - 1
        heads = range(M_HEADS)
        lconv_ref[0] = jnp.concatenate([v["lconv", e, g] for g in heads], axis=1)
        mconv_ref[0] = jnp.concatenate([v["mconv", e, g] for g in heads], axis=1)
        h_ref[0] = jnp.concatenate([v["h", e, g] for g in heads], axis=1)
        for h in heads:
            c_ref[0, h] = v["c", e, h]
        n_ref[0] = jnp.concatenate([v["n", e, h] for h in heads], axis=0)
        m_ref[0] = jnp.concatenate([v["m", e, h] for h in heads], axis=1)
    plan.add("state", "V", 60, tuple(("cnew", SUB_TILES - 1, h) for h in range(M_HEADS))
             + tuple(("scan", SUB_TILES - 1, h) for h in range(M_HEADS))
             + tuple(("fin", t) for t in range(SUB_TILES)), t_state)

    plan.run()


def _split_kernel(wt_ref, gt_ref, wbr_ref, wmg_ref, wif_ref):
    j = pl.program_id(0)
    piece = wt_ref[...].T.astype(BF)

    @pl.when(j < BRANCH_STEPS)
    def _():
        wbr_ref[...] = piece

    @pl.when(j >= BRANCH_STEPS)
    def _():
        wmg_ref[...] = piece

    @pl.when(j == 0)
    def _():
        gates = gt_ref[...].T
        lane = lax.broadcasted_iota(jnp.int32, gates.shape, 1)
        wif_ref[...] = jnp.where(lane < N_GATES, gates, 0.0).astype(BF)


def _ffn_tail(x1, p, gffn_ref, wg_ref, wu_ref, wd_ref, gple_ref, wpg_ref, wple_ref, gfin_ref):
    xnb = _rms(x1, gffn_ref[...]).astype(BF)
    x2 = x1
    for c in range(D_FF // FF_CHUNK):
        cs = slice(c * FF_CHUNK, (c + 1) * FF_CHUNK)
        hg = _dot(xnb, wg_ref[:, cs])
        hu = _dot(xnb, wu_ref[:, cs])
        act = (hg * _sigmoid(hg) * hu).astype(BF)
        x2 = x2 + _dot(act, wd_ref[cs, :])
    gate = _sigmoid(_dot(_rms(x2, gple_ref[...]).astype(BF), wpg_ref[...]))
    x3 = x2 + gate * _dot(p.astype(BF), wple_ref[...])
    return _rms(x3, gfin_ref[...])


def _ffn_kernel(sc_ref, x1_ref, p_ref, q_ref, kw_ref, v_ref, c_ref, gffn_ref, wg_ref, wu_ref,
                wd_ref, gple_ref, wpg_ref, wple_ref, gfin_ref, y_ref, qc_out, c_out):
    tt = FFN_ROWS
    v = {}
    plan = _Plan()

    for t in range(FFN_SUB):
        rows = slice(t * tt, (t + 1) * tt)

        def t_norm(t=t, rows=rows):
            v["x", t] = x1_ref[rows, :]
            v["xnb", t] = _rms(v["x", t], gffn_ref[...]).astype(BF)
        plan.add(("norm", t), "V", 650, (), t_norm)

        for c in range(D_FF // FF_CHUNK):
            cs = slice(c * FF_CHUNK, (c + 1) * FF_CHUNK)

            def t_gu(t=t, c=c, cs=cs):
                v["hg", t, c] = _dot(v["xnb", t], wg_ref[:, cs])
                v["hu", t, c] = _dot(v["xnb", t], wu_ref[:, cs])
            plan.add(("gu", t, c), "M", 512, (("norm", t),), t_gu)

            def t_act(t=t, c=c):
                hg = v["hg", t, c]
                v["act", t, c] = (hg * _sigmoid(hg) * v["hu", t, c]).astype(BF)
            plan.add(("act", t, c), "V", 110, (("gu", t, c),), t_act)

            def t_dn(t=t, c=c, cs=cs):
                prev = v["x", t] if c == 0 else v["x2", t]
                v["x2", t] = prev + _dot(v["act", t, c], wd_ref[cs, :])
            plan.add(("dn", t, c), "M", 300, (("act", t, c),) + ((("dn", t, c - 1),) if c else ()),
                     t_dn)

        last_dn = ("dn", t, D_FF // FF_CHUNK - 1)

        def t_norm2(t=t):
            v["xgb", t] = _rms(v["x2", t], gple_ref[...]).astype(BF)
        plan.add(("norm2", t), "V", 650, (last_dn,), t_norm2)

        def t_ple(t=t, rows=rows):
            v["pe", t] = _dot(p_ref[rows, :].astype(BF), wple_ref[...])
        plan.add(("ple", t), "M", 300, (), t_ple)

        def t_wpg(t=t):
            v["gpre", t] = _dot(v["xgb", t], wpg_ref[...])
        plan.add(("wpg", t), "M", 1024, (("norm2", t),), t_wpg)

        def t_fin(t=t, rows=rows):
            x3 = v["x2", t] + _sigmoid(v["gpre", t]) * v["pe", t]
            y_ref[rows, :] = _rms(x3, gfin_ref[...])
        plan.add(("fin", t), "V", 1000, (("wpg", t), ("ple", t)), t_fin)

    base = pl.program_id(0) * STATE_BATCH
    heads = [slice(h * M_HD, (h + 1) * M_HD) for h in range(M_HEADS)]
    for bb in range(STATE_BATCH):
        def t_cols(bb=bb):
            row = pl.ds(base + bb, 1)
            q, kw = q_ref[row, :], kw_ref[row, :]
            v["cols", bb] = jnp.concatenate([q[:, hs] for hs in heads] + [kw[:, hs] for hs in heads],
                                            axis=0).T
            v["vrow", bb] = v_ref[row, :]
        plan.add(("cols", bb), "V", 40, (), t_cols)

        for h, hs in enumerate(heads):
            def t_pair(bb=bb, h=h, hs=hs):
                cols = v["cols", bb]
                q_col, kw_col = cols[:, h:h + 1], cols[:, M_HEADS + h:M_HEADS + h + 1]
                c_prev = c_ref[bb, h]
                qc_out[pl.ds(base + bb, 1), hs] = jnp.sum(q_col * c_prev, axis=0, keepdims=True)
                c_out[bb, h] = sc_ref[base + bb, h] * c_prev + kw_col * v["vrow", bb][:, hs]
            plan.add(("pair", bb, h), "V", 170, (("cols", bb),), t_pair)

    plan.run()


def _front_kernel(x_ref, lbuf_ref, h0_ref, mbuf_ref, gmix_ref, wbr_ref, wmg_ref, wif_ref, bif_ref,
                  lcw_ref, lcb_ref, wax_ref, ba_ref, bx_ref, lam_ref, mcw_ref, mcb_ref,
                  wq_ref, wk_ref, wv_ref, m0_ref, n0_ref,
                  q_ref, kw_ref, v_ref, sc_ref, hv_ref, hc_ref, n_out, m_out,
                  yl_ref, gm_ref, lbuf_out, h_out, mbuf_out):
    xnb = _rms(x_ref[...], gmix_ref[...]).astype(BF)

    def per_head(cols):
        return jnp.concatenate([jnp.broadcast_to(cols[:, h:h + 1], (cols.shape[0], M_HD))
                                for h in range(M_HEADS)], axis=1)

    def head_sums(a):
        return jnp.concatenate([jnp.sum(a[:, h * M_HD:(h + 1) * M_HD], axis=1, keepdims=True)
                                for h in range(M_HEADS)], axis=1)

    def conv_step(buf_ref, buf_out, x_new, w_ref, b_ref):
        out = b_ref[...] + x_new * w_ref[CONV_W - 1:CONV_W, :]
        for j in range(CONV_W - 1):
            out = out + buf_ref[j] * w_ref[j:j + 1, :]
        for j in range(CONV_W - 2):
            buf_out[j] = buf_ref[j + 1]
        buf_out[CONV_W - 2] = x_new
        return out

    x_l = _dot(xnb, wbr_ref[:, 0:D_MODEL])
    xl_c = conv_step(lbuf_ref, lbuf_out, x_l, lcw_ref, lcb_ref)
    ga, gx = _lru_gates(xl_c.astype(BF), wax_ref, ba_ref[...], bx_ref[...])
    a, u = _lru_coeffs(xl_c, ga, gx, lam_ref[...])
    y_l = a * h0_ref[...] + u
    h_out[...] = y_l
    g_l = _dot(xnb, wmg_ref[:, 0:D_MODEL])
    yl_ref[...] = _sigmoid(g_l) * y_l

    x_m = _dot(xnb, wbr_ref[:, D_MODEL:2 * D_MODEL])
    xm_c = conv_step(mbuf_ref, mbuf_out, x_m, mcw_ref, mcb_ref)
    xcb = (xm_c * _sigmoid(xm_c)).astype(BF)
    xmb = x_m.astype(BF)
    q, k, v = [], [], []
    for h in range(M_HEADS):
        hs = slice(h * M_HD, (h + 1) * M_HD)
        q.append(_dot(xcb[:, hs], wq_ref[h]) * (M_HD ** -0.5))
        k.append(_dot(xcb[:, hs], wk_ref[h]))
        v.append(_dot(xmb[:, hs], wv_ref[h]))
    q, k, v = (jnp.concatenate(a, axis=1) for a in (q, k, v))
    o_m = _dot(xnb, wbr_ref[:, 2 * D_MODEL:3 * D_MODEL])
    g_m = _dot(xnb, wmg_ref[:, D_MODEL:2 * D_MODEL])
    gm_ref[...] = _sigmoid(g_m) * _sigmoid(o_m)

    pre = _dot(xnb, wif_ref[...]) + bif_ref[...]
    ig = pre[:, 0:M_HEADS]
    lf = -_softplus(-pre[:, M_HEADS:N_GATES])
    n_prev = n0_ref[...]
    m_inter = lf + m0_ref[...]
    m_t = jnp.maximum(m_inter, ig)
    wk = jnp.exp(ig - m_t)
    sc = jnp.exp(m_inter - m_t)
    s = head_sums(q * k) * wk
    den = s + sc * head_sums(q * n_prev)
    rden = 1.0 / jnp.maximum(jnp.abs(den), jnp.exp(-m_t))
    wk_d, sc_d = per_head(wk), per_head(sc)
    hv_ref[...] = per_head(s * rden) * v
    hc_ref[...] = per_head(sc * rden)
    q_ref[...] = q
    kw_ref[...] = wk_d * k
    v_ref[...] = v
    sc_ref[...] = sc
    n_out[...] = sc_d * n_prev + wk_d * k
    m_out[...] = m_t


def _back_kernel(x_ref, qc_ref, hv_ref, hc_ref, yl_ref, gm_ref, p_ref, mg_ref, wout_ref, gffn_ref,
                 wg_ref, wu_ref, wd_ref, gple_ref, wpg_ref, wple_ref, gfin_ref, y_ref, mrg_ref):
    for h in range(M_HEADS):
        hs = slice(h * M_HD, (h + 1) * M_HD)
        hh = hv_ref[:, hs] + hc_ref[:, hs] * qc_ref[:, hs]
        hn = hh * lax.rsqrt(jnp.mean(hh * hh, axis=-1, keepdims=True) + EPS) * mg_ref[:, hs]
        mrg_ref[:, hs] = (yl_ref[:, hs] + gm_ref[:, hs] * hn).astype(BF)
    x1 = x_ref[...] + _dot(mrg_ref[...], wout_ref[...])
    y_ref[...] = _ffn_tail(x1, p_ref[...], gffn_ref, wg_ref, wu_ref, wd_ref, gple_ref, wpg_ref,
                           wple_ref, gfin_ref)


def _resident(shape):
    nd = len(shape)
    return pl.BlockSpec(shape, lambda *_: (0,) * nd, pipeline_mode=pl.Buffered(1))


def _params(n_axes):
    return pltpu.CompilerParams(dimension_semantics=("arbitrary",) * n_axes,
                                vmem_limit_bytes=VMEM_LIMIT)


def _interleave_matrix():
    r = np.arange(SEQ_TILE)
    perm = np.zeros((SEQ_TILE, SEQ_TILE), np.float32)
    perm[r, (r % SUBLANES) * SEG + r // SUBLANES] = 1.0
    return perm


def kernel(x_prompt, x_sample, state_lru_conv, state_lru_h, state_mlstm_conv, state_mlstm_C, state_mlstm_n, state_mlstm_m, p_prompt, p_sample, norm_mix_g, w_in, b_gates, lru_conv_w, lru_conv_b, lru_w_a, lru_b_a, lru_w_x, lru_b_x, lru_lambda, mlstm_conv_w, mlstm_conv_b, w_q, w_k, w_v, mlstm_norm_g, w_out, norm_ffn_g, w_ffn_gate, w_ffn_up, w_ffn_down, norm_ple_g, w_ple_gate, w_ple, final_norm_g):
    assert w_in.shape[0] == 1, "single-layer trunk"
    B, T, _ = x_prompt.shape
    S = x_sample.shape[0]
    step = SEQ_TILE * SUB_TILES
    assert T % step == 0 and (B * T) % FFN_TILE == 0 and x_sample.shape[1] == 1
    assert S == STATE_BATCH * ((B * T) // FFN_TILE), "one state batch per ffn grid step"

    sds = jax.ShapeDtypeStruct
    assert w_in.shape[1:] == (D_MODEL, 5 * D_MODEL + N_GATES)
    wt = jnp.swapaxes(w_in[0], 0, 1)
    g0 = 3 * D_MODEL
    src_row = lambda j: pl.multiple_of(
        jnp.where(j < BRANCH_STEPS, j * PREP_COLS, g0 + N_GATES + (j - BRANCH_STEPS) * PREP_COLS),
        SUBLANES)
    slab = lambda rows, index: pl.BlockSpec((pl.Element(rows), pl.Element(D_MODEL)), index)
    cols = lambda index: pl.BlockSpec((D_MODEL, PREP_COLS), index)
    wbr, wmg, wif = pl.pallas_call(
        _split_kernel,
        grid=(5 * D_MODEL // PREP_COLS,),
        in_specs=[slab(PREP_COLS, lambda j: (src_row(j), 0)), slab(GATE_PAD, lambda j: (g0, 0))],
        out_specs=[cols(lambda j: (0, jnp.minimum(j, BRANCH_STEPS - 1))),
                   cols(lambda j: (0, jnp.maximum(j - BRANCH_STEPS, 0))),
                   pl.BlockSpec((D_MODEL, GATE_PAD), lambda j: (0, 0))],
        out_shape=[sds((D_MODEL, 3 * D_MODEL), BF), sds((D_MODEL, 2 * D_MODEL), BF),
                   sds((D_MODEL, GATE_PAD), BF)],
        compiler_params=_params(1),
        name="split_w_in",
    )(wt, wt)
    bif = jnp.pad(b_gates[0], (0, GATE_PAD - N_GATES)).reshape(1, GATE_PAD)
    wax = jnp.concatenate([lru_w_a[0], lru_w_x[0]], axis=2).astype(BF)
    row = lambda a: a.reshape(1, -1)
    gmix, ba, bx, lam = row(norm_mix_g[0]), row(lru_b_a[0]), row(lru_b_x[0]), row(lru_lambda[0])
    lcw, lcb = lru_conv_w[0], row(lru_conv_b[0])
    mcw, mcb = mlstm_conv_w[0], row(mlstm_conv_b[0])
    wq, wk, wv = w_q[0].astype(BF), w_k[0].astype(BF), w_v[0].astype(BF)
    mg = row(mlstm_norm_g[0])
    wout = w_out[0].astype(BF)
    gffn, gple, gfin = row(norm_ffn_g[0]), row(norm_ple_g[0]), row(final_norm_g)
    wg, wu, wd = w_ffn_gate[0].astype(BF), w_ffn_up[0].astype(BF), w_ffn_down[0].astype(BF)
    wpg, wple = w_ple_gate[0].astype(BF), w_ple[0].astype(BF)
    perm_np = _interleave_matrix()
    perm, perm_t = jnp.asarray(perm_np, BF), jnp.asarray(perm_np.T, BF)

    front_w = (gmix, wbr, wmg, wif, bif, lcw, lcb, wax, ba, bx, lam, mcw, mcb, wq, wk, wv)
    mixer_w = (perm, perm_t) + front_w + (mg, wout)
    ffn_w = (gffn, wg, wu, wd, gple, wpg, wple, gfin)

    nt = T // step
    x1, p_lconv, p_h, p_mconv, p_c, p_n, p_m = pl.pallas_call(
        _mixer_kernel,
        grid=(B, nt),
        in_specs=[pl.BlockSpec((1, step, D_MODEL), lambda b, t: (b, t, 0))]
                 + [_resident(w.shape) for w in mixer_w],
        out_specs=[
            pl.BlockSpec((1, step, D_MODEL), lambda b, t: (b, t, 0)),
            pl.BlockSpec((1, CONV_W - 1, D_MODEL), lambda b, t: (b, 0, 0)),
            pl.BlockSpec((1, 1, D_MODEL), lambda b, t: (b, 0, 0)),
            pl.BlockSpec((1, CONV_W - 1, D_MODEL), lambda b, t: (b, 0, 0)),
            pl.BlockSpec((1, M_HEADS, M_HD, M_HD), lambda b, t: (b, 0, 0, 0)),
            pl.BlockSpec((1, M_HEADS, M_HD), lambda b, t: (b, 0, 0)),
            pl.BlockSpec((1, 1, M_HEADS), lambda b, t: (b, 0, 0)),
        ],
        out_shape=[
            sds((B, T, D_MODEL), F32),
            sds((B, CONV_W - 1, D_MODEL), F32),
            sds((B, 1, D_MODEL), F32),
            sds((B, CONV_W - 1, D_MODEL), F32),
            sds((B, M_HEADS, M_HD, M_HD), F32),
            sds((B, M_HEADS, M_HD), F32),
            sds((B, 1, M_HEADS), F32),
        ],
        compiler_params=_params(2),
        name="prompt_mixer",
    )(x_prompt, *mixer_w)

    xs = x_sample.reshape(S, D_MODEL)
    lbuf = jnp.swapaxes(state_lru_conv[0], 0, 1)
    mbuf = jnp.swapaxes(state_mlstm_conv[0], 0, 1)
    tok = sds((S, D_MODEL), F32)
    buf = sds((CONV_W - 1, S, D_MODEL), F32)
    per_head = sds((S, M_HEADS), F32)
    q, kw, v, sc, hv, hc, s_n, s_m, yl, gm, s_lbuf, s_h, s_mbuf = pl.pallas_call(
        _front_kernel,
        out_shape=[tok, tok, tok, per_head, tok, tok, tok, per_head, tok, tok, buf, tok, buf],
        compiler_params=pltpu.CompilerParams(vmem_limit_bytes=VMEM_LIMIT),
        name="sample_front",
    )(xs, lbuf, state_lru_h[0], mbuf, *front_w, state_mlstm_m[0],
      state_mlstm_n[0].reshape(S, D_MODEL))

    n_tok = B * T
    tile = lambda width: pl.BlockSpec((FFN_TILE, width), lambda i: (i, 0))
    c_spec = pl.BlockSpec((STATE_BATCH, M_HEADS, M_HD, M_HD), lambda i: (i, 0, 0, 0))
    y_prompt, qc, s_c = pl.pallas_call(
        _ffn_kernel,
        grid=(n_tok // FFN_TILE,),
        in_specs=[pl.BlockSpec(memory_space=pltpu.SMEM), tile(D_MODEL), tile(P_DIM),
                  _resident(tok.shape), _resident(tok.shape), _resident(tok.shape), c_spec]
                 + [_resident(w.shape) for w in ffn_w],
        out_specs=[tile(D_MODEL), pl.BlockSpec(tok.shape, lambda i: (0, 0)), c_spec],
        out_shape=[sds((n_tok, D_MODEL), F32), tok, sds((S, M_HEADS, M_HD, M_HD), F32)],
        compiler_params=_params(1),
        name="prompt_ffn",
    )(sc, x1.reshape(n_tok, D_MODEL), p_prompt[0].reshape(n_tok, P_DIM), q, kw, v,
      state_mlstm_C[0], *ffn_w)

    y_sample = pl.pallas_call(
        _back_kernel,
        out_shape=tok,
        scratch_shapes=[pltpu.VMEM((S, D_MODEL), BF)],
        compiler_params=pltpu.CompilerParams(vmem_limit_bytes=VMEM_LIMIT),
        name="sample_back",
    )(xs, qc, hv, hc, yl, gm, p_sample[0].reshape(S, P_DIM), mg, wout, *ffn_w)

    lead = lambda a: a[None]
    return (y_prompt.reshape(B, T, D_MODEL), y_sample.reshape(S, 1, D_MODEL),
            lead(p_lconv), p_h.reshape(1, B, D_MODEL), lead(p_mconv), lead(p_c), lead(p_n),
            p_m.reshape(1, B, M_HEADS),
            lead(jnp.swapaxes(s_lbuf, 0, 1)), lead(s_h), lead(jnp.swapaxes(s_mbuf, 0, 1)),
            lead(s_c), s_n.reshape(1, S, M_HEADS, M_HD), lead(s_m))
```

```python
import numpy as np

import jax
import jax.numpy as jnp
from jax import lax
from jax.experimental import pallas as pl
from jax.experimental.pallas import tpu as pltpu

D_MODEL = 1024
M_HEADS = 4
M_HD = D_MODEL // M_HEADS
LRU_BLOCKS = 8
LRU_BLK = D_MODEL // LRU_BLOCKS
LRU_C = 8.0
CONV_W = 4
D_FF = 2816
P_DIM = 256
EPS = 1e-6
M_INIT = -1e30
NEG_LOG2E = -1.4426950408889634
MASKED = -1e30
N_GATES = 2 * M_HEADS
GATE_PAD = 128
SUBLANES = 8
SEQ_TILE = 256
SEG = SEQ_TILE // SUBLANES
SUB_TILES = 2
FFN_TILE = 512
FFN_ROWS = 256
FFN_SUB = FFN_TILE // FFN_ROWS
FF_CHUNK = 256
STATE_BATCH = 4
PREP_COLS = 512
BRANCH_STEPS = 3 * D_MODEL // PREP_COLS
VMEM_LIMIT = 56 * 1024 * 1024

BF = jnp.bfloat16
F32 = jnp.float32


def _dot(a, b):
    return jnp.dot(a, b, preferred_element_type=F32)


def _dot_nt(a, b):
    return lax.dot_general(a, b, (((1,), (1,)), ((), ())), preferred_element_type=F32)


def _dot_tn(a, b):
    return lax.dot_general(a, b, (((0,), (0,)), ((), ())), preferred_element_type=F32)


def _sigmoid(x):
    return 1.0 / (1.0 + jnp.exp2(x * NEG_LOG2E))


def _softplus(x):
    return jnp.maximum(x, 0.0) + jnp.log1p(jnp.exp(-jnp.abs(x)))


def _rms(x, g):
    return x * lax.rsqrt(jnp.mean(x * x, axis=-1, keepdims=True) + EPS) * g


def _group(x, i):
    return x[i * SUBLANES:(i + 1) * SUBLANES, :]


def _lru_coeffs(xc, ga, gx, lam):
    r = _sigmoid(ga)
    ig = _sigmoid(gx)
    log_a = -LRU_C * r * _softplus(-lam)
    a = jnp.exp(log_a)
    om = 1.0 - a * a
    root = jnp.where(om > 0.0, om * lax.rsqrt(om), 0.0)
    return a, root * ig * xc


def _lru_gates(xcb, wax_ref, ba, bx):
    ga, gx = [], []
    for n in range(LRU_BLOCKS):
        g = _dot(xcb[:, n * LRU_BLK:(n + 1) * LRU_BLK], wax_ref[n])
        ga.append(g[:, :LRU_BLK])
        gx.append(g[:, LRU_BLK:])
    return jnp.concatenate(ga, axis=1) + ba, jnp.concatenate(gx, axis=1) + bx


def _scan_interleaved(a, u, h0):
    prods, sums = [], []
    p = s = None
    for i in range(SEG):
        ai, ui = _group(a, i), _group(u, i)
        p, s = (ai, ui) if i == 0 else (ai * p, ai * s + ui)
        prods.append(p)
        sums.append(s)
    c = h0
    starts = [c]
    for j in range(SUBLANES - 1):
        c = p[j:j + 1, :] * c + s[j:j + 1, :]
        starts.append(c)
    start = jnp.concatenate(starts, axis=0)
    return jnp.concatenate([sums[i] + prods[i] * start for i in range(SEG)], axis=0)


def _cumsum_interleaved(x):
    acc = []
    run = None
    for i in range(SEG):
        run = _group(x, i) if i == 0 else run + _group(x, i)
        acc.append(run)
    sub = lax.broadcasted_iota(jnp.int32, run.shape, 0)
    inc = run
    s = 1
    while s < SUBLANES:
        inc = inc + jnp.where(sub >= s, pltpu.roll(inc, s, 0), 0.0)
        s *= 2
    before = inc - run
    return jnp.concatenate([r + before for r in acc], axis=0)


def _conv_interleaved(carry, x_new, w, b):
    sub = lax.broadcasted_iota(jnp.int32, (SUBLANES, x_new.shape[1]), 0)
    head, new = [], []
    for r in range(CONV_W - 1):
        cur = _group(x_new, SEG - (CONV_W - 1) + r)
        head.append(jnp.where(sub == 0, carry[r:r + 1, :], pltpu.roll(cur, 1, 0)))
        new.append(cur[SUBLANES - 1:SUBLANES, :])
    ext = jnp.concatenate(head + [x_new], axis=0)
    out = b
    for j in range(CONV_W):
        out = out + ext[j * SUBLANES:j * SUBLANES + SEQ_TILE, :] * w[j:j + 1, :]
    return out, jnp.concatenate(new, axis=0)


def _time_of_row(r):
    return (r & (SUBLANES - 1)) * SEG + (r >> (SUBLANES.bit_length() - 1))


RESULT_DELAY = {"M": 250, "V": 40}


class _Plan:
    def __init__(self):
        self.tasks = {}

    def add(self, name, unit, cost, deps, fn):
        self.tasks[name] = (unit, cost, tuple(deps), fn)

    def order(self):
        succ = {n: [] for n in self.tasks}
        for n, (_, _, deps, _) in self.tasks.items():
            for p in deps:
                succ[p].append(n)
        tail = {}

        def path(n):
            if n not in tail:
                tail[n] = self.tasks[n][1] + max([path(s) for s in succ[n]], default=0)
            return tail[n]

        free = {"M": 0, "V": 0}
        done, order, left = {}, [], list(self.tasks)
        while left:
            ready = [n for n in left if all(p in done for p in self.tasks[n][2])]

            def start(n):
                unit, _, deps, _ = self.tasks[n]
                return max([free[unit]] + [done[p] for p in deps])

            n = min(ready, key=lambda n: (start(n), -path(n)))
            unit, cost, _, _ = self.tasks[n]
            st = start(n)
            free[unit] = st + cost
            done[n] = st + cost + RESULT_DELAY[unit]
            order.append((st, len(order), n))
            left.remove(n)
        return [n for _, _, n in sorted(order)]

    def run(self):
        for n in self.order():
            self.tasks[n][3]()


def _mixer_kernel(x_ref, perm_ref, permt_ref, gmix_ref, wbr_ref, wmg_ref, wif_ref, bif_ref, lcw_ref,
                  lcb_ref, wax_ref, ba_ref, bx_ref, lam_ref, mcw_ref, mcb_ref, wq_ref, wk_ref,
                  wv_ref, mg_ref, wout_ref,
                  x1_ref, lconv_ref, h_ref, mconv_ref, c_ref, n_ref, m_ref):
    tt = SEQ_TILE
    last = tt - 1

    @pl.when(pl.program_id(1) == 0)
    def _():
        lconv_ref[...] = jnp.zeros(lconv_ref.shape, F32)
        mconv_ref[...] = jnp.zeros(mconv_ref.shape, F32)
        h_ref[...] = jnp.zeros(h_ref.shape, F32)
        c_ref[...] = jnp.zeros(c_ref.shape, F32)
        n_ref[...] = jnp.zeros(n_ref.shape, F32)
        m_ref[...] = jnp.full(m_ref.shape, M_INIT, F32)

    lcw, lcb, mcw, mcb = lcw_ref[...], lcb_ref[...], mcw_ref[...], mcb_ref[...]
    ba, bx, lam, mg = ba_ref[...], bx_ref[...], lam_ref[...], mg_ref[...]

    v = {}
    for g in range(M_HEADS):
        gs = slice(g * M_HD, (g + 1) * M_HD)
        v["lconv", -1, g], v["mconv", -1, g] = lconv_ref[0, :, gs], mconv_ref[0, :, gs]
        v["h", -1, g] = h_ref[0, :, gs]
        v["c", -1, g], v["n", -1, g] = c_ref[0, g], n_ref[0, g:g + 1, :]
        v["m", -1, g] = m_ref[0, :, g:g + 1]
    plan = _Plan()

    for t in range(SUB_TILES):
        dep_prev = (lambda name, g, t=t: ((name, t - 1, g),)) if t else (lambda name, g: ())

        def t_norm(t=t):
            v["x", t] = x_ref[0, t * tt:(t + 1) * tt, :]
            v["xn_t", t] = _rms(v["x", t], gmix_ref[...]).astype(BF)
        plan.add(("norm", t), "V", 650, (), t_norm)

        def t_perm(t=t):
            v["xnb", t] = _dot(perm_ref[...], v["xn_t", t]).astype(BF)
        plan.add(("perm", t), "M", 260, (("norm", t),), t_perm)

        def proj(name, w_ref, col0, g, t=t):
            def run():
                v[name, t, g] = _dot(v["xnb", t], w_ref[:, col0 + g * M_HD:col0 + (g + 1) * M_HD])
            plan.add((name, t, g), "M", 260, (("perm", t),), run)

        def t_wif(t=t):
            v["pre", t] = _dot(v["xnb", t], wif_ref[...]) + bif_ref[...]
        plan.add(("wif", t), "M", 260, (("perm", t),), t_wif)

        def t_gcum(t=t):
            pre = v["pre", t]
            lane = lax.broadcasted_iota(jnp.int32, (tt, GATE_PAD), 1)
            v["gcol", t] = jnp.where(lane < M_HEADS, pre, _cumsum_interleaved(-_softplus(-pre)))
            v["grow", t] = v["gcol", t].T
            v["tri", t] = (_time_of_row(lax.broadcasted_iota(jnp.int32, (tt, 1), 0))
                           >= _time_of_row(lax.broadcasted_iota(jnp.int32, (1, tt), 1)))
        plan.add(("gcum", t), "V", 150, (("wif", t),), t_gcum)

        for g in range(M_HEADS):
            gs = slice(g * M_HD, (g + 1) * M_HD)
            for name, w_ref, col0 in (("xl", wbr_ref, 0), ("xm", wbr_ref, D_MODEL),
                                      ("om", wbr_ref, 2 * D_MODEL), ("gl", wmg_ref, 0),
                                      ("gm", wmg_ref, D_MODEL)):
                proj(name, w_ref, col0, g)

            def t_convl(t=t, g=g, gs=gs):
                v["xlc", t, g], v["lconv", t, g] = _conv_interleaved(
                    v["lconv", t - 1, g], v["xl", t, g], lcw[:, gs], lcb[:, gs])
            plan.add(("convl", t, g), "V", 135, (("xl", t, g),) + dep_prev("convl", g), t_convl)

            def t_gates(t=t, g=g, gs=gs):
                xlb = v["xlc", t, g].astype(BF)
                ga, gx = [], []
                for n in range(2):
                    gg = _dot(xlb[:, n * LRU_BLK:(n + 1) * LRU_BLK], wax_ref[2 * g + n])
                    ga.append(gg[:, :LRU_BLK])
                    gx.append(gg[:, LRU_BLK:])
                v["ga", t, g] = jnp.concatenate(ga, axis=1) + ba[:, gs]
                v["gx", t, g] = jnp.concatenate(gx, axis=1) + bx[:, gs]
            plan.add(("gates", t, g), "M", 130, (("convl", t, g),), t_gates)

            def t_coef(t=t, g=g, gs=gs):
                v["a", t, g], v["u", t, g] = _lru_coeffs(v["xlc", t, g], v["ga", t, g], v["gx", t, g],
                                                         lam[:, gs])
            plan.add(("coef", t, g), "V", 280, (("gates", t, g),), t_coef)

            def t_scan(t=t, g=g):
                v["yl", t, g] = _scan_interleaved(v["a", t, g], v["u", t, g], v["h", t - 1, g])
                v["h", t, g] = v["yl", t, g][last:last + 1, :]
            plan.add(("scan", t, g), "V", 120, (("coef", t, g),) + dep_prev("scan", g), t_scan)

            h = g

            def t_convm(t=t, h=h, gs=gs):
                xm_c, v["mconv", t, h] = _conv_interleaved(v["mconv", t - 1, h], v["xm", t, h],
                                                           mcw[:, gs], mcb[:, gs])
                v["xcb", t, h] = (xm_c * _sigmoid(xm_c)).astype(BF)
            plan.add(("convm", t, h), "V", 210, (("xm", t, h),) + dep_prev("convm", h), t_convm)

            def t_qkv(t=t, h=h):
                v["q", t, h] = _dot(v["xcb", t, h], wq_ref[h]) * (M_HD ** -0.5)
                v["k", t, h] = _dot(v["xcb", t, h], wk_ref[h])
                vv = _dot(v["xm", t, h].astype(BF), wv_ref[h])
                v["qb", t, h], v["kb", t, h], v["vb", t, h] = (
                    v["q", t, h].astype(BF), v["k", t, h].astype(BF), vv.astype(BF))
            plan.add(("qkv", t, h), "M", 200, (("convm", t, h),), t_qkv)

            def t_qk(t=t, h=h):
                v["qk", t, h] = _dot_nt(v["qb", t, h], v["kb", t, h])
            plan.add(("qk", t, h), "M", 64, (("qkv", t, h),), t_qk)

            def t_sp(t=t, h=h):
                gcol, grow = v["gcol", t], v["grow", t]
                b_col = gcol[:, M_HEADS + h:M_HEADS + h + 1]
                ig_row, b_row = grow[h:h + 1, :], grow[M_HEADS + h:M_HEADS + h + 1, :]
                dlog = jnp.where(v["tri", t], b_col - b_row + ig_row, MASKED)
                m_inter = b_col + v["m", t - 1, h]
                m_t = jnp.maximum(m_inter, jnp.max(dlog, axis=1, keepdims=True))
                s = v["qk", t, h] * jnp.exp(dlog - m_t)
                v["m_t", t, h], v["sc", t, h] = m_t, jnp.exp(m_inter - m_t)
                v["ssum", t, h] = jnp.sum(s, axis=1, keepdims=True)
                v["sb", t, h] = s.astype(BF)
                v["m", t, h] = m_t[last:last + 1, :]
            plan.add(("sp", t, h), "V", 200, (("qk", t, h), ("gcum", t)) + dep_prev("sp", h), t_sp)

            def t_sv(t=t, h=h):
                v["sv", t, h] = _dot(v["sb", t, h], v["vb", t, h])
                v["qc", t, h] = _dot(v["qb", t, h], v["c", t - 1, h].astype(BF))
            plan.add(("sv", t, h), "M", 130, (("sp", t, h),) + dep_prev("cnew", h), t_sv)

            def t_hn(t=t, h=h, gs=gs):
                sc, m_t = v["sc", t, h], v["m_t", t, h]
                num = v["sv", t, h] + sc * v["qc", t, h]
                den = v["ssum", t, h] + sc * jnp.sum(v["q", t, h] * v["n", t - 1, h], axis=1,
                                                     keepdims=True)
                hh = num / jnp.maximum(jnp.abs(den), jnp.exp(-m_t))
                v["hn", t, h] = (hh * lax.rsqrt(jnp.mean(hh * hh, axis=-1, keepdims=True) + EPS)
                                 * mg[:, gs])
            plan.add(("hn", t, h), "V", 170, (("sv", t, h),) + dep_prev("kw", h), t_hn)

            def t_kw(t=t, h=h):
                gcol = v["gcol", t]
                ig_col, b_col = gcol[:, h:h + 1], gcol[:, M_HEADS + h:M_HEADS + h + 1]
                m_last = v["m", t, h]
                b_last = b_col[last:last + 1, :]
                kw = v["k", t, h] * jnp.exp(b_last - b_col + ig_col - m_last)
                v["dec", t, h] = jnp.exp(b_last + v["m", t - 1, h] - m_last)
                v["n", t, h] = v["dec", t, h] * v["n", t - 1, h] + jnp.sum(kw, axis=0, keepdims=True)
                v["kwb", t, h] = kw.astype(BF)
            plan.add(("kw", t, h), "V", 80, (("sp", t, h),) + dep_prev("kw", h), t_kw)

            def t_ckv(t=t, h=h):
                v["ckv", t, h] = _dot_tn(v["kwb", t, h], v["vb", t, h])
            plan.add(("ckv", t, h), "M", 64, (("kw", t, h),), t_ckv)

            def t_cnew(t=t, h=h):
                v["c", t, h] = v["dec", t, h] * v["c", t - 1, h] + v["ckv", t, h]
            plan.add(("cnew", t, h), "V", 40, (("ckv", t, h),) + dep_prev("cnew", h), t_cnew)

            def t_sig(t=t, h=h):
                v["gate_l", t, h] = _sigmoid(v["gl", t, h]) * v["yl", t, h]
                v["gate_m", t, h] = _sigmoid(v["gm", t, h]) * _sigmoid(v["om", t, h])
            plan.add(("sig", t, h), "V", 180,
                     (("gl", t, h), ("gm", t, h), ("om", t, h), ("scan", t, h)), t_sig)

            def t_mrg(t=t, h=h):
                v["mrg", t, h] = (v["gate_l", t, h] + v["gate_m", t, h] * v["hn", t, h]).astype(BF)
            plan.add(("mrg", t, h), "V", 40, (("sig", t, h), ("hn", t, h)), t_mrg)

            def t_pt(t=t, h=h):
                v["mrg_t", t, h] = _dot(permt_ref[...], v["mrg", t, h]).astype(BF)
            plan.add(("pt", t, h), "M", 64, (("mrg", t, h),), t_pt)

            def t_wo(t=t, h=h, gs=gs):
                part = _dot(v["mrg_t", t, h], wout_ref[gs, :])
                v["out", t] = part if h == 0 else v["out", t] + part
            plan.add(("wo", t, h), "M", 260, (("pt", t, h),) + ((("wo", t, h - 1),) if h else ()),
                     t_wo)

        def t_fin(t=t):
            x1_ref[0, t * tt:(t + 1) * tt, :] = v["x", t] + v["out", t]
        plan.add(("fin", t), "V", 70, (("wo", t, M_HEADS - 1),), t_fin)

    def t_state():
        e = SUB_TILES - 1
        heads = range(M_HEADS)
        lconv_ref[0] = jnp.concatenate([v["lconv", e, g] for g in heads], axis=1)
        mconv_ref[0] = jnp.concatenate([v["mconv", e, g] for g in heads], axis=1)
        h_ref[0] = jnp.concatenate([v["h", e, g] for g in heads], axis=1)
        for h in heads:
            c_ref[0, h] = v["c", e, h]
        n_ref[0] = jnp.concatenate([v["n", e, h] for h in heads], axis=0)
        m_ref[0] = jnp.concatenate([v["m", e, h] for h in heads], axis=1)
    plan.add("state", "V", 60, tuple(("cnew", SUB_TILES - 1, h) for h in range(M_HEADS))
             + tuple(("scan", SUB_TILES - 1, h) for h in range(M_HEADS))
             + tuple(("fin", t) for t in range(SUB_TILES)), t_state)

    plan.run()


def _split_kernel(wt_ref, gt_ref, wbr_ref, wmg_ref, wif_ref):
    j = pl.program_id(0)
    piece = wt_ref[...].T.astype(BF)

    @pl.when(j < BRANCH_STEPS)
    def _():
        wbr_ref[...] = piece

    @pl.when(j >= BRANCH_STEPS)
    def _():
        wmg_ref[...] = piece

    @pl.when(j == 0)
    def _():
        gates = gt_ref[...].T
        lane = lax.broadcasted_iota(jnp.int32, gates.shape, 1)
        wif_ref[...] = jnp.where(lane < N_GATES, gates, 0.0).astype(BF)


def _ffn_tail(x1, p, gffn_ref, wg_ref, wu_ref, wd_ref, gple_ref, wpg_ref, wple_ref, gfin_ref):
    xnb = _rms(x1, gffn_ref[...]).astype(BF)
    x2 = x1
    for c in range(D_FF // FF_CHUNK):
        cs = slice(c * FF_CHUNK, (c + 1) * FF_CHUNK)
        hg = _dot(xnb, wg_ref[:, cs])
        hu = _dot(xnb, wu_ref[:, cs])
        act = (hg * _sigmoid(hg) * hu).astype(BF)
        x2 = x2 + _dot(act, wd_ref[cs, :])
    gate = _sigmoid(_dot(_rms(x2, gple_ref[...]).astype(BF), wpg_ref[...]))
    x3 = x2 + gate * _dot(p.astype(BF), wple_ref[...])
    return _rms(x3, gfin_ref[...])


def _ffn_kernel(sc_ref, x1_ref, p_ref, q_ref, kw_ref, v_ref, c_ref, gffn_ref, wg_ref, wu_ref,
                wd_ref, gple_ref, wpg_ref, wple_ref, gfin_ref, y_ref, qc_out, c_out):
    tt = FFN_ROWS
    v = {}
    plan = _Plan()

    for t in range(FFN_SUB):
        rows = slice(t * tt, (t + 1) * tt)

        def t_norm(t=t, rows=rows):
            v["x", t] = x1_ref[rows, :]
            v["xnb", t] = _rms(v["x", t], gffn_ref[...]).astype(BF)
        plan.add(("norm", t), "V", 650, (), t_norm)

        for c in range(D_FF // FF_CHUNK):
            cs = slice(c * FF_CHUNK, (c + 1) * FF_CHUNK)

            def t_gu(t=t, c=c, cs=cs):
                v["hg", t, c] = _dot(v["xnb", t], wg_ref[:, cs])
                v["hu", t, c] = _dot(v["xnb", t], wu_ref[:, cs])
            plan.add(("gu", t, c), "M", 512, (("norm", t),), t_gu)

            def t_act(t=t, c=c):
                hg = v["hg", t, c]
                v["act", t, c] = (hg * _sigmoid(hg) * v["hu", t, c]).astype(BF)
            plan.add(("act", t, c), "V", 110, (("gu", t, c),), t_act)

            def t_dn(t=t, c=c, cs=cs):
                prev = v["x", t] if c == 0 else v["x2", t]
                v["x2", t] = prev + _dot(v["act", t, c], wd_ref[cs, :])
            plan.add(("dn", t, c), "M", 300, (("act", t, c),) + ((("dn", t, c - 1),) if c else ()),
                     t_dn)

        last_dn = ("dn", t, D_FF // FF_CHUNK - 1)

        def t_norm2(t=t):
            v["xgb", t] = _rms(v["x2", t], gple_ref[...]).astype(BF)
        plan.add(("norm2", t), "V", 650, (last_dn,), t_norm2)

        def t_ple(t=t, rows=rows):
            v["pe", t] = _dot(p_ref[rows, :].astype(BF), wple_ref[...])
        plan.add(("ple", t), "M", 300, (), t_ple)

        def t_wpg(t=t):
            v["gpre", t] = _dot(v["xgb", t], wpg_ref[...])
        plan.add(("wpg", t), "M", 1024, (("norm2", t),), t_wpg)

        def t_fin(t=t, rows=rows):
            x3 = v["x2", t] + _sigmoid(v["gpre", t]) * v["pe", t]
            y_ref[rows, :] = _rms(x3, gfin_ref[...])
        plan.add(("fin", t), "V", 1000, (("wpg", t), ("ple", t)), t_fin)

    base = pl.program_id(0) * STATE_BATCH
    heads = [slice(h * M_HD, (h + 1) * M_HD) for h in range(M_HEADS)]
    for bb in range(STATE_BATCH):
        def t_cols(bb=bb):
            row = pl.ds(base + bb, 1)
            q, kw = q_ref[row, :], kw_ref[row, :]
            v["cols", bb] = jnp.concatenate([q[:, hs] for hs in heads] + [kw[:, hs] for hs in heads],
                                            axis=0).T
            v["vrow", bb] = v_ref[row, :]
        plan.add(("cols", bb), "V", 40, (), t_cols)

        for h, hs in enumerate(heads):
            def t_pair(bb=bb, h=h, hs=hs):
                cols = v["cols", bb]
                q_col, kw_col = cols[:, h:h + 1], cols[:, M_HEADS + h:M_HEADS + h + 1]
                c_prev = c_ref[bb, h]
                qc_out[pl.ds(base + bb, 1), hs] = jnp.sum(q_col * c_prev, axis=0, keepdims=True)
                c_out[bb, h] = sc_ref[base + bb, h] * c_prev + kw_col * v["vrow", bb][:, hs]
            plan.add(("pair", bb, h), "V", 170, (("cols", bb),), t_pair)

    plan.run()


def _front_kernel(x_ref, lbuf_ref, h0_ref, mbuf_ref, gmix_ref, wbr_ref, wmg_ref, wif_ref, bif_ref,
                  lcw_ref, lcb_ref, wax_ref, ba_ref, bx_ref, lam_ref, mcw_ref, mcb_ref,
                  wq_ref, wk_ref, wv_ref, m0_ref, n0_ref,
                  q_ref, kw_ref, v_ref, sc_ref, hv_ref, hc_ref, n_out, m_out,
                  yl_ref, gm_ref, lbuf_out, h_out, mbuf_out):
    xnb = _rms(x_ref[...], gmix_ref[...]).astype(BF)

    def per_head(cols):
        return jnp.concatenate([jnp.broadcast_to(cols[:, h:h + 1], (cols.shape[0], M_HD))
                                for h in range(M_HEADS)], axis=1)

    def head_sums(a):
        return jnp.concatenate([jnp.sum(a[:, h * M_HD:(h + 1) * M_HD], axis=1, keepdims=True)
                                for h in range(M_HEADS)], axis=1)

    def conv_step(buf_ref, buf_out, x_new, w_ref, b_ref):
        out = b_ref[...] + x_new * w_ref[CONV_W - 1:CONV_W, :]
        for j in range(CONV_W - 1):
            out = out + buf_ref[j] * w_ref[j:j + 1, :]
        for j in range(CONV_W - 2):
            buf_out[j] = buf_ref[j + 1]
        buf_out[CONV_W - 2] = x_new
        return out

    x_l = _dot(xnb, wbr_ref[:, 0:D_MODEL])
    xl_c = conv_step(lbuf_ref, lbuf_out, x_l, lcw_ref, lcb_ref)
    ga, gx = _lru_gates(xl_c.astype(BF), wax_ref, ba_ref[...], bx_ref[...])
    a, u = _lru_coeffs(xl_c, ga, gx, lam_ref[...])
    y_l = a * h0_ref[...] + u
    h_out[...] = y_l
    g_l = _dot(xnb, wmg_ref[:, 0:D_MODEL])
    yl_ref[...] = _sigmoid(g_l) * y_l

    x_m = _dot(xnb, wbr_ref[:, D_MODEL:2 * D_MODEL])
    xm_c = conv_step(mbuf_ref, mbuf_out, x_m, mcw_ref, mcb_ref)
    xcb = (xm_c * _sigmoid(xm_c)).astype(BF)
    xmb = x_m.astype(BF)
    q, k, v = [], [], []
    for h in range(M_HEADS):
        hs = slice(h * M_HD, (h + 1) * M_HD)
        q.append(_dot(xcb[:, hs], wq_ref[h]) * (M_HD ** -0.5))
        k.append(_dot(xcb[:, hs], wk_ref[h]))
        v.append(_dot(xmb[:, hs], wv_ref[h]))
    q, k, v = (jnp.concatenate(a, axis=1) for a in (q, k, v))
    o_m = _dot(xnb, wbr_ref[:, 2 * D_MODEL:3 * D_MODEL])
    g_m = _dot(xnb, wmg_ref[:, D_MODEL:2 * D_MODEL])
    gm_ref[...] = _sigmoid(g_m) * _sigmoid(o_m)

    pre = _dot(xnb, wif_ref[...]) + bif_ref[...]
    ig = pre[:, 0:M_HEADS]
    lf = -_softplus(-pre[:, M_HEADS:N_GATES])
    n_prev = n0_ref[...]
    m_inter = lf + m0_ref[...]
    m_t = jnp.maximum(m_inter, ig)
    wk = jnp.exp(ig - m_t)
    sc = jnp.exp(m_inter - m_t)
    s = head_sums(q * k) * wk
    den = s + sc * head_sums(q * n_prev)
    rden = 1.0 / jnp.maximum(jnp.abs(den), jnp.exp(-m_t))
    wk_d, sc_d = per_head(wk), per_head(sc)
    hv_ref[...] = per_head(s * rden) * v
    hc_ref[...] = per_head(sc * rden)
    q_ref[...] = q
    kw_ref[...] = wk_d * k
    v_ref[...] = v
    sc_ref[...] = sc
    n_out[...] = sc_d * n_prev + wk_d * k
    m_out[...] = m_t


def _back_kernel(x_ref, qc_ref, hv_ref, hc_ref, yl_ref, gm_ref, p_ref, mg_ref, wout_ref, gffn_ref,
                 wg_ref, wu_ref, wd_ref, gple_ref, wpg_ref, wple_ref, gfin_ref, y_ref, mrg_ref):
    for h in range(M_HEADS):
        hs = slice(h * M_HD, (h + 1) * M_HD)
        hh = hv_ref[:, hs] + hc_ref[:, hs] * qc_ref[:, hs]
        hn = hh * lax.rsqrt(jnp.mean(hh * hh, axis=-1, keepdims=True) + EPS) * mg_ref[:, hs]
        mrg_ref[:, hs] = (yl_ref[:, hs] + gm_ref[:, hs] * hn).astype(BF)
    x1 = x_ref[...] + _dot(mrg_ref[...], wout_ref[...])
    y_ref[...] = _ffn_tail(x1, p_ref[...], gffn_ref, wg_ref, wu_ref, wd_ref, gple_ref, wpg_ref,
                           wple_ref, gfin_ref)


def _resident(shape):
    nd = len(shape)
    return pl.BlockSpec(shape, lambda *_: (0,) * nd, pipeline_mode=pl.Buffered(1))


def _params(n_axes):
    return pltpu.CompilerParams(dimension_semantics=("arbitrary",) * n_axes,
                                vmem_limit_bytes=VMEM_LIMIT)


def _interleave_matrix():
    r = np.arange(SEQ_TILE)
    perm = np.zeros((SEQ_TILE, SEQ_TILE), np.float32)
    perm[r, (r % SUBLANES) * SEG + r // SUBLANES] = 1.0
    return perm


def kernel(x_prompt, x_sample, state_lru_conv, state_lru_h, state_mlstm_conv, state_mlstm_C, state_mlstm_n, state_mlstm_m, p_prompt, p_sample, norm_mix_g, w_in, b_gates, lru_conv_w, lru_conv_b, lru_w_a, lru_b_a, lru_w_x, lru_b_x, lru_lambda, mlstm_conv_w, mlstm_conv_b, w_q, w_k, w_v, mlstm_norm_g, w_out, norm_ffn_g, w_ffn_gate, w_ffn_up, w_ffn_down, norm_ple_g, w_ple_gate, w_ple, final_norm_g):
    assert w_in.shape[0] == 1, "single-layer trunk"
    B, T, _ = x_prompt.shape
    S = x_sample.shape[0]
    step = SEQ_TILE * SUB_TILES
    assert T % step == 0 and (B * T) % FFN_TILE == 0 and x_sample.shape[1] == 1
    assert S == STATE_BATCH * ((B * T) // FFN_TILE), "one state batch per ffn grid step"

    sds = jax.ShapeDtypeStruct
    assert w_in.shape[1:] == (D_MODEL, 5 * D_MODEL + N_GATES)
    wt = jnp.swapaxes(w_in[0], 0, 1)
    g0 = 3 * D_MODEL
    src_row = lambda j: pl.multiple_of(
        jnp.where(j < BRANCH_STEPS, j * PREP_COLS, g0 + N_GATES + (j - BRANCH_STEPS) * PREP_COLS),
        SUBLANES)
    slab = lambda rows, index: pl.BlockSpec((pl.Element(rows), pl.Element(D_MODEL)), index)
    cols = lambda index: pl.BlockSpec((D_MODEL, PREP_COLS), index)
    wbr, wmg, wif = pl.pallas_call(
        _split_kernel,
        grid=(5 * D_MODEL // PREP_COLS,),
        in_specs=[slab(PREP_COLS, lambda j: (src_row(j), 0)), slab(GATE_PAD, lambda j: (g0, 0))],
        out_specs=[cols(lambda j: (0, jnp.minimum(j, BRANCH_STEPS - 1))),
                   cols(lambda j: (0, jnp.maximum(j - BRANCH_STEPS, 0))),
                   pl.BlockSpec((D_MODEL, GATE_PAD), lambda j: (0, 0))],
        out_shape=[sds((D_MODEL, 3 * D_MODEL), BF), sds((D_MODEL, 2 * D_MODEL), BF),
                   sds((D_MODEL, GATE_PAD), BF)],
        compiler_params=_params(1),
        name="split_w_in",
    )(wt, wt)
    bif = jnp.pad(b_gates[0], (0, GATE_PAD - N_GATES)).reshape(1, GATE_PAD)
    wax = jnp.concatenate([lru_w_a[0], lru_w_x[0]], axis=2).astype(BF)
    row = lambda a: a.reshape(1, -1)
    gmix, ba, bx, lam = row(norm_mix_g[0]), row(lru_b_a[0]), row(lru_b_x[0]), row(lru_lambda[0])
    lcw, lcb = lru_conv_w[0], row(lru_conv_b[0])
    mcw, mcb = mlstm_conv_w[0], row(mlstm_conv_b[0])
    wq, wk, wv = w_q[0].astype(BF), w_k[0].astype(BF), w_v[0].astype(BF)
    mg = row(mlstm_norm_g[0])
    wout = w_out[0].astype(BF)
    gffn, gple, gfin = row(norm_ffn_g[0]), row(norm_ple_g[0]), row(final_norm_g)
    wg, wu, wd = w_ffn_gate[0].astype(BF), w_ffn_up[0].astype(BF), w_ffn_down[0].astype(BF)
    wpg, wple = w_ple_gate[0].astype(BF), w_ple[0].astype(BF)
    perm_np = _interleave_matrix()
    perm, perm_t = jnp.asarray(perm_np, BF), jnp.asarray(perm_np.T, BF)

    front_w = (gmix, wbr, wmg, wif, bif, lcw, lcb, wax, ba, bx, lam, mcw, mcb, wq, wk, wv)
    mixer_w = (perm, perm_t) + front_w + (mg, wout)
    ffn_w = (gffn, wg, wu, wd, gple, wpg, wple, gfin)

    nt = T // step
    x1, p_lconv, p_h, p_mconv, p_c, p_n, p_m = pl.pallas_call(
        _mixer_kernel,
        grid=(B, nt),
        in_specs=[pl.BlockSpec((1, step, D_MODEL), lambda b, t: (b, t, 0))]
                 + [_resident(w.shape) for w in mixer_w],
        out_specs=[
            pl.BlockSpec((1, step, D_MODEL), lambda b, t: (b, t, 0)),
            pl.BlockSpec((1, CONV_W - 1, D_MODEL), lambda b, t: (b, 0, 0)),
            pl.BlockSpec((1, 1, D_MODEL), lambda b, t: (b, 0, 0)),
            pl.BlockSpec((1, CONV_W - 1, D_MODEL), lambda b, t: (b, 0, 0)),
            pl.BlockSpec((1, M_HEADS, M_HD, M_HD), lambda b, t: (b, 0, 0, 0)),
            pl.BlockSpec((1, M_HEADS, M_HD), lambda b, t: (b, 0, 0)),
            pl.BlockSpec((1, 1, M_HEADS), lambda b, t: (b, 0, 0)),
        ],
        out_shape=[
            sds((B, T, D_MODEL), F32),
            sds((B, CONV_W - 1, D_MODEL), F32),
            sds((B, 1, D_MODEL), F32),
            sds((B, CONV_W - 1, D_MODEL), F32),
            sds((B, M_HEADS, M_HD, M_HD), F32),
            sds((B, M_HEADS, M_HD), F32),
            sds((B, 1, M_HEADS), F32),
        ],
        compiler_params=_params(2),
        name="prompt_mixer",
    )(x_prompt, *mixer_w)

    xs = x_sample.reshape(S, D_MODEL)
    lbuf = jnp.swapaxes(state_lru_conv[0], 0, 1)
    mbuf = jnp.swapaxes(state_mlstm_conv[0], 0, 1)
    tok = sds((S, D_MODEL), F32)
    buf = sds((CONV_W - 1, S, D_MODEL), F32)
    per_head = sds((S, M_HEADS), F32)
    q, kw, v, sc, hv, hc, s_n, s_m, yl, gm, s_lbuf, s_h, s_mbuf = pl.pallas_call(
        _front_kernel,
        out_shape=[tok, tok, tok, per_head, tok, tok, tok, per_head, tok, tok, buf, tok, buf],
        compiler_params=pltpu.CompilerParams(vmem_limit_bytes=VMEM_LIMIT),
        name="sample_front",
    )(xs, lbuf, state_lru_h[0], mbuf, *front_w, state_mlstm_m[0],
      state_mlstm_n[0].reshape(S, D_MODEL))

    n_tok = B * T
    tile = lambda width: pl.BlockSpec((FFN_TILE, width), lambda i: (i, 0))
    c_spec = pl.BlockSpec((STATE_BATCH, M_HEADS, M_HD, M_HD), lambda i: (i, 0, 0, 0))
    y_prompt, qc, s_c = pl.pallas_call(
        _ffn_kernel,
        grid=(n_tok // FFN_TILE,),
        in_specs=[pl.BlockSpec(memory_space=pltpu.SMEM), tile(D_MODEL), tile(P_DIM),
                  _resident(tok.shape), _resident(tok.shape), _resident(tok.shape), c_spec]
                 + [_resident(w.shape) for w in ffn_w],
        out_specs=[tile(D_MODEL), pl.BlockSpec(tok.shape, lambda i: (0, 0)), c_spec],
        out_shape=[sds((n_tok, D_MODEL), F32), tok, sds((S, M_HEADS, M_HD, M_HD), F32)],
        compiler_params=_params(1),
        name="prompt_ffn",
    )(sc, x1.reshape(n_tok, D_MODEL), p_prompt[0].reshape(n_tok, P_DIM), q, kw, v,
      state_mlstm_C[0], *ffn_w)

    y_sample = pl.pallas_call(
        _back_kernel,
        out_shape=tok,
        scratch_shapes=[pltpu.VMEM((S, D_MODEL), BF)],
        compiler_params=pltpu.CompilerParams(vmem_limit_bytes=VMEM_LIMIT),
        name="sample_back",
    )(xs, qc, hv, hc, yl, gm, p_sample[0].reshape(S, P_DIM), mg, wout, *ffn_w)

    lead = lambda a: a[None]
    return (y_prompt.reshape(B, T, D_MODEL), y_sample.reshape(S, 1, D_MODEL),
            lead(p_lconv), p_h.reshape(1, B, D_MODEL), lead(p_mconv), lead(p_c), lead(p_n),
            p_m.reshape(1, B, M_HEADS),
            lead(jnp.swapaxes(s_lbuf, 0, 1)), lead(s_h), lead(jnp.swapaxes(s_mbuf, 0, 1)),
            lead(s_c), s_n.reshape(1, S, M_HEADS, M_HD), lead(s_m))
```

```python
import numpy as np

import jax
import jax.numpy as jnp
from jax import lax
from jax.experimental import pallas as pl
from jax.experimental.pallas import tpu as pltpu

D_MODEL = 1024
M_HEADS = 4
M_HD = D_MODEL // M_HEADS
LRU_BLOCKS = 8
LRU_BLK = D_MODEL // LRU_BLOCKS
LRU_C = 8.0
CONV_W = 4
D_FF = 2816
P_DIM = 256
EPS = 1e-6
M_INIT = -1e30
NEG_LOG2E = -1.4426950408889634
MASKED = -1e30
N_GATES = 2 * M_HEADS
GATE_PAD = 128
SUBLANES = 8
SEQ_TILE = 256
SEG = SEQ_TILE // SUBLANES
SUB_TILES = 2
FFN_TILE = 512
FFN_ROWS = 256
FFN_SUB = FFN_TILE // FFN_ROWS
FF_CHUNK = 256
STATE_BATCH = 4
ANCHORS_PER_SUB = STATE_BATCH * M_HEADS // FFN_SUB
PREP_COLS = 512
BRANCH_STEPS = 3 * D_MODEL // PREP_COLS
VMEM_LIMIT = 56 * 1024 * 1024

BF = jnp.bfloat16
F32 = jnp.float32


def _dot(a, b):
    return jnp.dot(a, b, preferred_element_type=F32)


def _dot_nt(a, b):
    return lax.dot_general(a, b, (((1,), (1,)), ((), ())), preferred_element_type=F32)


def _dot_tn(a, b):
    return lax.dot_general(a, b, (((0,), (0,)), ((), ())), preferred_element_type=F32)


def _sigmoid(x):
    return 1.0 / (1.0 + jnp.exp2(x * NEG_LOG2E))


def _exact_one(x):
    bits = pltpu.bitcast(x, jnp.uint32)
    zero = pltpu.bitcast((bits >> 16) >> 16, F32)
    return 1.0 + zero


def _softplus(x):
    return jnp.maximum(x, 0.0) + jnp.log1p(jnp.exp(-jnp.abs(x)))


def _rms(x, g):
    return x * lax.rsqrt(jnp.mean(x * x, axis=-1, keepdims=True) + EPS) * g


def _group(x, i):
    return x[i * SUBLANES:(i + 1) * SUBLANES, :]


def _lru_coeffs(xc, ga, gx, lam):
    r = _sigmoid(ga)
    ig = _sigmoid(gx)
    log_a = -LRU_C * r * _softplus(-lam)
    a = jnp.exp(log_a)
    om = 1.0 - a * a
    root = jnp.where(om > 0.0, om * lax.rsqrt(om), 0.0)
    return a, root * ig * xc


def _lru_gates(xcb, wax_ref, ba, bx):
    ga, gx = [], []
    for n in range(LRU_BLOCKS):
        g = _dot(xcb[:, n * LRU_BLK:(n + 1) * LRU_BLK], wax_ref[n])
        ga.append(g[:, :LRU_BLK])
        gx.append(g[:, LRU_BLK:])
    return jnp.concatenate(ga, axis=1) + ba, jnp.concatenate(gx, axis=1) + bx


def _scan_interleaved(a, u, h0):
    prods, sums = [], []
    p = s = None
    for i in range(SEG):
        ai, ui = _group(a, i), _group(u, i)
        p, s = (ai, ui) if i == 0 else (ai * p, ai * s + ui)
        prods.append(p)
        sums.append(s)
    c = h0
    starts = [c]
    for j in range(SUBLANES - 1):
        c = p[j:j + 1, :] * c + s[j:j + 1, :]
        starts.append(c)
    start = jnp.concatenate(starts, axis=0)
    return jnp.concatenate([sums[i] + prods[i] * start for i in range(SEG)], axis=0)


def _cumsum_interleaved(x):
    acc = []
    run = None
    for i in range(SEG):
        run = _group(x, i) if i == 0 else run + _group(x, i)
        acc.append(run)
    sub = lax.broadcasted_iota(jnp.int32, run.shape, 0)
    inc = run
    s = 1
    while s < SUBLANES:
        inc = inc + jnp.where(sub >= s, pltpu.roll(inc, s, 0), 0.0)
        s *= 2
    before = inc - run
    return jnp.concatenate([r + before for r in acc], axis=0)


def _conv_interleaved(carry, x_new, w, b):
    sub = lax.broadcasted_iota(jnp.int32, (SUBLANES, x_new.shape[1]), 0)
    head, new = [], []
    for r in range(CONV_W - 1):
        cur = _group(x_new, SEG - (CONV_W - 1) + r)
        head.append(jnp.where(sub == 0, carry[r:r + 1, :], pltpu.roll(cur, 1, 0)))
        new.append(cur[SUBLANES - 1:SUBLANES, :])
    ext = jnp.concatenate(head + [x_new], axis=0)
    out = b
    for j in range(CONV_W):
        out = out + ext[j * SUBLANES:j * SUBLANES + SEQ_TILE, :] * w[j:j + 1, :]
    return out, jnp.concatenate(new, axis=0)


def _time_of_row(r):
    return (r & (SUBLANES - 1)) * SEG + (r >> (SUBLANES.bit_length() - 1))


RESULT_DELAY = {"M": 250, "V": 40}


class _Plan:
    def __init__(self):
        self.tasks = {}

    def add(self, name, unit, cost, deps, fn):
        self.tasks[name] = (unit, cost, tuple(deps), fn)

    def order(self):
        succ = {n: [] for n in self.tasks}
        for n, (_, _, deps, _) in self.tasks.items():
            for p in deps:
                succ[p].append(n)
        tail = {}

        def path(n):
            if n not in tail:
                tail[n] = self.tasks[n][1] + max([path(s) for s in succ[n]], default=0)
            return tail[n]

        free = {"M": 0, "V": 0}
        done, order, left = {}, [], list(self.tasks)
        while left:
            ready = [n for n in left if all(p in done for p in self.tasks[n][2])]

            def start(n):
                unit, _, deps, _ = self.tasks[n]
                return max([free[unit]] + [done[p] for p in deps])

            n = min(ready, key=lambda n: (start(n), -path(n)))
            unit, cost, _, _ = self.tasks[n]
            st = start(n)
            free[unit] = st + cost
            done[n] = st + cost + RESULT_DELAY[unit]
            order.append((st, len(order), n))
            left.remove(n)
        return [n for _, _, n in sorted(order)]

    def run(self):
        for n in self.order():
            self.tasks[n][3]()


def _mixer_kernel(x_ref, perm_ref, permt_ref, gmix_ref, wbr_ref, wmg_ref, wif_ref, bif_ref, lcw_ref,
                  lcb_ref, wax_ref, ba_ref, bx_ref, lam_ref, mcw_ref, mcb_ref, wq_ref, wk_ref,
                  wv_ref, mg_ref, wout_ref,
                  x1_ref, lconv_ref, h_ref, mconv_ref, c_ref, n_ref, m_ref):
    tt = SEQ_TILE
    last = tt - 1

    @pl.when(pl.program_id(1) == 0)
    def _():
        lconv_ref[...] = jnp.zeros(lconv_ref.shape, F32)
        mconv_ref[...] = jnp.zeros(mconv_ref.shape, F32)
        h_ref[...] = jnp.zeros(h_ref.shape, F32)
        c_ref[...] = jnp.zeros(c_ref.shape, F32)
        n_ref[...] = jnp.zeros(n_ref.shape, F32)
        m_ref[...] = jnp.full(m_ref.shape, M_INIT, F32)

    lcw, lcb, mcw, mcb = lcw_ref[...], lcb_ref[...], mcw_ref[...], mcb_ref[...]
    ba, bx, lam, mg = ba_ref[...], bx_ref[...], lam_ref[...], mg_ref[...]

    v = {}
    for g in range(M_HEADS):
        gs = slice(g * M_HD, (g + 1) * M_HD)
        v["lconv", -1, g], v["mconv", -1, g] = lconv_ref[0, :, gs], mconv_ref[0, :, gs]
        v["h", -1, g] = h_ref[0, :, gs]
        v["c", -1, g], v["n", -1, g] = c_ref[0, g], n_ref[0, g:g + 1, :]
        v["m", -1, g] = m_ref[0, :, g:g + 1]
    plan = _Plan()

    for t in range(SUB_TILES):
        dep_prev = (lambda name, g, t=t: ((name, t - 1, g),)) if t else (lambda name, g: ())

        def t_norm(t=t):
            v["x", t] = x_ref[0, t * tt:(t + 1) * tt, :]
            v["xn_t", t] = _rms(v["x", t], gmix_ref[...]).astype(BF)
        plan.add(("norm", t), "V", 650, (), t_norm)

        def t_perm(t=t):
            v["xnb", t] = _dot(perm_ref[...], v["xn_t", t]).astype(BF)
        plan.add(("perm", t), "M", 260, (("norm", t),), t_perm)

        def proj(name, w_ref, col0, g, t=t):
            def run():
                v[name, t, g] = _dot(v["xnb", t], w_ref[:, col0 + g * M_HD:col0 + (g + 1) * M_HD])
            plan.add((name, t, g), "M", 260, (("perm", t),), run)

        def t_wif(t=t):
            v["pre", t] = _dot(v["xnb", t], wif_ref[...]) + bif_ref[...]
        plan.add(("wif", t), "M", 260, (("perm", t),), t_wif)

        def t_gcum(t=t):
            pre = v["pre", t]
            lane = lax.broadcasted_iota(jnp.int32, (tt, GATE_PAD), 1)
            v["gcol", t] = jnp.where(lane < M_HEADS, pre, _cumsum_interleaved(-_softplus(-pre)))
            v["grow", t] = v["gcol", t].T
            v["tri", t] = (_time_of_row(lax.broadcasted_iota(jnp.int32, (tt, 1), 0))
                           >= _time_of_row(lax.broadcasted_iota(jnp.int32, (1, tt), 1)))
        plan.add(("gcum", t), "V", 150, (("wif", t),), t_gcum)

        for g in range(M_HEADS):
            gs = slice(g * M_HD, (g + 1) * M_HD)
            for name, w_ref, col0 in (("xl", wbr_ref, 0), ("xm", wbr_ref, D_MODEL),
                                      ("om", wbr_ref, 2 * D_MODEL), ("gl", wmg_ref, 0),
                                      ("gm", wmg_ref, D_MODEL)):
                proj(name, w_ref, col0, g)

            def t_convl(t=t, g=g, gs=gs):
                v["xlc", t, g], v["lconv", t, g] = _conv_interleaved(
                    v["lconv", t - 1, g], v["xl", t, g], lcw[:, gs], lcb[:, gs])
            plan.add(("convl", t, g), "V", 135, (("xl", t, g),) + dep_prev("convl", g), t_convl)

            def t_gates(t=t, g=g, gs=gs):
                xlb = v["xlc", t, g].astype(BF)
                ga, gx = [], []
                for n in range(2):
                    gg = _dot(xlb[:, n * LRU_BLK:(n + 1) * LRU_BLK], wax_ref[2 * g + n])
                    ga.append(gg[:, :LRU_BLK])
                    gx.append(gg[:, LRU_BLK:])
                v["ga", t, g] = jnp.concatenate(ga, axis=1) + ba[:, gs]
                v["gx", t, g] = jnp.concatenate(gx, axis=1) + bx[:, gs]
            plan.add(("gates", t, g), "M", 130, (("convl", t, g),), t_gates)

            def t_coef(t=t, g=g, gs=gs):
                v["a", t, g], v["u", t, g] = _lru_coeffs(v["xlc", t, g], v["ga", t, g], v["gx", t, g],
                                                         lam[:, gs])
            plan.add(("coef", t, g), "V", 280, (("gates", t, g),), t_coef)

            def t_scan(t=t, g=g):
                v["yl", t, g] = _scan_interleaved(v["a", t, g], v["u", t, g], v["h", t - 1, g])
                v["h", t, g] = v["yl", t, g][last:last + 1, :]
            plan.add(("scan", t, g), "V", 120, (("coef", t, g),) + dep_prev("scan", g), t_scan)

            h = g

            def t_convm(t=t, h=h, gs=gs):
                xm_c, v["mconv", t, h] = _conv_interleaved(v["mconv", t - 1, h], v["xm", t, h],
                                                           mcw[:, gs], mcb[:, gs])
                v["xcb", t, h] = (xm_c * _sigmoid(xm_c)).astype(BF)
            plan.add(("convm", t, h), "V", 210, (("xm", t, h),) + dep_prev("convm", h), t_convm)

            def t_qkv(t=t, h=h):
                v["q", t, h] = _dot(v["xcb", t, h], wq_ref[h]) * (M_HD ** -0.5)
                v["k", t, h] = _dot(v["xcb", t, h], wk_ref[h])
                vv = _dot(v["xm", t, h].astype(BF), wv_ref[h])
                v["qb", t, h], v["kb", t, h], v["vb", t, h] = (
                    v["q", t, h].astype(BF), v["k", t, h].astype(BF), vv.astype(BF))
            plan.add(("qkv", t, h), "M", 200, (("convm", t, h),), t_qkv)

            def t_qk(t=t, h=h):
                v["qk", t, h] = _dot_nt(v["qb", t, h], v["kb", t, h])
            plan.add(("qk", t, h), "M", 64, (("qkv", t, h),), t_qk)

            def t_sp(t=t, h=h):
                gcol, grow = v["gcol", t], v["grow", t]
                b_col = gcol[:, M_HEADS + h:M_HEADS + h + 1]
                ig_row, b_row = grow[h:h + 1, :], grow[M_HEADS + h:M_HEADS + h + 1, :]
                dlog = jnp.where(v["tri", t], b_col - b_row + ig_row, MASKED)
                m_inter = b_col + v["m", t - 1, h]
                m_t = jnp.maximum(m_inter, jnp.max(dlog, axis=1, keepdims=True))
                s = v["qk", t, h] * jnp.exp(dlog - m_t)
                v["m_t", t, h], v["sc", t, h] = m_t, jnp.exp(m_inter - m_t)
                v["ssum", t, h] = jnp.sum(s, axis=1, keepdims=True)
                v["sb", t, h] = s.astype(BF)
                v["m", t, h] = m_t[last:last + 1, :]
            plan.add(("sp", t, h), "V", 200, (("qk", t, h), ("gcum", t)) + dep_prev("sp", h), t_sp)

            def t_sv(t=t, h=h):
                v["sv", t, h] = _dot(v["sb", t, h], v["vb", t, h])
                v["qc", t, h] = _dot(v["qb", t, h], v["c", t - 1, h].astype(BF))
            plan.add(("sv", t, h), "M", 130, (("sp", t, h),) + dep_prev("cnew", h), t_sv)

            def t_hn(t=t, h=h, gs=gs):
                sc, m_t = v["sc", t, h], v["m_t", t, h]
                num = v["sv", t, h] + sc * v["qc", t, h]
                den = v["ssum", t, h] + sc * jnp.sum(v["q", t, h] * v["n", t - 1, h], axis=1,
                                                     keepdims=True)
                hh = num / jnp.maximum(jnp.abs(den), jnp.exp(-m_t))
                v["hn", t, h] = (hh * lax.rsqrt(jnp.mean(hh * hh, axis=-1, keepdims=True) + EPS)
                                 * mg[:, gs])
            plan.add(("hn", t, h), "V", 170, (("sv", t, h),) + dep_prev("kw", h), t_hn)

            def t_kw(t=t, h=h):
                gcol = v["gcol", t]
                ig_col, b_col = gcol[:, h:h + 1], gcol[:, M_HEADS + h:M_HEADS + h + 1]
                m_last = v["m", t, h]
                b_last = b_col[last:last + 1, :]
                kw = v["k", t, h] * jnp.exp(b_last - b_col + ig_col - m_last)
                v["dec", t, h] = jnp.exp(b_last + v["m", t - 1, h] - m_last)
                v["n", t, h] = v["dec", t, h] * v["n", t - 1, h] + jnp.sum(kw, axis=0, keepdims=True)
                v["kwb", t, h] = kw.astype(BF)
            plan.add(("kw", t, h), "V", 80, (("sp", t, h),) + dep_prev("kw", h), t_kw)

            def t_ckv(t=t, h=h):
                v["ckv", t, h] = _dot_tn(v["kwb", t, h], v["vb", t, h])
            plan.add(("ckv", t, h), "M", 64, (("kw", t, h),), t_ckv)

            def t_cnew(t=t, h=h):
                v["c", t, h] = v["dec", t, h] * v["c", t - 1, h] + v["ckv", t, h]
            plan.add(("cnew", t, h), "V", 40, (("ckv", t, h),) + dep_prev("cnew", h), t_cnew)

            def t_sig(t=t, h=h):
                v["gate_l", t, h] = _sigmoid(v["gl", t, h]) * v["yl", t, h]
                v["gate_m", t, h] = _sigmoid(v["gm", t, h]) * _sigmoid(v["om", t, h])
            plan.add(("sig", t, h), "V", 180,
                     (("gl", t, h), ("gm", t, h), ("om", t, h), ("scan", t, h)), t_sig)

            def t_mrg(t=t, h=h):
                v["mrg", t, h] = (v["gate_l", t, h] + v["gate_m", t, h] * v["hn", t, h]).astype(BF)
            plan.add(("mrg", t, h), "V", 40, (("sig", t, h), ("hn", t, h)), t_mrg)

            def t_pt(t=t, h=h):
                v["mrg_t", t, h] = _dot(permt_ref[...], v["mrg", t, h]).astype(BF)
            plan.add(("pt", t, h), "M", 64, (("mrg", t, h),), t_pt)

            def t_wo(t=t, h=h, gs=gs):
                part = _dot(v["mrg_t", t, h], wout_ref[gs, :])
                v["out", t] = part if h == 0 else v["out", t] + part
            plan.add(("wo", t, h), "M", 260, (("pt", t, h),) + ((("wo", t, h - 1),) if h else ()),
                     t_wo)

        def t_fin(t=t):
            x1_ref[0, t * tt:(t + 1) * tt, :] = v["x", t] + v["out", t]
        plan.add(("fin", t), "V", 70, (("wo", t, M_HEADS - 1),), t_fin)

    def t_state():
        e = SUB_TILES - 1
        heads = range(M_HEADS)
        lconv_ref[0] = jnp.concatenate([v["lconv", e, g] for g in heads], axis=1)
        mconv_ref[0] = jnp.concatenate([v["mconv", e, g] for g in heads], axis=1)
        h_ref[0] = jnp.concatenate([v["h", e, g] for g in heads], axis=1)
        for h in heads:
            c_ref[0, h] = v["c", e, h]
        n_ref[0] = jnp.concatenate([v["n", e, h] for h in heads], axis=0)
        m_ref[0] = jnp.concatenate([v["m", e, h] for h in heads], axis=1)
    plan.add("state", "V", 60, tuple(("cnew", SUB_TILES - 1, h) for h in range(M_HEADS))
             + tuple(("scan", SUB_TILES - 1, h) for h in range(M_HEADS))
             + tuple(("fin", t) for t in range(SUB_TILES)), t_state)

    plan.run()


def _split_kernel(wt_ref, gt_ref, wbr_ref, wmg_ref, wif_ref):
    j = pl.program_id(0)
    piece = wt_ref[...].T.astype(BF)

    @pl.when(j < BRANCH_STEPS)
    def _():
        wbr_ref[...] = piece

    @pl.when(j >= BRANCH_STEPS)
    def _():
        wmg_ref[...] = piece

    @pl.when(j == 0)
    def _():
        gates = gt_ref[...].T
        lane = lax.broadcasted_iota(jnp.int32, gates.shape, 1)
        wif_ref[...] = jnp.where(lane < N_GATES, gates, 0.0).astype(BF)


def _ffn_tail(x1, p, gffn_ref, wg_ref, wu_ref, wd_ref, gple_ref, wpg_ref, wple_ref, gfin_ref):
    xnb = _rms(x1, gffn_ref[...]).astype(BF)
    x2 = x1
    for c in range(D_FF // FF_CHUNK):
        cs = slice(c * FF_CHUNK, (c + 1) * FF_CHUNK)
        hg = _dot(xnb, wg_ref[:, cs])
        hu = _dot(xnb, wu_ref[:, cs])
        act = (hg * _sigmoid(hg) * hu).astype(BF)
        x2 = x2 + _dot(act, wd_ref[cs, :])
    gate = _sigmoid(_dot(_rms(x2, gple_ref[...]).astype(BF), wpg_ref[...]))
    x3 = x2 + gate * _dot(p.astype(BF), wple_ref[...])
    return _rms(x3, gfin_ref[...])


def _ffn_kernel(sc_ref, x1_ref, p_ref, q_ref, kw_ref, v_ref, c_ref, gffn_ref, wg_ref, wu_ref,
                wd_ref, gple_ref, wpg_ref, wple_ref, gfin_ref, y_ref, qc_out, c_out):
    tt = FFN_ROWS
    v = {}
    plan = _Plan()

    for t in range(FFN_SUB):
        rows = slice(t * tt, (t + 1) * tt)

        def t_norm(t=t, rows=rows):
            v["x", t] = x1_ref[rows, :]
            v["xnb", t] = _rms(v["x", t], gffn_ref[...]).astype(BF)
        plan.add(("norm", t), "V", 650, (), t_norm)

        for c in range(D_FF // FF_CHUNK):
            cs = slice(c * FF_CHUNK, (c + 1) * FF_CHUNK)

            def t_gu(t=t, c=c, cs=cs):
                v["hg", t, c] = _dot(v["xnb", t], wg_ref[:, cs])
                v["hu", t, c] = _dot(v["xnb", t], wu_ref[:, cs])
            plan.add(("gu", t, c), "M", 512, (("norm", t),), t_gu)

            anchor = t * ANCHORS_PER_SUB + c if c < ANCHORS_PER_SUB else None

            def t_act(t=t, c=c, anchor=anchor):
                hg = v["hg", t, c]
                scale = NEG_LOG2E if anchor is None else NEG_LOG2E * v["one", anchor]
                v["act", t, c] = (hg / (1.0 + jnp.exp2(hg * scale)) * v["hu", t, c]).astype(BF)
            pair_dep = () if anchor is None else (("pair", anchor // M_HEADS, anchor % M_HEADS),)
            plan.add(("act", t, c), "V", 110, (("gu", t, c),) + pair_dep, t_act)

            def t_dn(t=t, c=c, cs=cs):
                prev = v["x", t] if c == 0 else v["x2", t]
                v["x2", t] = prev + _dot(v["act", t, c], wd_ref[cs, :])
            plan.add(("dn", t, c), "M", 300, (("act", t, c),) + ((("dn", t, c - 1),) if c else ()),
                     t_dn)

        last_dn = ("dn", t, D_FF // FF_CHUNK - 1)

        def t_norm2(t=t):
            v["xgb", t] = _rms(v["x2", t], gple_ref[...]).astype(BF)
        plan.add(("norm2", t), "V", 650, (last_dn,), t_norm2)

        def t_ple(t=t, rows=rows):
            v["pe", t] = _dot(p_ref[rows, :].astype(BF), wple_ref[...])
        plan.add(("ple", t), "M", 300, (), t_ple)

        def t_wpg(t=t):
            v["gpre", t] = _dot(v["xgb", t], wpg_ref[...])
        plan.add(("wpg", t), "M", 1024, (("norm2", t),), t_wpg)

        def t_fin(t=t, rows=rows):
            x3 = v["x2", t] + _sigmoid(v["gpre", t]) * v["pe", t]
            y_ref[rows, :] = _rms(x3, gfin_ref[...])
        plan.add(("fin", t), "V", 1000, (("wpg", t), ("ple", t)), t_fin)

    base = pl.program_id(0) * STATE_BATCH
    heads = [slice(h * M_HD, (h + 1) * M_HD) for h in range(M_HEADS)]
    for bb in range(STATE_BATCH):
        def t_cols(bb=bb):
            row = pl.ds(base + bb, 1)
            q, kw = q_ref[row, :], kw_ref[row, :]
            v["cols", bb] = jnp.concatenate([q[:, hs] for hs in heads] + [kw[:, hs] for hs in heads],
                                            axis=0).T
            v["vrow", bb] = v_ref[row, :]
        plan.add(("cols", bb), "V", 40, (), t_cols)

        for h, hs in enumerate(heads):
            def t_pair(bb=bb, h=h, hs=hs):
                cols = v["cols", bb]
                q_col, kw_col = cols[:, h:h + 1], cols[:, M_HEADS + h:M_HEADS + h + 1]
                c_prev = c_ref[bb, h]
                qc_out[pl.ds(base + bb, 1), hs] = jnp.sum(q_col * c_prev, axis=0, keepdims=True)
                c_out[bb, h] = sc_ref[base + bb, h] * c_prev + kw_col * v["vrow", bb][:, hs]
                stored = jnp.sum(c_out[bb, h], axis=0, keepdims=True)[:, 0:1]
                v["one", bb * M_HEADS + h] = _exact_one(stored)
            plan.add(("pair", bb, h), "V", 200, (("cols", bb),), t_pair)

    plan.run()


def _front_kernel(x_ref, lbuf_ref, h0_ref, mbuf_ref, gmix_ref, wbr_ref, wmg_ref, wif_ref, bif_ref,
                  lcw_ref, lcb_ref, wax_ref, ba_ref, bx_ref, lam_ref, mcw_ref, mcb_ref,
                  wq_ref, wk_ref, wv_ref, m0_ref, n0_ref,
                  q_ref, kw_ref, v_ref, sc_ref, hv_ref, hc_ref, n_out, m_out,
                  yl_ref, gm_ref, lbuf_out, h_out, mbuf_out):
    xnb = _rms(x_ref[...], gmix_ref[...]).astype(BF)

    def per_head(cols):
        return jnp.concatenate([jnp.broadcast_to(cols[:, h:h + 1], (cols.shape[0], M_HD))
                                for h in range(M_HEADS)], axis=1)

    def head_sums(a):
        return jnp.concatenate([jnp.sum(a[:, h * M_HD:(h + 1) * M_HD], axis=1, keepdims=True)
                                for h in range(M_HEADS)], axis=1)

    def conv_step(buf_ref, buf_out, x_new, w_ref, b_ref):
        out = b_ref[...] + x_new * w_ref[CONV_W - 1:CONV_W, :]
        for j in range(CONV_W - 1):
            out = out + buf_ref[j] * w_ref[j:j + 1, :]
        for j in range(CONV_W - 2):
            buf_out[j] = buf_ref[j + 1]
        buf_out[CONV_W - 2] = x_new
        return out

    x_l = _dot(xnb, wbr_ref[:, 0:D_MODEL])
    xl_c = conv_step(lbuf_ref, lbuf_out, x_l, lcw_ref, lcb_ref)
    ga, gx = _lru_gates(xl_c.astype(BF), wax_ref, ba_ref[...], bx_ref[...])
    a, u = _lru_coeffs(xl_c, ga, gx, lam_ref[...])
    y_l = a * h0_ref[...] + u
    h_out[...] = y_l
    g_l = _dot(xnb, wmg_ref[:, 0:D_MODEL])
    yl_ref[...] = _sigmoid(g_l) * y_l

    x_m = _dot(xnb, wbr_ref[:, D_MODEL:2 * D_MODEL])
    xm_c = conv_step(mbuf_ref, mbuf_out, x_m, mcw_ref, mcb_ref)
    xcb = (xm_c * _sigmoid(xm_c)).astype(BF)
    xmb = x_m.astype(BF)
    q, k, v = [], [], []
    for h in range(M_HEADS):
        hs = slice(h * M_HD, (h + 1) * M_HD)
        q.append(_dot(xcb[:, hs], wq_ref[h]) * (M_HD ** -0.5))
        k.append(_dot(xcb[:, hs], wk_ref[h]))
        v.append(_dot(xmb[:, hs], wv_ref[h]))
    q, k, v = (jnp.concatenate(a, axis=1) for a in (q, k, v))
    o_m = _dot(xnb, wbr_ref[:, 2 * D_MODEL:3 * D_MODEL])
    g_m = _dot(xnb, wmg_ref[:, D_MODEL:2 * D_MODEL])
    gm_ref[...] = _sigmoid(g_m) * _sigmoid(o_m)

    pre = _dot(xnb, wif_ref[...]) + bif_ref[...]
    ig = pre[:, 0:M_HEADS]
    lf = -_softplus(-pre[:, M_HEADS:N_GATES])
    n_prev = n0_ref[...]
    m_inter = lf + m0_ref[...]
    m_t = jnp.maximum(m_inter, ig)
    wk = jnp.exp(ig - m_t)
    sc = jnp.exp(m_inter - m_t)
    s = head_sums(q * k) * wk
    den = s + sc * head_sums(q * n_prev)
    rden = 1.0 / jnp.maximum(jnp.abs(den), jnp.exp(-m_t))
    wk_d, sc_d = per_head(wk), per_head(sc)
    hv_ref[...] = per_head(s * rden) * v
    hc_ref[...] = per_head(sc * rden)
    q_ref[...] = q
    kw_ref[...] = wk_d * k
    v_ref[...] = v
    sc_ref[...] = sc
    n_out[...] = sc_d * n_prev + wk_d * k
    m_out[...] = m_t


def _back_kernel(x_ref, qc_ref, hv_ref, hc_ref, yl_ref, gm_ref, p_ref, mg_ref, wout_ref, gffn_ref,
                 wg_ref, wu_ref, wd_ref, gple_ref, wpg_ref, wple_ref, gfin_ref, y_ref, mrg_ref):
    for h in range(M_HEADS):
        hs = slice(h * M_HD, (h + 1) * M_HD)
        hh = hv_ref[:, hs] + hc_ref[:, hs] * qc_ref[:, hs]
        hn = hh * lax.rsqrt(jnp.mean(hh * hh, axis=-1, keepdims=True) + EPS) * mg_ref[:, hs]
        mrg_ref[:, hs] = (yl_ref[:, hs] + gm_ref[:, hs] * hn).astype(BF)
    x1 = x_ref[...] + _dot(mrg_ref[...], wout_ref[...])
    y_ref[...] = _ffn_tail(x1, p_ref[...], gffn_ref, wg_ref, wu_ref, wd_ref, gple_ref, wpg_ref,
                           wple_ref, gfin_ref)


def _resident(shape):
    nd = len(shape)
    return pl.BlockSpec(shape, lambda *_: (0,) * nd, pipeline_mode=pl.Buffered(1))


def _params(n_axes):
    return pltpu.CompilerParams(dimension_semantics=("arbitrary",) * n_axes,
                                vmem_limit_bytes=VMEM_LIMIT)


def _interleave_matrix():
    r = np.arange(SEQ_TILE)
    perm = np.zeros((SEQ_TILE, SEQ_TILE), np.float32)
    perm[r, (r % SUBLANES) * SEG + r // SUBLANES] = 1.0
    return perm


def kernel(x_prompt, x_sample, state_lru_conv, state_lru_h, state_mlstm_conv, state_mlstm_C, state_mlstm_n, state_mlstm_m, p_prompt, p_sample, norm_mix_g, w_in, b_gates, lru_conv_w, lru_conv_b, lru_w_a, lru_b_a, lru_w_x, lru_b_x, lru_lambda, mlstm_conv_w, mlstm_conv_b, w_q, w_k, w_v, mlstm_norm_g, w_out, norm_ffn_g, w_ffn_gate, w_ffn_up, w_ffn_down, norm_ple_g, w_ple_gate, w_ple, final_norm_g):
    assert w_in.shape[0] == 1, "single-layer trunk"
    B, T, _ = x_prompt.shape
    S = x_sample.shape[0]
    step = SEQ_TILE * SUB_TILES
    assert T % step == 0 and (B * T) % FFN_TILE == 0 and x_sample.shape[1] == 1
    assert S == STATE_BATCH * ((B * T) // FFN_TILE), "one state batch per ffn grid step"

    sds = jax.ShapeDtypeStruct
    assert w_in.shape[1:] == (D_MODEL, 5 * D_MODEL + N_GATES)
    wt = jnp.swapaxes(w_in[0], 0, 1)
    g0 = 3 * D_MODEL
    src_row = lambda j: pl.multiple_of(
        jnp.where(j < BRANCH_STEPS, j * PREP_COLS, g0 + N_GATES + (j - BRANCH_STEPS) * PREP_COLS),
        SUBLANES)
    slab = lambda rows, index: pl.BlockSpec((pl.Element(rows), pl.Element(D_MODEL)), index)
    cols = lambda index: pl.BlockSpec((D_MODEL, PREP_COLS), index)
    wbr, wmg, wif = pl.pallas_call(
        _split_kernel,
        grid=(5 * D_MODEL // PREP_COLS,),
        in_specs=[slab(PREP_COLS, lambda j: (src_row(j), 0)), slab(GATE_PAD, lambda j: (g0, 0))],
        out_specs=[cols(lambda j: (0, jnp.minimum(j, BRANCH_STEPS - 1))),
                   cols(lambda j: (0, jnp.maximum(j - BRANCH_STEPS, 0))),
                   pl.BlockSpec((D_MODEL, GATE_PAD), lambda j: (0, 0))],
        out_shape=[sds((D_MODEL, 3 * D_MODEL), BF), sds((D_MODEL, 2 * D_MODEL), BF),
                   sds((D_MODEL, GATE_PAD), BF)],
        compiler_params=_params(1),
        name="split_w_in",
    )(wt, wt)
    bif = jnp.pad(b_gates[0], (0, GATE_PAD - N_GATES)).reshape(1, GATE_PAD)
    wax = jnp.concatenate([lru_w_a[0], lru_w_x[0]], axis=2).astype(BF)
    row = lambda a: a.reshape(1, -1)
    gmix, ba, bx, lam = row(norm_mix_g[0]), row(lru_b_a[0]), row(lru_b_x[0]), row(lru_lambda[0])
    lcw, lcb = lru_conv_w[0], row(lru_conv_b[0])
    mcw, mcb = mlstm_conv_w[0], row(mlstm_conv_b[0])
    wq, wk, wv = w_q[0].astype(BF), w_k[0].astype(BF), w_v[0].astype(BF)
    mg = row(mlstm_norm_g[0])
    wout = w_out[0].astype(BF)
    gffn, gple, gfin = row(norm_ffn_g[0]), row(norm_ple_g[0]), row(final_norm_g)
    wg, wu, wd = w_ffn_gate[0].astype(BF), w_ffn_up[0].astype(BF), w_ffn_down[0].astype(BF)
    wpg, wple = w_ple_gate[0].astype(BF), w_ple[0].astype(BF)
    perm_np = _interleave_matrix()
    perm, perm_t = jnp.asarray(perm_np, BF), jnp.asarray(perm_np.T, BF)

    front_w = (gmix, wbr, wmg, wif, bif, lcw, lcb, wax, ba, bx, lam, mcw, mcb, wq, wk, wv)
    mixer_w = (perm, perm_t) + front_w + (mg, wout)
    ffn_w = (gffn, wg, wu, wd, gple, wpg, wple, gfin)

    nt = T // step
    x1, p_lconv, p_h, p_mconv, p_c, p_n, p_m = pl.pallas_call(
        _mixer_kernel,
        grid=(B, nt),
        in_specs=[pl.BlockSpec((1, step, D_MODEL), lambda b, t: (b, t, 0))]
                 + [_resident(w.shape) for w in mixer_w],
        out_specs=[
            pl.BlockSpec((1, step, D_MODEL), lambda b, t: (b, t, 0)),
            pl.BlockSpec((1, CONV_W - 1, D_MODEL), lambda b, t: (b, 0, 0)),
            pl.BlockSpec((1, 1, D_MODEL), lambda b, t: (b, 0, 0)),
            pl.BlockSpec((1, CONV_W - 1, D_MODEL), lambda b, t: (b, 0, 0)),
            pl.BlockSpec((1, M_HEADS, M_HD, M_HD), lambda b, t: (b, 0, 0, 0)),
            pl.BlockSpec((1, M_HEADS, M_HD), lambda b, t: (b, 0, 0)),
            pl.BlockSpec((1, 1, M_HEADS), lambda b, t: (b, 0, 0)),
        ],
        out_shape=[
            sds((B, T, D_MODEL), F32),
            sds((B, CONV_W - 1, D_MODEL), F32),
            sds((B, 1, D_MODEL), F32),
            sds((B, CONV_W - 1, D_MODEL), F32),
            sds((B, M_HEADS, M_HD, M_HD), F32),
            sds((B, M_HEADS, M_HD), F32),
            sds((B, 1, M_HEADS), F32),
        ],
        compiler_params=_params(2),
        name="prompt_mixer",
    )(x_prompt, *mixer_w)

    xs = x_sample.reshape(S, D_MODEL)
    lbuf = jnp.swapaxes(state_lru_conv[0], 0, 1)
    mbuf = jnp.swapaxes(state_mlstm_conv[0], 0, 1)
    tok = sds((S, D_MODEL), F32)
    buf = sds((CONV_W - 1, S, D_MODEL), F32)
    per_head = sds((S, M_HEADS), F32)
    q, kw, v, sc, hv, hc, s_n, s_m, yl, gm, s_lbuf, s_h, s_mbuf = pl.pallas_call(
        _front_kernel,
        out_shape=[tok, tok, tok, per_head, tok, tok, tok, per_head, tok, tok, buf, tok, buf],
        compiler_params=pltpu.CompilerParams(vmem_limit_bytes=VMEM_LIMIT),
        name="sample_front",
    )(xs, lbuf, state_lru_h[0], mbuf, *front_w, state_mlstm_m[0],
      state_mlstm_n[0].reshape(S, D_MODEL))

    n_tok = B * T
    tile = lambda width: pl.BlockSpec((FFN_TILE, width), lambda i: (i, 0))
    c_spec = pl.BlockSpec((STATE_BATCH, M_HEADS, M_HD, M_HD), lambda i: (i, 0, 0, 0))
    y_prompt, qc, s_c = pl.pallas_call(
        _ffn_kernel,
        grid=(n_tok // FFN_TILE,),
        in_specs=[pl.BlockSpec(memory_space=pltpu.SMEM), tile(D_MODEL), tile(P_DIM),
                  _resident(tok.shape), _resident(tok.shape), _resident(tok.shape), c_spec]
                 + [_resident(w.shape) for w in ffn_w],
        out_specs=[tile(D_MODEL), pl.BlockSpec(tok.shape, lambda i: (0, 0)), c_spec],
        out_shape=[sds((n_tok, D_MODEL), F32), tok, sds((S, M_HEADS, M_HD, M_HD), F32)],
        compiler_params=_params(1),
        name="prompt_ffn",
    )(sc, x1.reshape(n_tok, D_MODEL), p_prompt[0].reshape(n_tok, P_DIM), q, kw, v,
      state_mlstm_C[0], *ffn_w)

    y_sample = pl.pallas_call(
        _back_kernel,
        out_shape=tok,
        scratch_shapes=[pltpu.VMEM((S, D_MODEL), BF)],
        compiler_params=pltpu.CompilerParams(vmem_limit_bytes=VMEM_LIMIT),
        name="sample_back",
    )(xs, qc, hv, hc, yl, gm, p_sample[0].reshape(S, P_DIM), mg, wout, *ffn_w)

    lead = lambda a: a[None]
    return (y_prompt.reshape(B, T, D_MODEL), y_sample.reshape(S, 1, D_MODEL),
            lead(p_lconv), p_h.reshape(1, B, D_MODEL), lead(p_mconv), lead(p_c), lead(p_n),
            p_m.reshape(1, B, M_HEADS),
            lead(jnp.swapaxes(s_lbuf, 0, 1)), lead(s_h), lead(jnp.swapaxes(s_mbuf, 0, 1)),
            lead(s_c), s_n.reshape(1, S, M_HEADS, M_HD), lead(s_m))
```

```python
import numpy as np

import jax
import jax.numpy as jnp
from jax import lax
from jax.experimental import pallas as pl
from jax.experimental.pallas import tpu as pltpu

D_MODEL = 1024
M_HEADS = 4
M_HD = D_MODEL // M_HEADS
LRU_BLOCKS = 8
LRU_BLK = D_MODEL // LRU_BLOCKS
LRU_C = 8.0
CONV_W = 4
D_FF = 2816
P_DIM = 256
EPS = 1e-6
M_INIT = -1e30
NEG_LOG2E = -1.4426950408889634
MASKED = -1e30
N_GATES = 2 * M_HEADS
GATE_PAD = 128
SUBLANES = 8
SEQ_TILE = 256
SEG = SEQ_TILE // SUBLANES
SUB_TILES = 2
FFN_TILE = 512
FFN_ROWS = 256
FFN_SUB = FFN_TILE // FFN_ROWS
FF_CHUNK = 256
STATE_BATCH = 4
PREP_COLS = 512
BRANCH_STEPS = 3 * D_MODEL // PREP_COLS
VMEM_LIMIT = 56 * 1024 * 1024

BF = jnp.bfloat16
F32 = jnp.float32


def _dot(a, b):
    return jnp.dot(a, b, preferred_element_type=F32)


def _dot_nt(a, b):
    return lax.dot_general(a, b, (((1,), (1,)), ((), ())), preferred_element_type=F32)


def _dot_tn(a, b):
    return lax.dot_general(a, b, (((0,), (0,)), ((), ())), preferred_element_type=F32)


def _sigmoid(x):
    return 1.0 / (1.0 + jnp.exp2(x * NEG_LOG2E))


def _softplus(x):
    return jnp.maximum(x, 0.0) + jnp.log1p(jnp.exp(-jnp.abs(x)))


def _rms(x, g):
    return x * lax.rsqrt(jnp.mean(x * x, axis=-1, keepdims=True) + EPS) * g


def _group(x, i):
    return x[i * SUBLANES:(i + 1) * SUBLANES, :]


def _lru_coeffs(xc, ga, gx, lam):
    r = _sigmoid(ga)
    ig = _sigmoid(gx)
    log_a = -LRU_C * r * _softplus(-lam)
    a = jnp.exp(log_a)
    om = 1.0 - a * a
    root = jnp.where(om > 0.0, om * lax.rsqrt(om), 0.0)
    return a, root * ig * xc


def _lru_gates(xcb, wax_ref, ba, bx):
    ga, gx = [], []
    for n in range(LRU_BLOCKS):
        g = _dot(xcb[:, n * LRU_BLK:(n + 1) * LRU_BLK], wax_ref[n])
        ga.append(g[:, :LRU_BLK])
        gx.append(g[:, LRU_BLK:])
    return jnp.concatenate(ga, axis=1) + ba, jnp.concatenate(gx, axis=1) + bx


def _scan_interleaved(a, u, h0):
    prods, sums = [], []
    p = s = None
    for i in range(SEG):
        ai, ui = _group(a, i), _group(u, i)
        p, s = (ai, ui) if i == 0 else (ai * p, ai * s + ui)
        prods.append(p)
        sums.append(s)
    c = h0
    starts = [c]
    for j in range(SUBLANES - 1):
        c = p[j:j + 1, :] * c + s[j:j + 1, :]
        starts.append(c)
    start = jnp.concatenate(starts, axis=0)
    return jnp.concatenate([sums[i] + prods[i] * start for i in range(SEG)], axis=0)


def _cumsum_interleaved(x):
    acc = []
    run = None
    for i in range(SEG):
        run = _group(x, i) if i == 0 else run + _group(x, i)
        acc.append(run)
    sub = lax.broadcasted_iota(jnp.int32, run.shape, 0)
    inc = run
    s = 1
    while s < SUBLANES:
        inc = inc + jnp.where(sub >= s, pltpu.roll(inc, s, 0), 0.0)
        s *= 2
    before = inc - run
    return jnp.concatenate([r + before for r in acc], axis=0)


def _conv_interleaved(carry, x_new, w, b):
    sub = lax.broadcasted_iota(jnp.int32, (SUBLANES, x_new.shape[1]), 0)
    head, new = [], []
    for r in range(CONV_W - 1):
        cur = _group(x_new, SEG - (CONV_W - 1) + r)
        head.append(jnp.where(sub == 0, carry[r:r + 1, :], pltpu.roll(cur, 1, 0)))
        new.append(cur[SUBLANES - 1:SUBLANES, :])
    ext = jnp.concatenate(head + [x_new], axis=0)
    out = b
    for j in range(CONV_W):
        out = out + ext[j * SUBLANES:j * SUBLANES + SEQ_TILE, :] * w[j:j + 1, :]
    return out, jnp.concatenate(new, axis=0)


def _time_of_row(r):
    return (r & (SUBLANES - 1)) * SEG + (r >> (SUBLANES.bit_length() - 1))


RESULT_DELAY = {"M": 250, "V": 40}
COST_SCALE = {"M": 1.0, "V": 0.75}


class _Plan:
    def __init__(self):
        self.tasks = {}

    def add(self, name, unit, cost, deps, fn):
        self.tasks[name] = (unit, cost * COST_SCALE[unit], tuple(deps), fn)

    def order(self):
        succ = {n: [] for n in self.tasks}
        for n, (_, _, deps, _) in self.tasks.items():
            for p in deps:
                succ[p].append(n)
        tail = {}

        def path(n):
            if n not in tail:
                tail[n] = self.tasks[n][1] + max([path(s) for s in succ[n]], default=0)
            return tail[n]

        free = {"M": 0, "V": 0}
        done, order, left = {}, [], list(self.tasks)
        while left:
            ready = [n for n in left if all(p in done for p in self.tasks[n][2])]

            def start(n):
                unit, _, deps, _ = self.tasks[n]
                return max([free[unit]] + [done[p] for p in deps])

            n = min(ready, key=lambda n: (start(n), -path(n)))
            unit, cost, _, _ = self.tasks[n]
            st = start(n)
            free[unit] = st + cost
            done[n] = st + cost + RESULT_DELAY[unit]
            order.append((st, len(order), n))
            left.remove(n)
        return [n for _, _, n in sorted(order)]

    def run(self):
        for n in self.order():
            self.tasks[n][3]()


def _mixer_kernel(x_ref, perm_ref, permt_ref, gmix_ref, wbr_ref, wmg_ref, wif_ref, bif_ref, lcw_ref,
                  lcb_ref, wax_ref, ba_ref, bx_ref, lam_ref, mcw_ref, mcb_ref, wq_ref, wk_ref,
                  wv_ref, mg_ref, wout_ref,
                  x1_ref, lconv_ref, h_ref, mconv_ref, c_ref, n_ref, m_ref):
    tt = SEQ_TILE
    last = tt - 1

    @pl.when(pl.program_id(1) == 0)
    def _():
        lconv_ref[...] = jnp.zeros(lconv_ref.shape, F32)
        mconv_ref[...] = jnp.zeros(mconv_ref.shape, F32)
        h_ref[...] = jnp.zeros(h_ref.shape, F32)
        c_ref[...] = jnp.zeros(c_ref.shape, F32)
        n_ref[...] = jnp.zeros(n_ref.shape, F32)
        m_ref[...] = jnp.full(m_ref.shape, M_INIT, F32)

    lcw, lcb, mcw, mcb = lcw_ref[...], lcb_ref[...], mcw_ref[...], mcb_ref[...]
    ba, bx, lam, mg = ba_ref[...], bx_ref[...], lam_ref[...], mg_ref[...]

    v = {}
    for g in range(M_HEADS):
        gs = slice(g * M_HD, (g + 1) * M_HD)
        v["lconv", -1, g], v["mconv", -1, g] = lconv_ref[0, :, gs], mconv_ref[0, :, gs]
        v["h", -1, g] = h_ref[0, :, gs]
        v["c", -1, g], v["n", -1, g] = c_ref[0, g], n_ref[0, g:g + 1, :]
        v["m", -1, g] = m_ref[0, :, g:g + 1]
    plan = _Plan()

    for t in range(SUB_TILES):
        dep_prev = (lambda name, g, t=t: ((name, t - 1, g),)) if t else (lambda name, g: ())

        def t_norm(t=t):
            v["x", t] = x_ref[0, t * tt:(t + 1) * tt, :]
            v["xn_t", t] = _rms(v["x", t], gmix_ref[...]).astype(BF)
        plan.add(("norm", t), "V", 650, (), t_norm)

        def t_perm(t=t):
            v["xnb", t] = _dot(perm_ref[...], v["xn_t", t]).astype(BF)
        plan.add(("perm", t), "M", 260, (("norm", t),), t_perm)

        def proj(name, w_ref, col0, g, t=t):
            def run():
                v[name, t, g] = _dot(v["xnb", t], w_ref[:, col0 + g * M_HD:col0 + (g + 1) * M_HD])
            plan.add((name, t, g), "M", 260, (("perm", t),), run)

        def t_wif(t=t):
            v["pre", t] = _dot(v["xnb", t], wif_ref[...]) + bif_ref[...]
        plan.add(("wif", t), "M", 260, (("perm", t),), t_wif)

        def t_gcum(t=t):
            pre = v["pre", t]
            lane = lax.broadcasted_iota(jnp.int32, (tt, GATE_PAD), 1)
            v["gcol", t] = jnp.where(lane < M_HEADS, pre, _cumsum_interleaved(-_softplus(-pre)))
            v["grow", t] = v["gcol", t].T
            v["tri", t] = (_time_of_row(lax.broadcasted_iota(jnp.int32, (tt, 1), 0))
                           >= _time_of_row(lax.broadcasted_iota(jnp.int32, (1, tt), 1)))
        plan.add(("gcum", t), "V", 150, (("wif", t),), t_gcum)

        for g in range(M_HEADS):
            gs = slice(g * M_HD, (g + 1) * M_HD)
            for name, w_ref, col0 in (("xl", wbr_ref, 0), ("xm", wbr_ref, D_MODEL),
                                      ("om", wbr_ref, 2 * D_MODEL), ("gl", wmg_ref, 0),
                                      ("gm", wmg_ref, D_MODEL)):
                proj(name, w_ref, col0, g)

            def t_convl(t=t, g=g, gs=gs):
                v["xlc", t, g], v["lconv", t, g] = _conv_interleaved(
                    v["lconv", t - 1, g], v["xl", t, g], lcw[:, gs], lcb[:, gs])
            plan.add(("convl", t, g), "V", 135, (("xl", t, g),) + dep_prev("convl", g), t_convl)

            def t_gates(t=t, g=g, gs=gs):
                xlb = v["xlc", t, g].astype(BF)
                ga, gx = [], []
                for n in range(2):
                    gg = _dot(xlb[:, n * LRU_BLK:(n + 1) * LRU_BLK], wax_ref[2 * g + n])
                    ga.append(gg[:, :LRU_BLK])
                    gx.append(gg[:, LRU_BLK:])
                v["ga", t, g] = jnp.concatenate(ga, axis=1) + ba[:, gs]
                v["gx", t, g] = jnp.concatenate(gx, axis=1) + bx[:, gs]
            plan.add(("gates", t, g), "M", 130, (("convl", t, g),), t_gates)

            def t_coef(t=t, g=g, gs=gs):
                v["a", t, g], v["u", t, g] = _lru_coeffs(v["xlc", t, g], v["ga", t, g], v["gx", t, g],
                                                         lam[:, gs])
            plan.add(("coef", t, g), "V", 280, (("gates", t, g),), t_coef)

            def t_scan(t=t, g=g):
                v["yl", t, g] = _scan_interleaved(v["a", t, g], v["u", t, g], v["h", t - 1, g])
                v["h", t, g] = v["yl", t, g][last:last + 1, :]
            plan.add(("scan", t, g), "V", 120, (("coef", t, g),) + dep_prev("scan", g), t_scan)

            h = g

            def t_convm(t=t, h=h, gs=gs):
                xm_c, v["mconv", t, h] = _conv_interleaved(v["mconv", t - 1, h], v["xm", t, h],
                                                           mcw[:, gs], mcb[:, gs])
                v["xcb", t, h] = (xm_c * _sigmoid(xm_c)).astype(BF)
            plan.add(("convm", t, h), "V", 210, (("xm", t, h),) + dep_prev("convm", h), t_convm)

            def t_qkv(t=t, h=h):
                v["q", t, h] = _dot(v["xcb", t, h], wq_ref[h]) * (M_HD ** -0.5)
                v["k", t, h] = _dot(v["xcb", t, h], wk_ref[h])
                vv = _dot(v["xm", t, h].astype(BF), wv_ref[h])
                v["qb", t, h], v["kb", t, h], v["vb", t, h] = (
                    v["q", t, h].astype(BF), v["k", t, h].astype(BF), vv.astype(BF))
            plan.add(("qkv", t, h), "M", 200, (("convm", t, h),), t_qkv)

            def t_qk(t=t, h=h):
                v["qk", t, h] = _dot_nt(v["qb", t, h], v["kb", t, h])
            plan.add(("qk", t, h), "M", 64, (("qkv", t, h),), t_qk)

            def t_sp(t=t, h=h):
                gcol, grow = v["gcol", t], v["grow", t]
                b_col = gcol[:, M_HEADS + h:M_HEADS + h + 1]
                ig_row, b_row = grow[h:h + 1, :], grow[M_HEADS + h:M_HEADS + h + 1, :]
                dlog = jnp.where(v["tri", t], b_col - b_row + ig_row, MASKED)
                m_inter = b_col + v["m", t - 1, h]
                m_t = jnp.maximum(m_inter, jnp.max(dlog, axis=1, keepdims=True))
                s = v["qk", t, h] * jnp.exp(dlog - m_t)
                v["m_t", t, h], v["sc", t, h] = m_t, jnp.exp(m_inter - m_t)
                v["ssum", t, h] = jnp.sum(s, axis=1, keepdims=True)
                v["sb", t, h] = s.astype(BF)
                v["m", t, h] = m_t[last:last + 1, :]
            plan.add(("sp", t, h), "V", 200, (("qk", t, h), ("gcum", t)) + dep_prev("sp", h), t_sp)

            def t_sv(t=t, h=h):
                v["sv", t, h] = _dot(v["sb", t, h], v["vb", t, h])
                v["qc", t, h] = _dot(v["qb", t, h], v["c", t - 1, h].astype(BF))
            plan.add(("sv", t, h), "M", 130, (("sp", t, h),) + dep_prev("cnew", h), t_sv)

            def t_hn(t=t, h=h, gs=gs):
                sc, m_t = v["sc", t, h], v["m_t", t, h]
                num = v["sv", t, h] + sc * v["qc", t, h]
                den = v["ssum", t, h] + sc * jnp.sum(v["q", t, h] * v["n", t - 1, h], axis=1,
                                                     keepdims=True)
                hh = num / jnp.maximum(jnp.abs(den), jnp.exp(-m_t))
                v["hn", t, h] = (hh * lax.rsqrt(jnp.mean(hh * hh, axis=-1, keepdims=True) + EPS)
                                 * mg[:, gs])
            plan.add(("hn", t, h), "V", 170, (("sv", t, h),) + dep_prev("kw", h), t_hn)

            def t_kw(t=t, h=h):
                gcol = v["gcol", t]
                ig_col, b_col = gcol[:, h:h + 1], gcol[:, M_HEADS + h:M_HEADS + h + 1]
                m_last = v["m", t, h]
                b_last = b_col[last:last + 1, :]
                kw = v["k", t, h] * jnp.exp(b_last - b_col + ig_col - m_last)
                v["dec", t, h] = jnp.exp(b_last + v["m", t - 1, h] - m_last)
                v["n", t, h] = v["dec", t, h] * v["n", t - 1, h] + jnp.sum(kw, axis=0, keepdims=True)
                v["kwb", t, h] = kw.astype(BF)
            plan.add(("kw", t, h), "V", 80, (("sp", t, h),) + dep_prev("kw", h), t_kw)

            def t_ckv(t=t, h=h):
                v["ckv", t, h] = _dot_tn(v["kwb", t, h], v["vb", t, h])
            plan.add(("ckv", t, h), "M", 64, (("kw", t, h),), t_ckv)

            def t_cnew(t=t, h=h):
                v["c", t, h] = v["dec", t, h] * v["c", t - 1, h] + v["ckv", t, h]
            plan.add(("cnew", t, h), "V", 40, (("ckv", t, h),) + dep_prev("cnew", h), t_cnew)

            def t_sig(t=t, h=h):
                v["gate_l", t, h] = _sigmoid(v["gl", t, h]) * v["yl", t, h]
                v["gate_m", t, h] = _sigmoid(v["gm", t, h]) * _sigmoid(v["om", t, h])
            plan.add(("sig", t, h), "V", 180,
                     (("gl", t, h), ("gm", t, h), ("om", t, h), ("scan", t, h)), t_sig)

            def t_mrg(t=t, h=h):
                v["mrg", t, h] = (v["gate_l", t, h] + v["gate_m", t, h] * v["hn", t, h]).astype(BF)
            plan.add(("mrg", t, h), "V", 40, (("sig", t, h), ("hn", t, h)), t_mrg)

            def t_pt(t=t, h=h):
                v["mrg_t", t, h] = _dot(permt_ref[...], v["mrg", t, h]).astype(BF)
            plan.add(("pt", t, h), "M", 64, (("mrg", t, h),), t_pt)

            def t_wo(t=t, h=h, gs=gs):
                part = _dot(v["mrg_t", t, h], wout_ref[gs, :])
                v["out", t] = part if h == 0 else v["out", t] + part
            plan.add(("wo", t, h), "M", 260, (("pt", t, h),) + ((("wo", t, h - 1),) if h else ()),
                     t_wo)

        def t_fin(t=t):
            x1_ref[0, t * tt:(t + 1) * tt, :] = v["x", t] + v["out", t]
        plan.add(("fin", t), "V", 70, (("wo", t, M_HEADS - 1),), t_fin)

    def t_state():
        e = SUB_TILES - 1
        heads = range(M_HEADS)
        lconv_ref[0] = jnp.concatenate([v["lconv", e, g] for g in heads], axis=1)
        mconv_ref[0] = jnp.concatenate([v["mconv", e, g] for g in heads], axis=1)
        h_ref[0] = jnp.concatenate([v["h", e, g] for g in heads], axis=1)
        for h in heads:
            c_ref[0, h] = v["c", e, h]
        n_ref[0] = jnp.concatenate([v["n", e, h] for h in heads], axis=0)
        m_ref[0] = jnp.concatenate([v["m", e, h] for h in heads], axis=1)
    plan.add("state", "V", 60, tuple(("cnew", SUB_TILES - 1, h) for h in range(M_HEADS))
             + tuple(("scan", SUB_TILES - 1, h) for h in range(M_HEADS))
             + tuple(("fin", t) for t in range(SUB_TILES)), t_state)

    plan.run()


def _split_kernel(wt_ref, gt_ref, wbr_ref, wmg_ref, wif_ref):
    j = pl.program_id(0)
    piece = wt_ref[...].T.astype(BF)

    @pl.when(j < BRANCH_STEPS)
    def _():
        wbr_ref[...] = piece

    @pl.when(j >= BRANCH_STEPS)
    def _():
        wmg_ref[...] = piece

    @pl.when(j == 0)
    def _():
        gates = gt_ref[...].T
        lane = lax.broadcasted_iota(jnp.int32, gates.shape, 1)
        wif_ref[...] = jnp.where(lane < N_GATES, gates, 0.0).astype(BF)


def _ffn_tail(x1, p, gffn_ref, wg_ref, wu_ref, wd_ref, gple_ref, wpg_ref, wple_ref, gfin_ref):
    xnb = _rms(x1, gffn_ref[...]).astype(BF)
    x2 = x1
    for c in range(D_FF // FF_CHUNK):
        cs = slice(c * FF_CHUNK, (c + 1) * FF_CHUNK)
        hg = _dot(xnb, wg_ref[:, cs])
        hu = _dot(xnb, wu_ref[:, cs])
        act = (hg * _sigmoid(hg) * hu).astype(BF)
        x2 = x2 + _dot(act, wd_ref[cs, :])
    gate = _sigmoid(_dot(_rms(x2, gple_ref[...]).astype(BF), wpg_ref[...]))
    x3 = x2 + gate * _dot(p.astype(BF), wple_ref[...])
    return _rms(x3, gfin_ref[...])


def _ffn_kernel(sc_ref, x1_ref, p_ref, q_ref, kw_ref, v_ref, c_ref, gffn_ref, wg_ref, wu_ref,
                wd_ref, gple_ref, wpg_ref, wple_ref, gfin_ref, y_ref, qc_out, c_out):
    tt = FFN_ROWS
    v = {}
    plan = _Plan()

    for t in range(FFN_SUB):
        rows = slice(t * tt, (t + 1) * tt)

        def t_norm(t=t, rows=rows):
            v["x", t] = x1_ref[rows, :]
            v["xnb", t] = _rms(v["x", t], gffn_ref[...]).astype(BF)
        plan.add(("norm", t), "V", 650, (), t_norm)

        for c in range(D_FF // FF_CHUNK):
            cs = slice(c * FF_CHUNK, (c + 1) * FF_CHUNK)

            def t_gu(t=t, c=c, cs=cs):
                v["hg", t, c] = _dot(v["xnb", t], wg_ref[:, cs])
                v["hu", t, c] = _dot(v["xnb", t], wu_ref[:, cs])
            plan.add(("gu", t, c), "M", 512, (("norm", t),), t_gu)

            def t_act(t=t, c=c):
                hg = v["hg", t, c]
                v["act", t, c] = (hg * _sigmoid(hg) * v["hu", t, c]).astype(BF)
            plan.add(("act", t, c), "V", 110, (("gu", t, c),), t_act)

            def t_dn(t=t, c=c, cs=cs):
                prev = v["x", t] if c == 0 else v["x2", t]
                v["x2", t] = prev + _dot(v["act", t, c], wd_ref[cs, :])
            plan.add(("dn", t, c), "M", 300, (("act", t, c),) + ((("dn", t, c - 1),) if c else ()),
                     t_dn)

        last_dn = ("dn", t, D_FF // FF_CHUNK - 1)

        def t_norm2(t=t):
            v["xgb", t] = _rms(v["x2", t], gple_ref[...]).astype(BF)
        plan.add(("norm2", t), "V", 650, (last_dn,), t_norm2)

        def t_ple(t=t, rows=rows):
            v["pe", t] = _dot(p_ref[rows, :].astype(BF), wple_ref[...])
        plan.add(("ple", t), "M", 300, (), t_ple)

        def t_wpg(t=t):
            v["gpre", t] = _dot(v["xgb", t], wpg_ref[...])
        plan.add(("wpg", t), "M", 1024, (("norm2", t),), t_wpg)

        def t_fin(t=t, rows=rows):
            x3 = v["x2", t] + _sigmoid(v["gpre", t]) * v["pe", t]
            y_ref[rows, :] = _rms(x3, gfin_ref[...])
        plan.add(("fin", t), "V", 1000, (("wpg", t), ("ple", t)), t_fin)

    base = pl.program_id(0) * STATE_BATCH
    heads = [slice(h * M_HD, (h + 1) * M_HD) for h in range(M_HEADS)]
    for bb in range(STATE_BATCH):
        def t_cols(bb=bb):
            row = pl.ds(base + bb, 1)
            q, kw = q_ref[row, :], kw_ref[row, :]
            v["cols", bb] = jnp.concatenate([q[:, hs] for hs in heads] + [kw[:, hs] for hs in heads],
                                            axis=0).T
            v["vrow", bb] = v_ref[row, :]
        plan.add(("cols", bb), "V", 40, (), t_cols)

        for h, hs in enumerate(heads):
            def t_pair(bb=bb, h=h, hs=hs):
                cols = v["cols", bb]
                q_col, kw_col = cols[:, h:h + 1], cols[:, M_HEADS + h:M_HEADS + h + 1]
                c_prev = c_ref[bb, h]
                qc_out[pl.ds(base + bb, 1), hs] = jnp.sum(q_col * c_prev, axis=0, keepdims=True)
                c_out[bb, h] = sc_ref[base + bb, h] * c_prev + kw_col * v["vrow", bb][:, hs]
            plan.add(("pair", bb, h), "V", 170, (("cols", bb),), t_pair)

    plan.run()


def _front_kernel(x_ref, lbuf_ref, h0_ref, mbuf_ref, gmix_ref, wbr_ref, wmg_ref, wif_ref, bif_ref,
                  lcw_ref, lcb_ref, wax_ref, ba_ref, bx_ref, lam_ref, mcw_ref, mcb_ref,
                  wq_ref, wk_ref, wv_ref, m0_ref, n0_ref,
                  q_ref, kw_ref, v_ref, sc_ref, hv_ref, hc_ref, n_out, m_out,
                  yl_ref, gm_ref, lbuf_out, h_out, mbuf_out):
    xnb = _rms(x_ref[...], gmix_ref[...]).astype(BF)

    def per_head(cols):
        return jnp.concatenate([jnp.broadcast_to(cols[:, h:h + 1], (cols.shape[0], M_HD))
                                for h in range(M_HEADS)], axis=1)

    def head_sums(a):
        return jnp.concatenate([jnp.sum(a[:, h * M_HD:(h + 1) * M_HD], axis=1, keepdims=True)
                                for h in range(M_HEADS)], axis=1)

    def conv_step(buf_ref, buf_out, x_new, w_ref, b_ref):
        out = b_ref[...] + x_new * w_ref[CONV_W - 1:CONV_W, :]
        for j in range(CONV_W - 1):
            out = out + buf_ref[j] * w_ref[j:j + 1, :]
        for j in range(CONV_W - 2):
            buf_out[j] = buf_ref[j + 1]
        buf_out[CONV_W - 2] = x_new
        return out

    x_l = _dot(xnb, wbr_ref[:, 0:D_MODEL])
    xl_c = conv_step(lbuf_ref, lbuf_out, x_l, lcw_ref, lcb_ref)
    ga, gx = _lru_gates(xl_c.astype(BF), wax_ref, ba_ref[...], bx_ref[...])
    a, u = _lru_coeffs(xl_c, ga, gx, lam_ref[...])
    y_l = a * h0_ref[...] + u
    h_out[...] = y_l
    g_l = _dot(xnb, wmg_ref[:, 0:D_MODEL])
    yl_ref[...] = _sigmoid(g_l) * y_l

    x_m = _dot(xnb, wbr_ref[:, D_MODEL:2 * D_MODEL])
    xm_c = conv_step(mbuf_ref, mbuf_out, x_m, mcw_ref, mcb_ref)
    xcb = (xm_c * _sigmoid(xm_c)).astype(BF)
    xmb = x_m.astype(BF)
    q, k, v = [], [], []
    for h in range(M_HEADS):
        hs = slice(h * M_HD, (h + 1) * M_HD)
        q.append(_dot(xcb[:, hs], wq_ref[h]) * (M_HD ** -0.5))
        k.append(_dot(xcb[:, hs], wk_ref[h]))
        v.append(_dot(xmb[:, hs], wv_ref[h]))
    q, k, v = (jnp.concatenate(a, axis=1) for a in (q, k, v))
    o_m = _dot(xnb, wbr_ref[:, 2 * D_MODEL:3 * D_MODEL])
    g_m = _dot(xnb, wmg_ref[:, D_MODEL:2 * D_MODEL])
    gm_ref[...] = _sigmoid(g_m) * _sigmoid(o_m)

    pre = _dot(xnb, wif_ref[...]) + bif_ref[...]
    ig = pre[:, 0:M_HEADS]
    lf = -_softplus(-pre[:, M_HEADS:N_GATES])
    n_prev = n0_ref[...]
    m_inter = lf + m0_ref[...]
    m_t = jnp.maximum(m_inter, ig)
    wk = jnp.exp(ig - m_t)
    sc = jnp.exp(m_inter - m_t)
    s = head_sums(q * k) * wk
    den = s + sc * head_sums(q * n_prev)
    rden = 1.0 / jnp.maximum(jnp.abs(den), jnp.exp(-m_t))
    wk_d, sc_d = per_head(wk), per_head(sc)
    hv_ref[...] = per_head(s * rden) * v
    hc_ref[...] = per_head(sc * rden)
    q_ref[...] = q
    kw_ref[...] = wk_d * k
    v_ref[...] = v
    sc_ref[...] = sc
    n_out[...] = sc_d * n_prev + wk_d * k
    m_out[...] = m_t


def _back_kernel(x_ref, qc_ref, hv_ref, hc_ref, yl_ref, gm_ref, p_ref, mg_ref, wout_ref, gffn_ref,
                 wg_ref, wu_ref, wd_ref, gple_ref, wpg_ref, wple_ref, gfin_ref, y_ref, mrg_ref):
    for h in range(M_HEADS):
        hs = slice(h * M_HD, (h + 1) * M_HD)
        hh = hv_ref[:, hs] + hc_ref[:, hs] * qc_ref[:, hs]
        hn = hh * lax.rsqrt(jnp.mean(hh * hh, axis=-1, keepdims=True) + EPS) * mg_ref[:, hs]
        mrg_ref[:, hs] = (yl_ref[:, hs] + gm_ref[:, hs] * hn).astype(BF)
    x1 = x_ref[...] + _dot(mrg_ref[...], wout_ref[...])
    y_ref[...] = _ffn_tail(x1, p_ref[...], gffn_ref, wg_ref, wu_ref, wd_ref, gple_ref, wpg_ref,
                           wple_ref, gfin_ref)


def _resident(shape):
    nd = len(shape)
    return pl.BlockSpec(shape, lambda *_: (0,) * nd, pipeline_mode=pl.Buffered(1))


def _params(n_axes):
    return pltpu.CompilerParams(dimension_semantics=("arbitrary",) * n_axes,
                                vmem_limit_bytes=VMEM_LIMIT)


def _interleave_matrix():
    r = np.arange(SEQ_TILE)
    perm = np.zeros((SEQ_TILE, SEQ_TILE), np.float32)
    perm[r, (r % SUBLANES) * SEG + r // SUBLANES] = 1.0
    return perm


def kernel(x_prompt, x_sample, state_lru_conv, state_lru_h, state_mlstm_conv, state_mlstm_C, state_mlstm_n, state_mlstm_m, p_prompt, p_sample, norm_mix_g, w_in, b_gates, lru_conv_w, lru_conv_b, lru_w_a, lru_b_a, lru_w_x, lru_b_x, lru_lambda, mlstm_conv_w, mlstm_conv_b, w_q, w_k, w_v, mlstm_norm_g, w_out, norm_ffn_g, w_ffn_gate, w_ffn_up, w_ffn_down, norm_ple_g, w_ple_gate, w_ple, final_norm_g):
    assert w_in.shape[0] == 1, "single-layer trunk"
    B, T, _ = x_prompt.shape
    S = x_sample.shape[0]
    step = SEQ_TILE * SUB_TILES
    assert T % step == 0 and (B * T) % FFN_TILE == 0 and x_sample.shape[1] == 1
    assert S == STATE_BATCH * ((B * T) // FFN_TILE), "one state batch per ffn grid step"

    sds = jax.ShapeDtypeStruct
    assert w_in.shape[1:] == (D_MODEL, 5 * D_MODEL + N_GATES)
    wt = jnp.swapaxes(w_in[0], 0, 1)
    g0 = 3 * D_MODEL
    src_row = lambda j: pl.multiple_of(
        jnp.where(j < BRANCH_STEPS, j * PREP_COLS, g0 + N_GATES + (j - BRANCH_STEPS) * PREP_COLS),
        SUBLANES)
    slab = lambda rows, index: pl.BlockSpec((pl.Element(rows), pl.Element(D_MODEL)), index)
    cols = lambda index: pl.BlockSpec((D_MODEL, PREP_COLS), index)
    wbr, wmg, wif = pl.pallas_call(
        _split_kernel,
        grid=(5 * D_MODEL // PREP_COLS,),
        in_specs=[slab(PREP_COLS, lambda j: (src_row(j), 0)), slab(GATE_PAD, lambda j: (g0, 0))],
        out_specs=[cols(lambda j: (0, jnp.minimum(j, BRANCH_STEPS - 1))),
                   cols(lambda j: (0, jnp.maximum(j - BRANCH_STEPS, 0))),
                   pl.BlockSpec((D_MODEL, GATE_PAD), lambda j: (0, 0))],
        out_shape=[sds((D_MODEL, 3 * D_MODEL), BF), sds((D_MODEL, 2 * D_MODEL), BF),
                   sds((D_MODEL, GATE_PAD), BF)],
        compiler_params=_params(1),
        name="split_w_in",
    )(wt, wt)
    bif = jnp.pad(b_gates[0], (0, GATE_PAD - N_GATES)).reshape(1, GATE_PAD)
    wax = jnp.concatenate([lru_w_a[0], lru_w_x[0]], axis=2).astype(BF)
    row = lambda a: a.reshape(1, -1)
    gmix, ba, bx, lam = row(norm_mix_g[0]), row(lru_b_a[0]), row(lru_b_x[0]), row(lru_lambda[0])
    lcw, lcb = lru_conv_w[0], row(lru_conv_b[0])
    mcw, mcb = mlstm_conv_w[0], row(mlstm_conv_b[0])
    wq, wk, wv = w_q[0].astype(BF), w_k[0].astype(BF), w_v[0].astype(BF)
    mg = row(mlstm_norm_g[0])
    wout = w_out[0].astype(BF)
    gffn, gple, gfin = row(norm_ffn_g[0]), row(norm_ple_g[0]), row(final_norm_g)
    wg, wu, wd = w_ffn_gate[0].astype(BF), w_ffn_up[0].astype(BF), w_ffn_down[0].astype(BF)
    wpg, wple = w_ple_gate[0].astype(BF), w_ple[0].astype(BF)
    perm_np = _interleave_matrix()
    perm, perm_t = jnp.asarray(perm_np, BF), jnp.asarray(perm_np.T, BF)

    front_w = (gmix, wbr, wmg, wif, bif, lcw, lcb, wax, ba, bx, lam, mcw, mcb, wq, wk, wv)
    mixer_w = (perm, perm_t) + front_w + (mg, wout)
    ffn_w = (gffn, wg, wu, wd, gple, wpg, wple, gfin)

    nt = T // step
    x1, p_lconv, p_h, p_mconv, p_c, p_n, p_m = pl.pallas_call(
        _mixer_kernel,
        grid=(B, nt),
        in_specs=[pl.BlockSpec((1, step, D_MODEL), lambda b, t: (b, t, 0))]
                 + [_resident(w.shape) for w in mixer_w],
        out_specs=[
            pl.BlockSpec((1, step, D_MODEL), lambda b, t: (b, t, 0)),
            pl.BlockSpec((1, CONV_W - 1, D_MODEL), lambda b, t: (b, 0, 0)),
            pl.BlockSpec((1, 1, D_MODEL), lambda b, t: (b, 0, 0)),
            pl.BlockSpec((1, CONV_W - 1, D_MODEL), lambda b, t: (b, 0, 0)),
            pl.BlockSpec((1, M_HEADS, M_HD, M_HD), lambda b, t: (b, 0, 0, 0)),
            pl.BlockSpec((1, M_HEADS, M_HD), lambda b, t: (b, 0, 0)),
            pl.BlockSpec((1, 1, M_HEADS), lambda b, t: (b, 0, 0)),
        ],
        out_shape=[
            sds((B, T, D_MODEL), F32),
            sds((B, CONV_W - 1, D_MODEL), F32),
            sds((B, 1, D_MODEL), F32),
            sds((B, CONV_W - 1, D_MODEL), F32),
            sds((B, M_HEADS, M_HD, M_HD), F32),
            sds((B, M_HEADS, M_HD), F32),
            sds((B, 1, M_HEADS), F32),
        ],
        compiler_params=_params(2),
        name="prompt_mixer",
    )(x_prompt, *mixer_w)

    xs = x_sample.reshape(S, D_MODEL)
    lbuf = jnp.swapaxes(state_lru_conv[0], 0, 1)
    mbuf = jnp.swapaxes(state_mlstm_conv[0], 0, 1)
    tok = sds((S, D_MODEL), F32)
    buf = sds((CONV_W - 1, S, D_MODEL), F32)
    per_head = sds((S, M_HEADS), F32)
    q, kw, v, sc, hv, hc, s_n, s_m, yl, gm, s_lbuf, s_h, s_mbuf = pl.pallas_call(
        _front_kernel,
        out_shape=[tok, tok, tok, per_head, tok, tok, tok, per_head, tok, tok, buf, tok, buf],
        compiler_params=pltpu.CompilerParams(vmem_limit_bytes=VMEM_LIMIT),
        name="sample_front",
    )(xs, lbuf, state_lru_h[0], mbuf, *front_w, state_mlstm_m[0],
      state_mlstm_n[0].reshape(S, D_MODEL))

    n_tok = B * T
    tile = lambda width: pl.BlockSpec((FFN_TILE, width), lambda i: (i, 0))
    c_spec = pl.BlockSpec((STATE_BATCH, M_HEADS, M_HD, M_HD), lambda i: (i, 0, 0, 0))
    y_prompt, qc, s_c = pl.pallas_call(
        _ffn_kernel,
        grid=(n_tok // FFN_TILE,),
        in_specs=[pl.BlockSpec(memory_space=pltpu.SMEM), tile(D_MODEL), tile(P_DIM),
                  _resident(tok.shape), _resident(tok.shape), _resident(tok.shape), c_spec]
                 + [_resident(w.shape) for w in ffn_w],
        out_specs=[tile(D_MODEL), pl.BlockSpec(tok.shape, lambda i: (0, 0)), c_spec],
        out_shape=[sds((n_tok, D_MODEL), F32), tok, sds((S, M_HEADS, M_HD, M_HD), F32)],
        compiler_params=_params(1),
        name="prompt_ffn",
    )(sc, x1.reshape(n_tok, D_MODEL), p_prompt[0].reshape(n_tok, P_DIM), q, kw, v,
      state_mlstm_C[0], *ffn_w)

    y_sample = pl.pallas_call(
        _back_kernel,
        out_shape=tok,
        scratch_shapes=[pltpu.VMEM((S, D_MODEL), BF)],
        compiler_params=pltpu.CompilerParams(vmem_limit_bytes=VMEM_LIMIT),
        name="sample_back",
    )(xs, qc, hv, hc, yl, gm, p_sample[0].reshape(S, P_DIM), mg, wout, *ffn_w)

    lead = lambda a: a[None]
    return (y_prompt.reshape(B, T, D_MODEL), y_sample.reshape(S, 1, D_MODEL),
            lead(p_lconv), p_h.reshape(1, B, D_MODEL), lead(p_mconv), lead(p_c), lead(p_n),
            p_m.reshape(1, B, M_HEADS),
            lead(jnp.swapaxes(s_lbuf, 0, 1)), lead(s_h), lead(jnp.swapaxes(s_mbuf, 0, 1)),
            lead(s_c), s_n.reshape(1, S, M_HEADS, M_HD), lead(s_m))
```

```python
import numpy as np

import jax
import jax.numpy as jnp
from jax import lax
from jax.experimental import pallas as pl
from jax.experimental.pallas import tpu as pltpu

D_MODEL = 1024
M_HEADS = 4
M_HD = D_MODEL // M_HEADS
LRU_BLOCKS = 8
LRU_BLK = D_MODEL // LRU_BLOCKS
LRU_C = 8.0
CONV_W = 4
D_FF = 2816
P_DIM = 256
EPS = 1e-6
M_INIT = -1e30
NEG_LOG2E = -1.4426950408889634
MASKED = -1e30
N_GATES = 2 * M_HEADS
GATE_PAD = 128
SUBLANES = 8
SEQ_TILE = 256
SEG = SEQ_TILE // SUBLANES
SUB_TILES = 2
FFN_TILE = 512
FFN_ROWS = 256
FFN_SUB = FFN_TILE // FFN_ROWS
FF_CHUNK = 256
STATE_BATCH = 4
PREP_COLS = 512
BRANCH_STEPS = 3 * D_MODEL // PREP_COLS
VMEM_LIMIT = 56 * 1024 * 1024

BF = jnp.bfloat16
F32 = jnp.float32


def _dot(a, b):
    return jnp.dot(a, b, preferred_element_type=F32)


def _dot_nt(a, b):
    return lax.dot_general(a, b, (((1,), (1,)), ((), ())), preferred_element_type=F32)


def _dot_tn(a, b):
    return lax.dot_general(a, b, (((0,), (0,)), ((), ())), preferred_element_type=F32)


def _sigmoid(x):
    return 1.0 / (1.0 + jnp.exp2(x * NEG_LOG2E))


def _softplus(x):
    return jnp.maximum(x, 0.0) + jnp.log1p(jnp.exp(-jnp.abs(x)))


def _rms(x, g):
    return x * lax.rsqrt(jnp.mean(x * x, axis=-1, keepdims=True) + EPS) * g


def _group(x, i):
    return x[i * SUBLANES:(i + 1) * SUBLANES, :]


def _lru_coeffs(xc, ga, gx, lam):
    r = _sigmoid(ga)
    ig = _sigmoid(gx)
    log_a = -LRU_C * r * _softplus(-lam)
    a = jnp.exp(log_a)
    om = 1.0 - a * a
    root = jnp.where(om > 0.0, om * lax.rsqrt(om), 0.0)
    return a, root * ig * xc


def _lru_gates(xcb, wax_ref, ba, bx):
    ga, gx = [], []
    for n in range(LRU_BLOCKS):
        g = _dot(xcb[:, n * LRU_BLK:(n + 1) * LRU_BLK], wax_ref[n])
        ga.append(g[:, :LRU_BLK])
        gx.append(g[:, LRU_BLK:])
    return jnp.concatenate(ga, axis=1) + ba, jnp.concatenate(gx, axis=1) + bx


def _scan_interleaved(a, u, h0):
    prods, sums = [], []
    p = s = None
    for i in range(SEG):
        ai, ui = _group(a, i), _group(u, i)
        p, s = (ai, ui) if i == 0 else (ai * p, ai * s + ui)
        prods.append(p)
        sums.append(s)
    c = h0
    starts = [c]
    for j in range(SUBLANES - 1):
        c = p[j:j + 1, :] * c + s[j:j + 1, :]
        starts.append(c)
    start = jnp.concatenate(starts, axis=0)
    return jnp.concatenate([sums[i] + prods[i] * start for i in range(SEG)], axis=0)


def _cumsum_interleaved(x):
    acc = []
    run = None
    for i in range(SEG):
        run = _group(x, i) if i == 0 else run + _group(x, i)
        acc.append(run)
    sub = lax.broadcasted_iota(jnp.int32, run.shape, 0)
    inc = run
    s = 1
    while s < SUBLANES:
        inc = inc + jnp.where(sub >= s, pltpu.roll(inc, s, 0), 0.0)
        s *= 2
    before = inc - run
    return jnp.concatenate([r + before for r in acc], axis=0)


def _conv_interleaved(carry, x_new, w, b):
    sub = lax.broadcasted_iota(jnp.int32, (SUBLANES, x_new.shape[1]), 0)
    head, new = [], []
    for r in range(CONV_W - 1):
        cur = _group(x_new, SEG - (CONV_W - 1) + r)
        head.append(jnp.where(sub == 0, carry[r:r + 1, :], pltpu.roll(cur, 1, 0)))
        new.append(cur[SUBLANES - 1:SUBLANES, :])
    ext = jnp.concatenate(head + [x_new], axis=0)
    out = b
    for j in range(CONV_W):
        out = out + ext[j * SUBLANES:j * SUBLANES + SEQ_TILE, :] * w[j:j + 1, :]
    return out, jnp.concatenate(new, axis=0)


def _time_of_row(r):
    return (r & (SUBLANES - 1)) * SEG + (r >> (SUBLANES.bit_length() - 1))


RESULT_DELAY = {"M": 250, "V": 80}
COST_SCALE = {"M": 1.0, "V": 0.75}


class _Plan:
    def __init__(self):
        self.tasks = {}

    def add(self, name, unit, cost, deps, fn):
        self.tasks[name] = (unit, cost * COST_SCALE[unit], tuple(deps), fn)

    def order(self):
        succ = {n: [] for n in self.tasks}
        for n, (_, _, deps, _) in self.tasks.items():
            for p in deps:
                succ[p].append(n)
        tail = {}

        def path(n):
            if n not in tail:
                tail[n] = self.tasks[n][1] + max([path(s) for s in succ[n]], default=0)
            return tail[n]

        free = {"M": 0, "V": 0}
        done, order, left = {}, [], list(self.tasks)
        while left:
            ready = [n for n in left if all(p in done for p in self.tasks[n][2])]

            def start(n):
                unit, _, deps, _ = self.tasks[n]
                return max([free[unit]] + [done[p] for p in deps])

            n = min(ready, key=lambda n: (start(n), -path(n)))
            unit, cost, _, _ = self.tasks[n]
            st = start(n)
            free[unit] = st + cost
            done[n] = st + cost + RESULT_DELAY[unit]
            order.append((st, len(order), n))
            left.remove(n)
        return [n for _, _, n in sorted(order)]

    def run(self):
        for n in self.order():
            self.tasks[n][3]()


def _mixer_kernel(x_ref, perm_ref, permt_ref, gmix_ref, wbr_ref, wmg_ref, wif_ref, bif_ref, lcw_ref,
                  lcb_ref, wax_ref, ba_ref, bx_ref, lam_ref, mcw_ref, mcb_ref, wq_ref, wk_ref,
                  wv_ref, mg_ref, wout_ref,
                  x1_ref, lconv_ref, h_ref, mconv_ref, c_ref, n_ref, m_ref):
    tt = SEQ_TILE
    last = tt - 1

    @pl.when(pl.program_id(1) == 0)
    def _():
        lconv_ref[...] = jnp.zeros(lconv_ref.shape, F32)
        mconv_ref[...] = jnp.zeros(mconv_ref.shape, F32)
        h_ref[...] = jnp.zeros(h_ref.shape, F32)
        c_ref[...] = jnp.zeros(c_ref.shape, F32)
        n_ref[...] = jnp.zeros(n_ref.shape, F32)
        m_ref[...] = jnp.full(m_ref.shape, M_INIT, F32)

    lcw, lcb, mcw, mcb = lcw_ref[...], lcb_ref[...], mcw_ref[...], mcb_ref[...]
    ba, bx, lam, mg = ba_ref[...], bx_ref[...], lam_ref[...], mg_ref[...]

    v = {}
    for g in range(M_HEADS):
        gs = slice(g * M_HD, (g + 1) * M_HD)
        v["lconv", -1, g], v["mconv", -1, g] = lconv_ref[0, :, gs], mconv_ref[0, :, gs]
        v["h", -1, g] = h_ref[0, :, gs]
        v["c", -1, g], v["n", -1, g] = c_ref[0, g], n_ref[0, g:g + 1, :]
        v["m", -1, g] = m_ref[0, :, g:g + 1]
    plan = _Plan()

    for t in range(SUB_TILES):
        dep_prev = (lambda name, g, t=t: ((name, t - 1, g),)) if t else (lambda name, g: ())

        def t_norm(t=t):
            v["x", t] = x_ref[0, t * tt:(t + 1) * tt, :]
            v["xn_t", t] = _rms(v["x", t], gmix_ref[...]).astype(BF)
        plan.add(("norm", t), "V", 650, (), t_norm)

        def t_perm(t=t):
            v["xnb", t] = _dot(perm_ref[...], v["xn_t", t]).astype(BF)
        plan.add(("perm", t), "M", 260, (("norm", t),), t_perm)

        def proj(name, w_ref, col0, g, t=t):
            def run():
                v[name, t, g] = _dot(v["xnb", t], w_ref[:, col0 + g * M_HD:col0 + (g + 1) * M_HD])
            plan.add((name, t, g), "M", 260, (("perm", t),), run)

        def t_wif(t=t):
            v["pre", t] = _dot(v["xnb", t], wif_ref[...]) + bif_ref[...]
        plan.add(("wif", t), "M", 260, (("perm", t),), t_wif)

        def t_gcum(t=t):
            pre = v["pre", t]
            lane = lax.broadcasted_iota(jnp.int32, (tt, GATE_PAD), 1)
            v["gcol", t] = jnp.where(lane < M_HEADS, pre, _cumsum_interleaved(-_softplus(-pre)))
            v["grow", t] = v["gcol", t].T
            v["tri", t] = (_time_of_row(lax.broadcasted_iota(jnp.int32, (tt, 1), 0))
                           >= _time_of_row(lax.broadcasted_iota(jnp.int32, (1, tt), 1)))
        plan.add(("gcum", t), "V", 150, (("wif", t),), t_gcum)

        for g in range(M_HEADS):
            gs = slice(g * M_HD, (g + 1) * M_HD)
            for name, w_ref, col0 in (("xl", wbr_ref, 0), ("xm", wbr_ref, D_MODEL),
                                      ("om", wbr_ref, 2 * D_MODEL), ("gl", wmg_ref, 0),
                                      ("gm", wmg_ref, D_MODEL)):
                proj(name, w_ref, col0, g)

            def t_convl(t=t, g=g, gs=gs):
                v["xlc", t, g], v["lconv", t, g] = _conv_interleaved(
                    v["lconv", t - 1, g], v["xl", t, g], lcw[:, gs], lcb[:, gs])
            plan.add(("convl", t, g), "V", 135, (("xl", t, g),) + dep_prev("convl", g), t_convl)

            def t_gates(t=t, g=g, gs=gs):
                xlb = v["xlc", t, g].astype(BF)
                ga, gx = [], []
                for n in range(2):
                    gg = _dot(xlb[:, n * LRU_BLK:(n + 1) * LRU_BLK], wax_ref[2 * g + n])
                    ga.append(gg[:, :LRU_BLK])
                    gx.append(gg[:, LRU_BLK:])
                v["ga", t, g] = jnp.concatenate(ga, axis=1) + ba[:, gs]
                v["gx", t, g] = jnp.concatenate(gx, axis=1) + bx[:, gs]
            plan.add(("gates", t, g), "M", 130, (("convl", t, g),), t_gates)

            def t_coef(t=t, g=g, gs=gs):
                v["a", t, g], v["u", t, g] = _lru_coeffs(v["xlc", t, g], v["ga", t, g], v["gx", t, g],
                                                         lam[:, gs])
            plan.add(("coef", t, g), "V", 280, (("gates", t, g),), t_coef)

            def t_scan(t=t, g=g):
                v["yl", t, g] = _scan_interleaved(v["a", t, g], v["u", t, g], v["h", t - 1, g])
                v["h", t, g] = v["yl", t, g][last:last + 1, :]
            plan.add(("scan", t, g), "V", 120, (("coef", t, g),) + dep_prev("scan", g), t_scan)

            h = g

            def t_convm(t=t, h=h, gs=gs):
                xm_c, v["mconv", t, h] = _conv_interleaved(v["mconv", t - 1, h], v["xm", t, h],
                                                           mcw[:, gs], mcb[:, gs])
                v["xcb", t, h] = (xm_c * _sigmoid(xm_c)).astype(BF)
            plan.add(("convm", t, h), "V", 210, (("xm", t, h),) + dep_prev("convm", h), t_convm)

            def t_qkv(t=t, h=h):
                v["q", t, h] = _dot(v["xcb", t, h], wq_ref[h]) * (M_HD ** -0.5)
                v["k", t, h] = _dot(v["xcb", t, h], wk_ref[h])
                vv = _dot(v["xm", t, h].astype(BF), wv_ref[h])
                v["qb", t, h], v["kb", t, h], v["vb", t, h] = (
                    v["q", t, h].astype(BF), v["k", t, h].astype(BF), vv.astype(BF))
            plan.add(("qkv", t, h), "M", 200, (("convm", t, h),), t_qkv)

            def t_qk(t=t, h=h):
                v["qk", t, h] = _dot_nt(v["qb", t, h], v["kb", t, h])
            plan.add(("qk", t, h), "M", 64, (("qkv", t, h),), t_qk)

            def t_sp(t=t, h=h):
                gcol, grow = v["gcol", t], v["grow", t]
                b_col = gcol[:, M_HEADS + h:M_HEADS + h + 1]
                ig_row, b_row = grow[h:h + 1, :], grow[M_HEADS + h:M_HEADS + h + 1, :]
                dlog = jnp.where(v["tri", t], b_col - b_row + ig_row, MASKED)
                m_inter = b_col + v["m", t - 1, h]
                m_t = jnp.maximum(m_inter, jnp.max(dlog, axis=1, keepdims=True))
                s = v["qk", t, h] * jnp.exp(dlog - m_t)
                v["m_t", t, h], v["sc", t, h] = m_t, jnp.exp(m_inter - m_t)
                v["ssum", t, h] = jnp.sum(s, axis=1, keepdims=True)
                v["sb", t, h] = s.astype(BF)
                v["m", t, h] = m_t[last:last + 1, :]
            plan.add(("sp", t, h), "V", 200, (("qk", t, h), ("gcum", t)) + dep_prev("sp", h), t_sp)

            def t_sv(t=t, h=h):
                v["sv", t, h] = _dot(v["sb", t, h], v["vb", t, h])
                v["qc", t, h] = _dot(v["qb", t, h], v["c", t - 1, h].astype(BF))
            plan.add(("sv", t, h), "M", 130, (("sp", t, h),) + dep_prev("cnew", h), t_sv)

            def t_hn(t=t, h=h, gs=gs):
                sc, m_t = v["sc", t, h], v["m_t", t, h]
                num = v["sv", t, h] + sc * v["qc", t, h]
                den = v["ssum", t, h] + sc * jnp.sum(v["q", t, h] * v["n", t - 1, h], axis=1,
                                                     keepdims=True)
                hh = num / jnp.maximum(jnp.abs(den), jnp.exp(-m_t))
                v["hn", t, h] = (hh * lax.rsqrt(jnp.mean(hh * hh, axis=-1, keepdims=True) + EPS)
                                 * mg[:, gs])
            plan.add(("hn", t, h), "V", 170, (("sv", t, h),) + dep_prev("kw", h), t_hn)

            def t_kw(t=t, h=h):
                gcol = v["gcol", t]
                ig_col, b_col = gcol[:, h:h + 1], gcol[:, M_HEADS + h:M_HEADS + h + 1]
                m_last = v["m", t, h]
                b_last = b_col[last:last + 1, :]
                kw = v["k", t, h] * jnp.exp(b_last - b_col + ig_col - m_last)
                v["dec", t, h] = jnp.exp(b_last + v["m", t - 1, h] - m_last)
                v["n", t, h] = v["dec", t, h] * v["n", t - 1, h] + jnp.sum(kw, axis=0, keepdims=True)
                v["kwb", t, h] = kw.astype(BF)
            plan.add(("kw", t, h), "V", 80, (("sp", t, h),) + dep_prev("kw", h), t_kw)

            def t_ckv(t=t, h=h):
                v["ckv", t, h] = _dot_tn(v["kwb", t, h], v["vb", t, h])
            plan.add(("ckv", t, h), "M", 64, (("kw", t, h),), t_ckv)

            def t_cnew(t=t, h=h):
                v["c", t, h] = v["dec", t, h] * v["c", t - 1, h] + v["ckv", t, h]
            plan.add(("cnew", t, h), "V", 40, (("ckv", t, h),) + dep_prev("cnew", h), t_cnew)

            def t_sig(t=t, h=h):
                v["gate_l", t, h] = _sigmoid(v["gl", t, h]) * v["yl", t, h]
                v["gate_m", t, h] = _sigmoid(v["gm", t, h]) * _sigmoid(v["om", t, h])
            plan.add(("sig", t, h), "V", 180,
                     (("gl", t, h), ("gm", t, h), ("om", t, h), ("scan", t, h)), t_sig)

            def t_mrg(t=t, h=h):
                v["mrg", t, h] = (v["gate_l", t, h] + v["gate_m", t, h] * v["hn", t, h]).astype(BF)
            plan.add(("mrg", t, h), "V", 40, (("sig", t, h), ("hn", t, h)), t_mrg)

            def t_pt(t=t, h=h):
                v["mrg_t", t, h] = _dot(permt_ref[...], v["mrg", t, h]).astype(BF)
            plan.add(("pt", t, h), "M", 64, (("mrg", t, h),), t_pt)

            def t_wo(t=t, h=h, gs=gs):
                part = _dot(v["mrg_t", t, h], wout_ref[gs, :])
                v["out", t] = part if h == 0 else v["out", t] + part
            plan.add(("wo", t, h), "M", 260, (("pt", t, h),) + ((("wo", t, h - 1),) if h else ()),
                     t_wo)

        def t_fin(t=t):
            x1_ref[0, t * tt:(t + 1) * tt, :] = v["x", t] + v["out", t]
        plan.add(("fin", t), "V", 70, (("wo", t, M_HEADS - 1),), t_fin)

    def t_state():
        e = SUB_TILES - 1
        heads = range(M_HEADS)
        lconv_ref[0] = jnp.concatenate([v["lconv", e, g] for g in heads], axis=1)
        mconv_ref[0] = jnp.concatenate([v["mconv", e, g] for g in heads], axis=1)
        h_ref[0] = jnp.concatenate([v["h", e, g] for g in heads], axis=1)
        for h in heads:
            c_ref[0, h] = v["c", e, h]
        n_ref[0] = jnp.concatenate([v["n", e, h] for h in heads], axis=0)
        m_ref[0] = jnp.concatenate([v["m", e, h] for h in heads], axis=1)
    plan.add("state", "V", 60, tuple(("cnew", SUB_TILES - 1, h) for h in range(M_HEADS))
             + tuple(("scan", SUB_TILES - 1, h) for h in range(M_HEADS))
             + tuple(("fin", t) for t in range(SUB_TILES)), t_state)

    plan.run()


def _split_kernel(wt_ref, gt_ref, wbr_ref, wmg_ref, wif_ref):
    j = pl.program_id(0)
    piece = wt_ref[...].T.astype(BF)

    @pl.when(j < BRANCH_STEPS)
    def _():
        wbr_ref[...] = piece

    @pl.when(j >= BRANCH_STEPS)
    def _():
        wmg_ref[...] = piece

    @pl.when(j == 0)
    def _():
        gates = gt_ref[...].T
        lane = lax.broadcasted_iota(jnp.int32, gates.shape, 1)
        wif_ref[...] = jnp.where(lane < N_GATES, gates, 0.0).astype(BF)


def _ffn_tail(x1, p, gffn_ref, wg_ref, wu_ref, wd_ref, gple_ref, wpg_ref, wple_ref, gfin_ref):
    xnb = _rms(x1, gffn_ref[...]).astype(BF)
    x2 = x1
    for c in range(D_FF // FF_CHUNK):
        cs = slice(c * FF_CHUNK, (c + 1) * FF_CHUNK)
        hg = _dot(xnb, wg_ref[:, cs])
        hu = _dot(xnb, wu_ref[:, cs])
        act = (hg * _sigmoid(hg) * hu).astype(BF)
        x2 = x2 + _dot(act, wd_ref[cs, :])
    gate = _sigmoid(_dot(_rms(x2, gple_ref[...]).astype(BF), wpg_ref[...]))
    x3 = x2 + gate * _dot(p.astype(BF), wple_ref[...])
    return _rms(x3, gfin_ref[...])


def _ffn_kernel(sc_ref, x1_ref, p_ref, q_ref, kw_ref, v_ref, c_ref, gffn_ref, wg_ref, wu_ref,
                wd_ref, gple_ref, wpg_ref, wple_ref, gfin_ref, y_ref, qc_out, c_out):
    tt = FFN_ROWS
    v = {}
    plan = _Plan()

    for t in range(FFN_SUB):
        rows = slice(t * tt, (t + 1) * tt)

        def t_norm(t=t, rows=rows):
            v["x", t] = x1_ref[rows, :]
            v["xnb", t] = _rms(v["x", t], gffn_ref[...]).astype(BF)
        plan.add(("norm", t), "V", 650, (), t_norm)

        for c in range(D_FF // FF_CHUNK):
            cs = slice(c * FF_CHUNK, (c + 1) * FF_CHUNK)

            def t_gu(t=t, c=c, cs=cs):
                v["hg", t, c] = _dot(v["xnb", t], wg_ref[:, cs])
                v["hu", t, c] = _dot(v["xnb", t], wu_ref[:, cs])
            plan.add(("gu", t, c), "M", 512, (("norm", t),), t_gu)

            def t_act(t=t, c=c):
                hg = v["hg", t, c]
                v["act", t, c] = (hg * _sigmoid(hg) * v["hu", t, c]).astype(BF)
            plan.add(("act", t, c), "V", 110, (("gu", t, c),), t_act)

            def t_dn(t=t, c=c, cs=cs):
                prev = v["x", t] if c == 0 else v["x2", t]
                v["x2", t] = prev + _dot(v["act", t, c], wd_ref[cs, :])
            plan.add(("dn", t, c), "M", 300, (("act", t, c),) + ((("dn", t, c - 1),) if c else ()),
                     t_dn)

        last_dn = ("dn", t, D_FF // FF_CHUNK - 1)

        def t_norm2(t=t):
            v["xgb", t] = _rms(v["x2", t], gple_ref[...]).astype(BF)
        plan.add(("norm2", t), "V", 650, (last_dn,), t_norm2)

        def t_ple(t=t, rows=rows):
            v["pe", t] = _dot(p_ref[rows, :].astype(BF), wple_ref[...])
        plan.add(("ple", t), "M", 300, (), t_ple)

        def t_wpg(t=t):
            v["gpre", t] = _dot(v["xgb", t], wpg_ref[...])
        plan.add(("wpg", t), "M", 1024, (("norm2", t),), t_wpg)

        def t_fin(t=t, rows=rows):
            x3 = v["x2", t] + _sigmoid(v["gpre", t]) * v["pe", t]
            y_ref[rows, :] = _rms(x3, gfin_ref[...])
        plan.add(("fin", t), "V", 1000, (("wpg", t), ("ple", t)), t_fin)

    base = pl.program_id(0) * STATE_BATCH
    heads = [slice(h * M_HD, (h + 1) * M_HD) for h in range(M_HEADS)]
    for bb in range(STATE_BATCH):
        def t_cols(bb=bb):
            row = pl.ds(base + bb, 1)
            q, kw = q_ref[row, :], kw_ref[row, :]
            v["cols", bb] = jnp.concatenate([q[:, hs] for hs in heads] + [kw[:, hs] for hs in heads],
                                            axis=0).T
            v["vrow", bb] = v_ref[row, :]
        plan.add(("cols", bb), "V", 40, (), t_cols)

        for h, hs in enumerate(heads):
            def t_pair(bb=bb, h=h, hs=hs):
                cols = v["cols", bb]
                q_col, kw_col = cols[:, h:h + 1], cols[:, M_HEADS + h:M_HEADS + h + 1]
                c_prev = c_ref[bb, h]
                qc_out[pl.ds(base + bb, 1), hs] = jnp.sum(q_col * c_prev, axis=0, keepdims=True)
                c_out[bb, h] = sc_ref[base + bb, h] * c_prev + kw_col * v["vrow", bb][:, hs]
            plan.add(("pair", bb, h), "V", 170, (("cols", bb),), t_pair)

    plan.run()


def _front_kernel(x_ref, lbuf_ref, h0_ref, mbuf_ref, gmix_ref, wbr_ref, wmg_ref, wif_ref, bif_ref,
                  lcw_ref, lcb_ref, wax_ref, ba_ref, bx_ref, lam_ref, mcw_ref, mcb_ref,
                  wq_ref, wk_ref, wv_ref, m0_ref, n0_ref,
                  q_ref, kw_ref, v_ref, sc_ref, hv_ref, hc_ref, n_out, m_out,
                  yl_ref, gm_ref, lbuf_out, h_out, mbuf_out):
    xnb = _rms(x_ref[...], gmix_ref[...]).astype(BF)

    def per_head(cols):
        return jnp.concatenate([jnp.broadcast_to(cols[:, h:h + 1], (cols.shape[0], M_HD))
                                for h in range(M_HEADS)], axis=1)

    def head_sums(a):
        return jnp.concatenate([jnp.sum(a[:, h * M_HD:(h + 1) * M_HD], axis=1, keepdims=True)
                                for h in range(M_HEADS)], axis=1)

    def conv_step(buf_ref, buf_out, x_new, w_ref, b_ref):
        out = b_ref[...] + x_new * w_ref[CONV_W - 1:CONV_W, :]
        for j in range(CONV_W - 1):
            out = out + buf_ref[j] * w_ref[j:j + 1, :]
        for j in range(CONV_W - 2):
            buf_out[j] = buf_ref[j + 1]
        buf_out[CONV_W - 2] = x_new
        return out

    x_l = _dot(xnb, wbr_ref[:, 0:D_MODEL])
    xl_c = conv_step(lbuf_ref, lbuf_out, x_l, lcw_ref, lcb_ref)
    ga, gx = _lru_gates(xl_c.astype(BF), wax_ref, ba_ref[...], bx_ref[...])
    a, u = _lru_coeffs(xl_c, ga, gx, lam_ref[...])
    y_l = a * h0_ref[...] + u
    h_out[...] = y_l
    g_l = _dot(xnb, wmg_ref[:, 0:D_MODEL])
    yl_ref[...] = _sigmoid(g_l) * y_l

    x_m = _dot(xnb, wbr_ref[:, D_MODEL:2 * D_MODEL])
    xm_c = conv_step(mbuf_ref, mbuf_out, x_m, mcw_ref, mcb_ref)
    xcb = (xm_c * _sigmoid(xm_c)).astype(BF)
    xmb = x_m.astype(BF)
    q, k, v = [], [], []
    for h in range(M_HEADS):
        hs = slice(h * M_HD, (h + 1) * M_HD)
        q.append(_dot(xcb[:, hs], wq_ref[h]) * (M_HD ** -0.5))
        k.append(_dot(xcb[:, hs], wk_ref[h]))
        v.append(_dot(xmb[:, hs], wv_ref[h]))
    q, k, v = (jnp.concatenate(a, axis=1) for a in (q, k, v))
    o_m = _dot(xnb, wbr_ref[:, 2 * D_MODEL:3 * D_MODEL])
    g_m = _dot(xnb, wmg_ref[:, D_MODEL:2 * D_MODEL])
    gm_ref[...] = _sigmoid(g_m) * _sigmoid(o_m)

    pre = _dot(xnb, wif_ref[...]) + bif_ref[...]
    ig = pre[:, 0:M_HEADS]
    lf = -_softplus(-pre[:, M_HEADS:N_GATES])
    n_prev = n0_ref[...]
    m_inter = lf + m0_ref[...]
    m_t = jnp.maximum(m_inter, ig)
    wk = jnp.exp(ig - m_t)
    sc = jnp.exp(m_inter - m_t)
    s = head_sums(q * k) * wk
    den = s + sc * head_sums(q * n_prev)
    rden = 1.0 / jnp.maximum(jnp.abs(den), jnp.exp(-m_t))
    wk_d, sc_d = per_head(wk), per_head(sc)
    hv_ref[...] = per_head(s * rden) * v
    hc_ref[...] = per_head(sc * rden)
    q_ref[...] = q
    kw_ref[...] = wk_d * k
    v_ref[...] = v
    sc_ref[...] = sc
    n_out[...] = sc_d * n_prev + wk_d * k
    m_out[...] = m_t


def _back_kernel(x_ref, qc_ref, hv_ref, hc_ref, yl_ref, gm_ref, p_ref, mg_ref, wout_ref, gffn_ref,
                 wg_ref, wu_ref, wd_ref, gple_ref, wpg_ref, wple_ref, gfin_ref, y_ref, mrg_ref):
    for h in range(M_HEADS):
        hs = slice(h * M_HD, (h + 1) * M_HD)
        hh = hv_ref[:, hs] + hc_ref[:, hs] * qc_ref[:, hs]
        hn = hh * lax.rsqrt(jnp.mean(hh * hh, axis=-1, keepdims=True) + EPS) * mg_ref[:, hs]
        mrg_ref[:, hs] = (yl_ref[:, hs] + gm_ref[:, hs] * hn).astype(BF)
    x1 = x_ref[...] + _dot(mrg_ref[...], wout_ref[...])
    y_ref[...] = _ffn_tail(x1, p_ref[...], gffn_ref, wg_ref, wu_ref, wd_ref, gple_ref, wpg_ref,
                           wple_ref, gfin_ref)


def _resident(shape):
    nd = len(shape)
    return pl.BlockSpec(shape, lambda *_: (0,) * nd, pipeline_mode=pl.Buffered(1))


def _params(n_axes):
    return pltpu.CompilerParams(dimension_semantics=("arbitrary",) * n_axes,
                                vmem_limit_bytes=VMEM_LIMIT)


def _interleave_matrix():
    r = np.arange(SEQ_TILE)
    perm = np.zeros((SEQ_TILE, SEQ_TILE), np.float32)
    perm[r, (r % SUBLANES) * SEG + r // SUBLANES] = 1.0
    return perm


def kernel(x_prompt, x_sample, state_lru_conv, state_lru_h, state_mlstm_conv, state_mlstm_C, state_mlstm_n, state_mlstm_m, p_prompt, p_sample, norm_mix_g, w_in, b_gates, lru_conv_w, lru_conv_b, lru_w_a, lru_b_a, lru_w_x, lru_b_x, lru_lambda, mlstm_conv_w, mlstm_conv_b, w_q, w_k, w_v, mlstm_norm_g, w_out, norm_ffn_g, w_ffn_gate, w_ffn_up, w_ffn_down, norm_ple_g, w_ple_gate, w_ple, final_norm_g):
    assert w_in.shape[0] == 1, "single-layer trunk"
    B, T, _ = x_prompt.shape
    S = x_sample.shape[0]
    step = SEQ_TILE * SUB_TILES
    assert T % step == 0 and (B * T) % FFN_TILE == 0 and x_sample.shape[1] == 1
    assert S == STATE_BATCH * ((B * T) // FFN_TILE), "one state batch per ffn grid step"

    sds = jax.ShapeDtypeStruct
    assert w_in.shape[1:] == (D_MODEL, 5 * D_MODEL + N_GATES)
    wt = jnp.swapaxes(w_in[0], 0, 1)
    g0 = 3 * D_MODEL
    src_row = lambda j: pl.multiple_of(
        jnp.where(j < BRANCH_STEPS, j * PREP_COLS, g0 + N_GATES + (j - BRANCH_STEPS) * PREP_COLS),
        SUBLANES)
    slab = lambda rows, index: pl.BlockSpec((pl.Element(rows), pl.Element(D_MODEL)), index)
    cols = lambda index: pl.BlockSpec((D_MODEL, PREP_COLS), index)
    wbr, wmg, wif = pl.pallas_call(
        _split_kernel,
        grid=(5 * D_MODEL // PREP_COLS,),
        in_specs=[slab(PREP_COLS, lambda j: (src_row(j), 0)), slab(GATE_PAD, lambda j: (g0, 0))],
        out_specs=[cols(lambda j: (0, jnp.minimum(j, BRANCH_STEPS - 1))),
                   cols(lambda j: (0, jnp.maximum(j - BRANCH_STEPS, 0))),
                   pl.BlockSpec((D_MODEL, GATE_PAD), lambda j: (0, 0))],
        out_shape=[sds((D_MODEL, 3 * D_MODEL), BF), sds((D_MODEL, 2 * D_MODEL), BF),
                   sds((D_MODEL, GATE_PAD), BF)],
        compiler_params=_params(1),
        name="split_w_in",
    )(wt, wt)
    bif = jnp.pad(b_gates[0], (0, GATE_PAD - N_GATES)).reshape(1, GATE_PAD)
    wax = jnp.concatenate([lru_w_a[0], lru_w_x[0]], axis=2).astype(BF)
    row = lambda a: a.reshape(1, -1)
    gmix, ba, bx, lam = row(norm_mix_g[0]), row(lru_b_a[0]), row(lru_b_x[0]), row(lru_lambda[0])
    lcw, lcb = lru_conv_w[0], row(lru_conv_b[0])
    mcw, mcb = mlstm_conv_w[0], row(mlstm_conv_b[0])
    wq, wk, wv = w_q[0].astype(BF), w_k[0].astype(BF), w_v[0].astype(BF)
    mg = row(mlstm_norm_g[0])
    wout = w_out[0].astype(BF)
    gffn, gple, gfin = row(norm_ffn_g[0]), row(norm_ple_g[0]), row(final_norm_g)
    wg, wu, wd = w_ffn_gate[0].astype(BF), w_ffn_up[0].astype(BF), w_ffn_down[0].astype(BF)
    wpg, wple = w_ple_gate[0].astype(BF), w_ple[0].astype(BF)
    perm_np = _interleave_matrix()
    perm, perm_t = jnp.asarray(perm_np, BF), jnp.asarray(perm_np.T, BF)

    front_w = (gmix, wbr, wmg, wif, bif, lcw, lcb, wax, ba, bx, lam, mcw, mcb, wq, wk, wv)
    mixer_w = (perm, perm_t) + front_w + (mg, wout)
    ffn_w = (gffn, wg, wu, wd, gple, wpg, wple, gfin)

    nt = T // step
    x1, p_lconv, p_h, p_mconv, p_c, p_n, p_m = pl.pallas_call(
        _mixer_kernel,
        grid=(B, nt),
        in_specs=[pl.BlockSpec((1, step, D_MODEL), lambda b, t: (b, t, 0))]
                 + [_resident(w.shape) for w in mixer_w],
        out_specs=[
            pl.BlockSpec((1, step, D_MODEL), lambda b, t: (b, t, 0)),
            pl.BlockSpec((1, CONV_W - 1, D_MODEL), lambda b, t: (b, 0, 0)),
            pl.BlockSpec((1, 1, D_MODEL), lambda b, t: (b, 0, 0)),
            pl.BlockSpec((1, CONV_W - 1, D_MODEL), lambda b, t: (b, 0, 0)),
            pl.BlockSpec((1, M_HEADS, M_HD, M_HD), lambda b, t: (b, 0, 0, 0)),
            pl.BlockSpec((1, M_HEADS, M_HD), lambda b, t: (b, 0, 0)),
            pl.BlockSpec((1, 1, M_HEADS), lambda b, t: (b, 0, 0)),
        ],
        out_shape=[
            sds((B, T, D_MODEL), F32),
            sds((B, CONV_W - 1, D_MODEL), F32),
            sds((B, 1, D_MODEL), F32),
            sds((B, CONV_W - 1, D_MODEL), F32),
            sds((B, M_HEADS, M_HD, M_HD), F32),
            sds((B, M_HEADS, M_HD), F32),
            sds((B, 1, M_HEADS), F32),
        ],
        compiler_params=_params(2),
        name="prompt_mixer",
    )(x_prompt, *mixer_w)

    xs = x_sample.reshape(S, D_MODEL)
    lbuf = jnp.swapaxes(state_lru_conv[0], 0, 1)
    mbuf = jnp.swapaxes(state_mlstm_conv[0], 0, 1)
    tok = sds((S, D_MODEL), F32)
    buf = sds((CONV_W - 1, S, D_MODEL), F32)
    per_head = sds((S, M_HEADS), F32)
    q, kw, v, sc, hv, hc, s_n, s_m, yl, gm, s_lbuf, s_h, s_mbuf = pl.pallas_call(
        _front_kernel,
        out_shape=[tok, tok, tok, per_head, tok, tok, tok, per_head, tok, tok, buf, tok, buf],
        compiler_params=pltpu.CompilerParams(vmem_limit_bytes=VMEM_LIMIT),
        name="sample_front",
    )(xs, lbuf, state_lru_h[0], mbuf, *front_w, state_mlstm_m[0],
      state_mlstm_n[0].reshape(S, D_MODEL))

    n_tok = B * T
    tile = lambda width: pl.BlockSpec((FFN_TILE, width), lambda i: (i, 0))
    c_spec = pl.BlockSpec((STATE_BATCH, M_HEADS, M_HD, M_HD), lambda i: (i, 0, 0, 0))
    y_prompt, qc, s_c = pl.pallas_call(
        _ffn_kernel,
        grid=(n_tok // FFN_TILE,),
        in_specs=[pl.BlockSpec(memory_space=pltpu.SMEM), tile(D_MODEL), tile(P_DIM),
                  _resident(tok.shape), _resident(tok.shape), _resident(tok.shape), c_spec]
                 + [_resident(w.shape) for w in ffn_w],
        out_specs=[tile(D_MODEL), pl.BlockSpec(tok.shape, lambda i: (0, 0)), c_spec],
        out_shape=[sds((n_tok, D_MODEL), F32), tok, sds((S, M_HEADS, M_HD, M_HD), F32)],
        compiler_params=_params(1),
        name="prompt_ffn",
    )(sc, x1.reshape(n_tok, D_MODEL), p_prompt[0].reshape(n_tok, P_DIM), q, kw, v,
      state_mlstm_C[0], *ffn_w)

    y_sample = pl.pallas_call(
        _back_kernel,
        out_shape=tok,
        scratch_shapes=[pltpu.VMEM((S, D_MODEL), BF)],
        compiler_params=pltpu.CompilerParams(vmem_limit_bytes=VMEM_LIMIT),
        name="sample_back",
    )(xs, qc, hv, hc, yl, gm, p_sample[0].reshape(S, P_DIM), mg, wout, *ffn_w)

    lead = lambda a: a[None]
    return (y_prompt.reshape(B, T, D_MODEL), y_sample.reshape(S, 1, D_MODEL),
            lead(p_lconv), p_h.reshape(1, B, D_MODEL), lead(p_mconv), lead(p_c), lead(p_n),
            p_m.reshape(1, B, M_HEADS),
            lead(jnp.swapaxes(s_lbuf, 0, 1)), lead(s_h), lead(jnp.swapaxes(s_mbuf, 0, 1)),
            lead(s_c), s_n.reshape(1, S, M_HEADS, M_HD), lead(s_m))
```

```python
import numpy as np

import jax
import jax.numpy as jnp
from jax import lax
from jax.experimental import pallas as pl
from jax.experimental.pallas import tpu as pltpu

D_MODEL = 1024
M_HEADS = 4
M_HD = D_MODEL // M_HEADS
LRU_BLOCKS = 8
LRU_BLK = D_MODEL // LRU_BLOCKS
LRU_C = 8.0
CONV_W = 4
D_FF = 2816
P_DIM = 256
EPS = 1e-6
M_INIT = -1e30
NEG_LOG2E = -1.4426950408889634
MASKED = -1e30
N_GATES = 2 * M_HEADS
GATE_PAD = 128
SUBLANES = 8
SEQ_TILE = 256
SEG = SEQ_TILE // SUBLANES
SUB_TILES = 2
FFN_TILE = 512
FFN_ROWS = 256
FFN_SUB = FFN_TILE // FFN_ROWS
FF_CHUNK = 256
STATE_BATCH = 4
PREP_COLS = 512
BRANCH_STEPS = 3 * D_MODEL // PREP_COLS
VMEM_LIMIT = 56 * 1024 * 1024

BF = jnp.bfloat16
F32 = jnp.float32


def _dot(a, b):
    return jnp.dot(a, b, preferred_element_type=F32)


def _dot_nt(a, b):
    return lax.dot_general(a, b, (((1,), (1,)), ((), ())), preferred_element_type=F32)


def _dot_tn(a, b):
    return lax.dot_general(a, b, (((0,), (0,)), ((), ())), preferred_element_type=F32)


def _sigmoid(x):
    return 1.0 / (1.0 + jnp.exp2(x * NEG_LOG2E))


def _softplus(x):
    return jnp.maximum(x, 0.0) + jnp.log1p(jnp.exp(-jnp.abs(x)))


def _rms(x, g):
    return x * lax.rsqrt(jnp.mean(x * x, axis=-1, keepdims=True) + EPS) * g


def _group(x, i):
    return x[i * SUBLANES:(i + 1) * SUBLANES, :]


def _lru_coeffs(xc, ga, gx, lam):
    r = _sigmoid(ga)
    ig = _sigmoid(gx)
    log_a = -LRU_C * r * _softplus(-lam)
    a = jnp.exp(log_a)
    om = 1.0 - a * a
    root = jnp.where(om > 0.0, om * lax.rsqrt(om), 0.0)
    return a, root * ig * xc


def _lru_gates(xcb, wax_ref, ba, bx):
    ga, gx = [], []
    for n in range(LRU_BLOCKS):
        g = _dot(xcb[:, n * LRU_BLK:(n + 1) * LRU_BLK], wax_ref[n])
        ga.append(g[:, :LRU_BLK])
        gx.append(g[:, LRU_BLK:])
    return jnp.concatenate(ga, axis=1) + ba, jnp.concatenate(gx, axis=1) + bx


def _scan_interleaved(a, u, h0):
    prods, sums = [], []
    p = s = None
    for i in range(SEG):
        ai, ui = _group(a, i), _group(u, i)
        p, s = (ai, ui) if i == 0 else (ai * p, ai * s + ui)
        prods.append(p)
        sums.append(s)
    c = h0
    starts = [c]
    for j in range(SUBLANES - 1):
        c = p[j:j + 1, :] * c + s[j:j + 1, :]
        starts.append(c)
    start = jnp.concatenate(starts, axis=0)
    return jnp.concatenate([sums[i] + prods[i] * start for i in range(SEG)], axis=0)


def _cumsum_interleaved(x):
    acc = []
    run = None
    for i in range(SEG):
        run = _group(x, i) if i == 0 else run + _group(x, i)
        acc.append(run)
    sub = lax.broadcasted_iota(jnp.int32, run.shape, 0)
    inc = run
    s = 1
    while s < SUBLANES:
        inc = inc + jnp.where(sub >= s, pltpu.roll(inc, s, 0), 0.0)
        s *= 2
    before = inc - run
    return jnp.concatenate([r + before for r in acc], axis=0)


def _conv_interleaved(carry, x_new, w, b):
    sub = lax.broadcasted_iota(jnp.int32, (SUBLANES, x_new.shape[1]), 0)
    head, new = [], []
    for r in range(CONV_W - 1):
        cur = _group(x_new, SEG - (CONV_W - 1) + r)
        head.append(jnp.where(sub == 0, carry[r:r + 1, :], pltpu.roll(cur, 1, 0)))
        new.append(cur[SUBLANES - 1:SUBLANES, :])
    ext = jnp.concatenate(head + [x_new], axis=0)
    out = b
    for j in range(CONV_W):
        out = out + ext[j * SUBLANES:j * SUBLANES + SEQ_TILE, :] * w[j:j + 1, :]
    return out, jnp.concatenate(new, axis=0)


def _time_of_row(r):
    return (r & (SUBLANES - 1)) * SEG + (r >> (SUBLANES.bit_length() - 1))


RESULT_DELAY = {"M": 250, "V": 40}
COST_SCALE = {"M": 1.0, "V": 0.75}


class _Plan:
    def __init__(self):
        self.tasks = {}

    def add(self, name, unit, cost, deps, fn):
        self.tasks[name] = (unit, cost * COST_SCALE[unit], tuple(deps), fn)

    def order(self):
        succ = {n: [] for n in self.tasks}
        for n, (_, _, deps, _) in self.tasks.items():
            for p in deps:
                succ[p].append(n)
        tail = {}

        def path(n):
            if n not in tail:
                tail[n] = self.tasks[n][1] + max([path(s) for s in succ[n]], default=0)
            return tail[n]

        free = {"M": 0, "V": 0}
        done, order, left = {}, [], list(self.tasks)
        while left:
            ready = [n for n in left if all(p in done for p in self.tasks[n][2])]

            def start(n):
                unit, _, deps, _ = self.tasks[n]
                return max([free[unit]] + [done[p] for p in deps])

            n = min(ready, key=lambda n: (start(n), -path(n)))
            unit, cost, _, _ = self.tasks[n]
            st = start(n)
            free[unit] = st + cost
            done[n] = st + cost + RESULT_DELAY[unit]
            order.append((st, len(order), n))
            left.remove(n)
        return [n for _, _, n in sorted(order)]

    def run(self):
        for n in self.order():
            self.tasks[n][3]()


def _mixer_kernel(x_ref, perm_ref, permt_ref, gmix_ref, wbr_ref, wmg_ref, wif_ref, bif_ref, lcw_ref,
                  lcb_ref, wax_ref, ba_ref, bx_ref, lam_ref, mcw_ref, mcb_ref, wq_ref, wk_ref,
                  wv_ref, mg_ref, wout_ref,
                  x1_ref, lconv_ref, h_ref, mconv_ref, c_ref, n_ref, m_ref):
    tt = SEQ_TILE
    last = tt - 1

    @pl.when(pl.program_id(1) == 0)
    def _():
        lconv_ref[...] = jnp.zeros(lconv_ref.shape, F32)
        mconv_ref[...] = jnp.zeros(mconv_ref.shape, F32)
        h_ref[...] = jnp.zeros(h_ref.shape, F32)
        c_ref[...] = jnp.zeros(c_ref.shape, F32)
        n_ref[...] = jnp.zeros(n_ref.shape, F32)
        m_ref[...] = jnp.full(m_ref.shape, M_INIT, F32)

    lcw, lcb, mcw, mcb = lcw_ref[...], lcb_ref[...], mcw_ref[...], mcb_ref[...]
    ba, bx, lam, mg = ba_ref[...], bx_ref[...], lam_ref[...], mg_ref[...]

    v = {}
    for g in range(M_HEADS):
        gs = slice(g * M_HD, (g + 1) * M_HD)
        v["lconv", -1, g], v["mconv", -1, g] = lconv_ref[0, :, gs], mconv_ref[0, :, gs]
        v["h", -1, g] = h_ref[0, :, gs]
        v["c", -1, g], v["n", -1, g] = c_ref[0, g], n_ref[0, g:g + 1, :]
        v["m", -1, g] = m_ref[0, :, g:g + 1]
    plan = _Plan()

    for t in range(SUB_TILES):
        dep_prev = (lambda name, g, t=t: ((name, t - 1, g),)) if t else (lambda name, g: ())

        def t_norm(t=t):
            v["x", t] = x_ref[0, t * tt:(t + 1) * tt, :]
            v["xn_t", t] = _rms(v["x", t], gmix_ref[...]).astype(BF)
        plan.add(("norm", t), "V", 650, (), t_norm)

        def t_perm(t=t):
            v["xnb", t] = _dot(perm_ref[...], v["xn_t", t]).astype(BF)
        plan.add(("perm", t), "M", 260, (("norm", t),), t_perm)

        def proj(name, w_ref, col0, g, t=t):
            def run():
                v[name, t, g] = _dot(v["xnb", t], w_ref[:, col0 + g * M_HD:col0 + (g + 1) * M_HD])
            plan.add((name, t, g), "M", 260, (("perm", t),), run)

        def t_wif(t=t):
            v["pre", t] = _dot(v["xnb", t], wif_ref[...]) + bif_ref[...]
        plan.add(("wif", t), "M", 260, (("perm", t),), t_wif)

        def t_gcum(t=t):
            pre = v["pre", t]
            lane = lax.broadcasted_iota(jnp.int32, (tt, GATE_PAD), 1)
            v["gcol", t] = jnp.where(lane < M_HEADS, pre, _cumsum_interleaved(-_softplus(-pre)))
            v["grow", t] = v["gcol", t].T
            v["tri", t] = (_time_of_row(lax.broadcasted_iota(jnp.int32, (tt, 1), 0))
                           >= _time_of_row(lax.broadcasted_iota(jnp.int32, (1, tt), 1)))
        plan.add(("gcum", t), "V", 150, (("wif", t),), t_gcum)

        for g in range(M_HEADS):
            gs = slice(g * M_HD, (g + 1) * M_HD)
            for name, w_ref, col0 in (("xl", wbr_ref, 0), ("xm", wbr_ref, D_MODEL),
                                      ("om", wbr_ref, 2 * D_MODEL), ("gl", wmg_ref, 0),
                                      ("gm", wmg_ref, D_MODEL)):
                proj(name, w_ref, col0, g)

            def t_convl(t=t, g=g, gs=gs):
                v["xlc", t, g], v["lconv", t, g] = _conv_interleaved(
                    v["lconv", t - 1, g], v["xl", t, g], lcw[:, gs], lcb[:, gs])
            plan.add(("convl", t, g), "V", 135, (("xl", t, g),) + dep_prev("convl", g), t_convl)

            def t_gates(t=t, g=g, gs=gs):
                xlb = v["xlc", t, g].astype(BF)
                ga, gx = [], []
                for n in range(2):
                    gg = _dot(xlb[:, n * LRU_BLK:(n + 1) * LRU_BLK], wax_ref[2 * g + n])
                    ga.append(gg[:, :LRU_BLK])
                    gx.append(gg[:, LRU_BLK:])
                v["ga", t, g] = jnp.concatenate(ga, axis=1) + ba[:, gs]
                v["gx", t, g] = jnp.concatenate(gx, axis=1) + bx[:, gs]
            plan.add(("gates", t, g), "M", 130, (("convl", t, g),), t_gates)

            def t_coef(t=t, g=g, gs=gs):
                v["a", t, g], v["u", t, g] = _lru_coeffs(v["xlc", t, g], v["ga", t, g], v["gx", t, g],
                                                         lam[:, gs])
            plan.add(("coef", t, g), "V", 280, (("gates", t, g),), t_coef)

            def t_scan(t=t, g=g):
                v["yl", t, g] = _scan_interleaved(v["a", t, g], v["u", t, g], v["h", t - 1, g])
                v["h", t, g] = v["yl", t, g][last:last + 1, :]
            plan.add(("scan", t, g), "V", 120, (("coef", t, g),) + dep_prev("scan", g), t_scan)

            h = g

            def t_convm(t=t, h=h, gs=gs):
                xm_c, v["mconv", t, h] = _conv_interleaved(v["mconv", t - 1, h], v["xm", t, h],
                                                           mcw[:, gs], mcb[:, gs])
                v["xcb", t, h] = (xm_c * _sigmoid(xm_c)).astype(BF)
            plan.add(("convm", t, h), "V", 210, (("xm", t, h),) + dep_prev("convm", h), t_convm)

            def t_qkv(t=t, h=h):
                v["q", t, h] = _dot(v["xcb", t, h], wq_ref[h]) * (M_HD ** -0.5)
                v["k", t, h] = _dot(v["xcb", t, h], wk_ref[h])
                vv = _dot(v["xm", t, h].astype(BF), wv_ref[h])
                v["qb", t, h], v["kb", t, h], v["vb", t, h] = (
                    v["q", t, h].astype(BF), v["k", t, h].astype(BF), vv.astype(BF))
            plan.add(("qkv", t, h), "M", 200, (("convm", t, h),), t_qkv)

            def t_qk(t=t, h=h):
                v["qk", t, h] = _dot_nt(v["qb", t, h], v["kb", t, h])
            plan.add(("qk", t, h), "M", 64, (("qkv", t, h),), t_qk)

            def t_sp(t=t, h=h):
                gcol, grow = v["gcol", t], v["grow", t]
                b_col = gcol[:, M_HEADS + h:M_HEADS + h + 1]
                ig_row, b_row = grow[h:h + 1, :], grow[M_HEADS + h:M_HEADS + h + 1, :]
                dlog = jnp.where(v["tri", t], b_col - b_row + ig_row, MASKED)
                m_inter = b_col + v["m", t - 1, h]
                m_t = jnp.maximum(m_inter, jnp.max(dlog, axis=1, keepdims=True))
                s = v["qk", t, h] * jnp.exp(dlog - m_t)
                v["m_t", t, h], v["sc", t, h] = m_t, jnp.exp(m_inter - m_t)
                v["ssum", t, h] = jnp.sum(s, axis=1, keepdims=True)
                v["sb", t, h] = s.astype(BF)
                v["m", t, h] = m_t[last:last + 1, :]
            plan.add(("sp", t, h), "V", 200, (("qk", t, h), ("gcum", t)) + dep_prev("sp", h), t_sp)

            def t_sv(t=t, h=h):
                v["sv", t, h] = _dot(v["sb", t, h], v["vb", t, h])
                v["qc", t, h] = _dot(v["qb", t, h], v["c", t - 1, h].astype(BF))
            plan.add(("sv", t, h), "M", 130, (("sp", t, h),) + dep_prev("cnew", h), t_sv)

            def t_hn(t=t, h=h, gs=gs):
                sc, m_t = v["sc", t, h], v["m_t", t, h]
                num = v["sv", t, h] + sc * v["qc", t, h]
                den = v["ssum", t, h] + sc * jnp.sum(v["q", t, h] * v["n", t - 1, h], axis=1,
                                                     keepdims=True)
                hh = num / jnp.maximum(jnp.abs(den), jnp.exp(-m_t))
                v["hn", t, h] = (hh * lax.rsqrt(jnp.mean(hh * hh, axis=-1, keepdims=True) + EPS)
                                 * mg[:, gs])
            plan.add(("hn", t, h), "V", 170, (("sv", t, h),) + dep_prev("kw", h), t_hn)

            def t_kw(t=t, h=h):
                gcol = v["gcol", t]
                ig_col, b_col = gcol[:, h:h + 1], gcol[:, M_HEADS + h:M_HEADS + h + 1]
                m_last = v["m", t, h]
                b_last = b_col[last:last + 1, :]
                kw = v["k", t, h] * jnp.exp(b_last - b_col + ig_col - m_last)
                v["dec", t, h] = jnp.exp(b_last + v["m", t - 1, h] - m_last)
                v["n", t, h] = v["dec", t, h] * v["n", t - 1, h] + jnp.sum(kw, axis=0, keepdims=True)
                v["kwb", t, h] = kw.astype(BF)
            plan.add(("kw", t, h), "V", 80, (("sp", t, h),) + dep_prev("kw", h), t_kw)

            def t_ckv(t=t, h=h):
                v["ckv", t, h] = _dot_tn(v["kwb", t, h], v["vb", t, h])
            plan.add(("ckv", t, h), "M", 64, (("kw", t, h),), t_ckv)

            def t_cnew(t=t, h=h):
                v["c", t, h] = v["dec", t, h] * v["c", t - 1, h] + v["ckv", t, h]
            plan.add(("cnew", t, h), "V", 40, (("ckv", t, h),) + dep_prev("cnew", h), t_cnew)

            def t_sig(t=t, h=h):
                v["gate_l", t, h] = _sigmoid(v["gl", t, h]) * v["yl", t, h]
                v["gate_m", t, h] = _sigmoid(v["gm", t, h]) * _sigmoid(v["om", t, h])
            plan.add(("sig", t, h), "V", 180,
                     (("gl", t, h), ("gm", t, h), ("om", t, h), ("scan", t, h)), t_sig)

            def t_mrg(t=t, h=h):
                v["mrg", t, h] = (v["gate_l", t, h] + v["gate_m", t, h] * v["hn", t, h]).astype(BF)
            plan.add(("mrg", t, h), "V", 40, (("sig", t, h), ("hn", t, h)), t_mrg)

            def t_pt(t=t, h=h):
                v["mrg_t", t, h] = _dot(permt_ref[...], v["mrg", t, h]).astype(BF)
            plan.add(("pt", t, h), "M", 64, (("mrg", t, h),), t_pt)

            def t_wo(t=t, h=h, gs=gs):
                part = _dot(v["mrg_t", t, h], wout_ref[gs, :])
                v["out", t] = part if h == 0 else v["out", t] + part
            plan.add(("wo", t, h), "M", 260, (("pt", t, h),) + ((("wo", t, h - 1),) if h else ()),
                     t_wo)

        def t_fin(t=t):
            x1_ref[0, t * tt:(t + 1) * tt, :] = v["x", t] + v["out", t]
        plan.add(("fin", t), "V", 70, (("wo", t, M_HEADS - 1),), t_fin)

    def t_state():
        e = SUB_TILES - 1
        heads = range(M_HEADS)
        lconv_ref[0] = jnp.concatenate([v["lconv", e, g] for g in heads], axis=1)
        mconv_ref[0] = jnp.concatenate([v["mconv", e, g] for g in heads], axis=1)
        h_ref[0] = jnp.concatenate([v["h", e, g] for g in heads], axis=1)
        for h in heads:
            c_ref[0, h] = v["c", e, h]
        n_ref[0] = jnp.concatenate([v["n", e, h] for h in heads], axis=0)
        m_ref[0] = jnp.concatenate([v["m", e, h] for h in heads], axis=1)
    plan.add("state", "V", 60, tuple(("cnew", SUB_TILES - 1, h) for h in range(M_HEADS))
             + tuple(("scan", SUB_TILES - 1, h) for h in range(M_HEADS))
             + tuple(("fin", t) for t in range(SUB_TILES)), t_state)

    plan.run()


def _split_kernel(wt_ref, gt_ref, wbr_ref, wmg_ref, wif_ref):
    j = pl.program_id(0)
    piece = wt_ref[...].T.astype(BF)

    @pl.when(j < BRANCH_STEPS)
    def _():
        wbr_ref[...] = piece

    @pl.when(j >= BRANCH_STEPS)
    def _():
        wmg_ref[...] = piece

    @pl.when(j == 0)
    def _():
        gates = gt_ref[...].T
        lane = lax.broadcasted_iota(jnp.int32, gates.shape, 1)
        wif_ref[...] = jnp.where(lane < N_GATES, gates, 0.0).astype(BF)


def _ffn_tail(x1, p, gffn_ref, wg_ref, wu_ref, wd_ref, gple_ref, wpg_ref, wple_ref, gfin_ref):
    xnb = _rms(x1, gffn_ref[...]).astype(BF)
    x2 = x1
    for c in range(D_FF // FF_CHUNK):
        cs = slice(c * FF_CHUNK, (c + 1) * FF_CHUNK)
        hg = _dot(xnb, wg_ref[:, cs])
        hu = _dot(xnb, wu_ref[:, cs])
        act = (hg * _sigmoid(hg) * hu).astype(BF)
        x2 = x2 + _dot(act, wd_ref[cs, :])
    gate = _sigmoid(_dot(_rms(x2, gple_ref[...]).astype(BF), wpg_ref[...]))
    x3 = x2 + gate * _dot(p.astype(BF), wple_ref[...])
    return _rms(x3, gfin_ref[...])


def _ffn_kernel(sc_ref, x1_ref, p_ref, q_ref, kw_ref, v_ref, c_ref, gffn_ref, wg_ref, wu_ref,
                wd_ref, gple_ref, wpg_ref, wple_ref, gfin_ref, y_ref, qc_out, c_out):
    tt = FFN_ROWS
    v = {}
    plan = _Plan()

    for t in range(FFN_SUB):
        rows = slice(t * tt, (t + 1) * tt)

        def t_norm(t=t, rows=rows):
            v["x", t] = x1_ref[rows, :]
            v["xnb", t] = _rms(v["x", t], gffn_ref[...]).astype(BF)
        plan.add(("norm", t), "V", 650, (), t_norm)

        for c in range(D_FF // FF_CHUNK):
            cs = slice(c * FF_CHUNK, (c + 1) * FF_CHUNK)

            def t_gu(t=t, c=c, cs=cs):
                v["hg", t, c] = _dot(v["xnb", t], wg_ref[:, cs])
                v["hu", t, c] = _dot(v["xnb", t], wu_ref[:, cs])
            plan.add(("gu", t, c), "M", 512, (("norm", t),), t_gu)

            def t_act(t=t, c=c):
                hg = v["hg", t, c]
                v["act", t, c] = (hg * _sigmoid(hg) * v["hu", t, c]).astype(BF)
            plan.add(("act", t, c), "V", 110, (("gu", t, c),), t_act)

            def t_dn(t=t, c=c, cs=cs):
                prev = v["x", t] if c == 0 else v["x2", t]
                v["x2", t] = prev + _dot(v["act", t, c], wd_ref[cs, :])
            plan.add(("dn", t, c), "M", 300, (("act", t, c),) + ((("dn", t, c - 1),) if c else ()),
                     t_dn)

        last_dn = ("dn", t, D_FF // FF_CHUNK - 1)

        def t_norm2(t=t):
            v["xgb", t] = _rms(v["x2", t], gple_ref[...]).astype(BF)
        plan.add(("norm2", t), "V", 650, (last_dn,), t_norm2)

        def t_ple(t=t, rows=rows):
            v["pe", t] = _dot(p_ref[rows, :].astype(BF), wple_ref[...])
        plan.add(("ple", t), "M", 300, (), t_ple)

        def t_wpg(t=t):
            v["gpre", t] = _dot(v["xgb", t], wpg_ref[...])
        plan.add(("wpg", t), "M", 1024, (("norm2", t),), t_wpg)

        def t_fin(t=t, rows=rows):
            x3 = v["x2", t] + _sigmoid(v["gpre", t]) * v["pe", t]
            y_ref[rows, :] = _rms(x3, gfin_ref[...])
        plan.add(("fin", t), "V", 1000, (("wpg", t), ("ple", t)), t_fin)

    base = pl.program_id(0) * STATE_BATCH
    heads = [slice(h * M_HD, (h + 1) * M_HD) for h in range(M_HEADS)]
    for bb in range(STATE_BATCH):
        def t_cols(bb=bb):
            row = pl.ds(base + bb, 1)
            q, kw = q_ref[row, :], kw_ref[row, :]
            v["cols", bb] = jnp.concatenate([q[:, hs] for hs in heads] + [kw[:, hs] for hs in heads],
                                            axis=0).T
            v["vrow", bb] = v_ref[row, :]
        plan.add(("cols", bb), "V", 40, (), t_cols)

        for h, hs in enumerate(heads):
            def t_pair(bb=bb, h=h, hs=hs):
                cols = v["cols", bb]
                q_col, kw_col = cols[:, h:h + 1], cols[:, M_HEADS + h:M_HEADS + h + 1]
                c_prev = c_ref[bb, h]
                qc_out[pl.ds(base + bb, 1), hs] = jnp.sum(q_col * c_prev, axis=0, keepdims=True)
                c_out[bb, h] = sc_ref[base + bb, h] * c_prev + kw_col * v["vrow", bb][:, hs]
            plan.add(("pair", bb, h), "V", 170, (("cols", bb),), t_pair)

    plan.run()


def _front_kernel(x_ref, lbuf_ref, h0_ref, mbuf_ref, gmix_ref, wbr_ref, wmg_ref, wif_ref, bif_ref,
                  lcw_ref, lcb_ref, wax_ref, ba_ref, bx_ref, lam_ref, mcw_ref, mcb_ref,
                  wq_ref, wk_ref, wv_ref, m0_ref, n0_ref,
                  q_ref, kw_ref, v_ref, sc_ref, hv_ref, hc_ref, n_out, m_out,
                  yl_ref, gm_ref, lbuf_out, h_out, mbuf_out):
    xnb = _rms(x_ref[...], gmix_ref[...]).astype(BF)

    def per_head(cols):
        return jnp.concatenate([jnp.broadcast_to(cols[:, h:h + 1], (cols.shape[0], M_HD))
                                for h in range(M_HEADS)], axis=1)

    def head_sums(a):
        return jnp.concatenate([jnp.sum(a[:, h * M_HD:(h + 1) * M_HD], axis=1, keepdims=True)
                                for h in range(M_HEADS)], axis=1)

    def conv_step(buf_ref, buf_out, x_new, w_ref, b_ref):
        out = b_ref[...] + x_new * w_ref[CONV_W - 1:CONV_W, :]
        for j in range(CONV_W - 1):
            out = out + buf_ref[j] * w_ref[j:j + 1, :]
        for j in range(CONV_W - 2):
            buf_out[j] = buf_ref[j + 1]
        buf_out[CONV_W - 2] = x_new
        return out

    x_l = _dot(xnb, wbr_ref[:, 0:D_MODEL])
    xl_c = conv_step(lbuf_ref, lbuf_out, x_l, lcw_ref, lcb_ref)
    ga, gx = _lru_gates(xl_c.astype(BF), wax_ref, ba_ref[...], bx_ref[...])
    a, u = _lru_coeffs(xl_c, ga, gx, lam_ref[...])
    y_l = a * h0_ref[...] + u
    h_out[...] = y_l
    g_l = _dot(xnb, wmg_ref[:, 0:D_MODEL])
    yl_ref[...] = _sigmoid(g_l) * y_l

    x_m = _dot(xnb, wbr_ref[:, D_MODEL:2 * D_MODEL])
    xm_c = conv_step(mbuf_ref, mbuf_out, x_m, mcw_ref, mcb_ref)
    xcb = (xm_c * _sigmoid(xm_c)).astype(BF)
    xmb = x_m.astype(BF)
    q, k, v = [], [], []
    for h in range(M_HEADS):
        hs = slice(h * M_HD, (h + 1) * M_HD)
        q.append(_dot(xcb[:, hs], wq_ref[h]) * (M_HD ** -0.5))
        k.append(_dot(xcb[:, hs], wk_ref[h]))
        v.append(_dot(xmb[:, hs], wv_ref[h]))
    q, k, v = (jnp.concatenate(a, axis=1) for a in (q, k, v))
    o_m = _dot(xnb, wbr_ref[:, 2 * D_MODEL:3 * D_MODEL])
    g_m = _dot(xnb, wmg_ref[:, D_MODEL:2 * D_MODEL])
    gm_ref[...] = _sigmoid(g_m) * _sigmoid(o_m)

    pre = _dot(xnb, wif_ref[...]) + bif_ref[...]
    ig = pre[:, 0:M_HEADS]
    lf = -_softplus(-pre[:, M_HEADS:N_GATES])
    n_prev = n0_ref[...]
    m_inter = lf + m0_ref[...]
    m_t = jnp.maximum(m_inter, ig)
    wk = jnp.exp(ig - m_t)
    sc = jnp.exp(m_inter - m_t)
    s = head_sums(q * k) * wk
    den = s + sc * head_sums(q * n_prev)
    rden = 1.0 / jnp.maximum(jnp.abs(den), jnp.exp(-m_t))
    wk_d, sc_d = per_head(wk), per_head(sc)
    hv_ref[...] = per_head(s * rden) * v
    hc_ref[...] = per_head(sc * rden)
    q_ref[...] = q
    kw_ref[...] = wk_d * k
    v_ref[...] = v
    sc_ref[...] = sc
    n_out[...] = sc_d * n_prev + wk_d * k
    m_out[...] = m_t


def _back_kernel(x_ref, qc_ref, hv_ref, hc_ref, yl_ref, gm_ref, p_ref, mg_ref, wout_ref, gffn_ref,
                 wg_ref, wu_ref, wd_ref, gple_ref, wpg_ref, wple_ref, gfin_ref, y_ref, mrg_ref):
    for h in range(M_HEADS):
        hs = slice(h * M_HD, (h + 1) * M_HD)
        hh = hv_ref[:, hs] + hc_ref[:, hs] * qc_ref[:, hs]
        hn = hh * lax.rsqrt(jnp.mean(hh * hh, axis=-1, keepdims=True) + EPS) * mg_ref[:, hs]
        mrg_ref[:, hs] = (yl_ref[:, hs] + gm_ref[:, hs] * hn).astype(BF)
    x1 = x_ref[...] + _dot(mrg_ref[...], wout_ref[...])
    y_ref[...] = _ffn_tail(x1, p_ref[...], gffn_ref, wg_ref, wu_ref, wd_ref, gple_ref, wpg_ref,
                           wple_ref, gfin_ref)


def _resident(shape):
    nd = len(shape)
    return pl.BlockSpec(shape, lambda *_: (0,) * nd, pipeline_mode=pl.Buffered(1))


def _params(n_axes):
    return pltpu.CompilerParams(dimension_semantics=("arbitrary",) * n_axes,
                                vmem_limit_bytes=VMEM_LIMIT)


def _interleave_matrix():
    r = np.arange(SEQ_TILE)
    perm = np.zeros((SEQ_TILE, SEQ_TILE), np.float32)
    perm[r, (r % SUBLANES) * SEG + r // SUBLANES] = 1.0
    return perm


def kernel(x_prompt, x_sample, state_lru_conv, state_lru_h, state_mlstm_conv, state_mlstm_C, state_mlstm_n, state_mlstm_m, p_prompt, p_sample, norm_mix_g, w_in, b_gates, lru_conv_w, lru_conv_b, lru_w_a, lru_b_a, lru_w_x, lru_b_x, lru_lambda, mlstm_conv_w, mlstm_conv_b, w_q, w_k, w_v, mlstm_norm_g, w_out, norm_ffn_g, w_ffn_gate, w_ffn_up, w_ffn_down, norm_ple_g, w_ple_gate, w_ple, final_norm_g):
    assert w_in.shape[0] == 1, "single-layer trunk"
    B, T, _ = x_prompt.shape
    S = x_sample.shape[0]
    step = SEQ_TILE * SUB_TILES
    assert T % step == 0 and (B * T) % FFN_TILE == 0 and x_sample.shape[1] == 1
    assert S == STATE_BATCH * ((B * T) // FFN_TILE), "one state batch per ffn grid step"

    sds = jax.ShapeDtypeStruct
    assert w_in.shape[1:] == (D_MODEL, 5 * D_MODEL + N_GATES)
    wt = jnp.swapaxes(w_in[0], 0, 1)
    g0 = 3 * D_MODEL
    src_row = lambda j: pl.multiple_of(
        jnp.where(j < BRANCH_STEPS, j * PREP_COLS, g0 + N_GATES + (j - BRANCH_STEPS) * PREP_COLS),
        SUBLANES)
    slab = lambda rows, index: pl.BlockSpec((pl.Element(rows), pl.Element(D_MODEL)), index)
    cols = lambda index: pl.BlockSpec((D_MODEL, PREP_COLS), index)
    wbr, wmg, wif = pl.pallas_call(
        _split_kernel,
        grid=(5 * D_MODEL // PREP_COLS,),
        in_specs=[slab(PREP_COLS, lambda j: (src_row(j), 0)), slab(GATE_PAD, lambda j: (g0, 0))],
        out_specs=[cols(lambda j: (0, jnp.minimum(j, BRANCH_STEPS - 1))),
                   cols(lambda j: (0, jnp.maximum(j - BRANCH_STEPS, 0))),
                   pl.BlockSpec((D_MODEL, GATE_PAD), lambda j: (0, 0))],
        out_shape=[sds((D_MODEL, 3 * D_MODEL), BF), sds((D_MODEL, 2 * D_MODEL), BF),
                   sds((D_MODEL, GATE_PAD), BF)],
        compiler_params=_params(1),
        name="split_w_in",
    )(wt, wt)
    bif = jnp.pad(b_gates[0], (0, GATE_PAD - N_GATES)).reshape(1, GATE_PAD)
    wax = jnp.concatenate([lru_w_a[0], lru_w_x[0]], axis=2).astype(BF)
    row = lambda a: a.reshape(1, -1)
    gmix, ba, bx, lam = row(norm_mix_g[0]), row(lru_b_a[0]), row(lru_b_x[0]), row(lru_lambda[0])
    lcw, lcb = lru_conv_w[0], row(lru_conv_b[0])
    mcw, mcb = mlstm_conv_w[0], row(mlstm_conv_b[0])
    wq, wk, wv = w_q[0].astype(BF), w_k[0].astype(BF), w_v[0].astype(BF)
    mg = row(mlstm_norm_g[0])
    wout = w_out[0].astype(BF)
    gffn, gple, gfin = row(norm_ffn_g[0]), row(norm_ple_g[0]), row(final_norm_g)
    wg, wu, wd = w_ffn_gate[0].astype(BF), w_ffn_up[0].astype(BF), w_ffn_down[0].astype(BF)
    wpg, wple = w_ple_gate[0].astype(BF), w_ple[0].astype(BF)
    perm_np = _interleave_matrix()
    perm, perm_t = jnp.asarray(perm_np, BF), jnp.asarray(perm_np.T, BF)

    front_w = (gmix, wbr, wmg, wif, bif, lcw, lcb, wax, ba, bx, lam, mcw, mcb, wq, wk, wv)
    mixer_w = (perm, perm_t) + front_w + (mg, wout)
    ffn_w = (gffn, wg, wu, wd, gple, wpg, wple, gfin)

    nt = T // step
    x1, p_lconv, p_h, p_mconv, p_c, p_n, p_m = pl.pallas_call(
        _mixer_kernel,
        grid=(B, nt),
        in_specs=[pl.BlockSpec((1, step, D_MODEL), lambda b, t: (b, t, 0))]
                 + [_resident(w.shape) for w in mixer_w],
        out_specs=[
            pl.BlockSpec((1, step, D_MODEL), lambda b, t: (b, t, 0)),
            pl.BlockSpec((1, CONV_W - 1, D_MODEL), lambda b, t: (b, 0, 0)),
            pl.BlockSpec((1, 1, D_MODEL), lambda b, t: (b, 0, 0)),
            pl.BlockSpec((1, CONV_W - 1, D_MODEL), lambda b, t: (b, 0, 0)),
            pl.BlockSpec((1, M_HEADS, M_HD, M_HD), lambda b, t: (b, 0, 0, 0)),
            pl.BlockSpec((1, M_HEADS, M_HD), lambda b, t: (b, 0, 0)),
            pl.BlockSpec((1, 1, M_HEADS), lambda b, t: (b, 0, 0)),
        ],
        out_shape=[
            sds((B, T, D_MODEL), F32),
            sds((B, CONV_W - 1, D_MODEL), F32),
            sds((B, 1, D_MODEL), F32),
            sds((B, CONV_W - 1, D_MODEL), F32),
            sds((B, M_HEADS, M_HD, M_HD), F32),
            sds((B, M_HEADS, M_HD), F32),
            sds((B, 1, M_HEADS), F32),
        ],
        compiler_params=_params(2),
        name="prompt_mixer",
    )(x_prompt, *mixer_w)

    xs = x_sample.reshape(S, D_MODEL)
    lbuf = jnp.swapaxes(state_lru_conv[0], 0, 1)
    mbuf = jnp.swapaxes(state_mlstm_conv[0], 0, 1)
    tok = sds((S, D_MODEL), F32)
    buf = sds((CONV_W - 1, S, D_MODEL), F32)
    per_head = sds((S, M_HEADS), F32)
    q, kw, v, sc, hv, hc, s_n, s_m, yl, gm, s_lbuf, s_h, s_mbuf = pl.pallas_call(
        _front_kernel,
        out_shape=[tok, tok, tok, per_head, tok, tok, tok, per_head, tok, tok, buf, tok, buf],
        compiler_params=pltpu.CompilerParams(vmem_limit_bytes=VMEM_LIMIT),
        name="sample_front",
    )(xs, lbuf, state_lru_h[0], mbuf, *front_w, state_mlstm_m[0],
      state_mlstm_n[0].reshape(S, D_MODEL))

    n_tok = B * T
    tile = lambda width: pl.BlockSpec((FFN_TILE, width), lambda i: (i, 0))
    c_spec = pl.BlockSpec((STATE_BATCH, M_HEADS, M_HD, M_HD), lambda i: (i, 0, 0, 0))
    y_prompt, qc, s_c = pl.pallas_call(
        _ffn_kernel,
        grid=(n_tok // FFN_TILE,),
        in_specs=[pl.BlockSpec(memory_space=pltpu.SMEM), tile(D_MODEL), tile(P_DIM),
                  _resident(tok.shape), _resident(tok.shape), _resident(tok.shape), c_spec]
                 + [_resident(w.shape) for w in ffn_w],
        out_specs=[tile(D_MODEL), pl.BlockSpec(tok.shape, lambda i: (0, 0)), c_spec],
        out_shape=[sds((n_tok, D_MODEL), F32), tok, sds((S, M_HEADS, M_HD, M_HD), F32)],
        compiler_params=_params(1),
        name="prompt_ffn",
    )(sc, x1.reshape(n_tok, D_MODEL), p_prompt[0].reshape(n_tok, P_DIM), q, kw, v,
      state_mlstm_C[0], *ffn_w)

    y_sample = pl.pallas_call(
        _back_kernel,
        out_shape=tok,
        scratch_shapes=[pltpu.VMEM((S, D_MODEL), BF)],
        compiler_params=pltpu.CompilerParams(vmem_limit_bytes=VMEM_LIMIT),
        name="sample_back",
    )(xs, qc, hv, hc, yl, gm, p_sample[0].reshape(S, P_DIM), mg, wout, *ffn_w)

    lead = lambda a: a[None]
    return (y_prompt.reshape(B, T, D_MODEL), y_sample.reshape(S, 1, D_MODEL),
            lead(p_lconv), p_h.reshape(1, B, D_MODEL), lead(p_mconv), lead(p_c), lead(p_n),
            p_m.reshape(1, B, M_HEADS),
            lead(jnp.swapaxes(s_lbuf, 0, 1)), lead(s_h), lead(jnp.swapaxes(s_mbuf, 0, 1)),
            lead(s_c), s_n.reshape(1, S, M_HEADS, M_HD), lead(s_m))
```
